```python
import math
import jax, jax.numpy as jnp
from jax import lax
import numpy as np

D_MODEL = 1024
BATCH = 8
SEQ = 2048
DEPTH = 2
DEC_BATCH = 128
DEC_SEQ = 8
PAST_LEN = 16384
PAGE_SIZE = 128

N_AB = (DEPTH + 1) // 2
N_C = DEPTH // 2
D_A = D_MODEL // 2
CONV_A = 3
H_B = 4
DK = 128
DV = 128
D_QK = H_B * DK
D_VB = H_B * DV
D_QKV = 2 * D_QK + D_VB
CONV_B = 4
DN_CHUNK = 64
W_AB = 3 * D_A + D_QKV + D_VB + 2 * H_B
D_C = D_MODEL
C_GROUPS = 8
C_CHUNK = 128
N_EXP = 32
TOP_K = 4
D_EXP = D_MODEL
SWIGLU_ALPHA = 1.702
SWIGLU_LIMIT = 7.0
MOE_BLOCK = 128
DEEPNORM_ALPHA = (2 * DEPTH) ** 0.25
DEEPNORM_BETA = (8 * DEPTH) ** -0.25
LN_EPS = 1e-5
RMS_EPS = 1e-6

kernel_name = 'hybrid_shortconv_gdn_chunkmlp_moe_step'


def layer_norm(x, g, b):
    xf = x.astype(jnp.float32)
    mu = jnp.mean(xf, axis=-1, keepdims=True)
    var = jnp.mean(jnp.square(xf - mu), axis=-1, keepdims=True)
    return ((xf - mu) * lax.rsqrt(var + LN_EPS) * g + b).astype(x.dtype)


def post_norm(x, h, g, b):
    return layer_norm(DEEPNORM_ALPHA * x + h, g, b)


def causal_conv(x, buf, w):
    k = w.shape[0]
    t = x.shape[1]
    xp = jnp.concatenate([buf.astype(x.dtype), x], axis=1)
    y = sum(xp[:, j:j + t] * w[j] for j in range(k))
    return y, xp[:, t:]


def l2norm(x):
    return x * lax.rsqrt(jnp.sum(jnp.square(x), axis=-1, keepdims=True) + RMS_EPS)


def gated_delta_rule(q, k, v, g, beta, s0):
    f32 = jnp.float32
    bsz, t = q.shape[0], q.shape[1]
    c = min(DN_CHUNK, t)
    n = -(-t // c)
    pad = n * c - t
    q = l2norm(q.astype(f32)) * (DK ** -0.5)
    k = l2norm(k.astype(f32))
    v = v.astype(f32)

    def blocks(a):
        a = jnp.pad(a.astype(f32), [(0, 0), (0, pad)] + [(0, 0)] * (a.ndim - 2))
        a = a.reshape((bsz, n, c) + a.shape[2:])
        return jnp.transpose(a, (1, 0, 3, 2) + tuple(range(4, a.ndim)))

    qc, kc, vc, bc = blocks(q), blocks(k), blocks(v), blocks(beta)
    gc = jnp.cumsum(blocks(g), axis=-1)
    idx = jnp.arange(c)
    incl = idx[:, None] >= idx[None, :]
    strict = idx[:, None] > idx[None, :]
    diff = gc[..., :, None] - gc[..., None, :]
    decay = jnp.where(incl, jnp.exp(jnp.where(incl, diff, 0.0)), 0.0)
    kb = kc * bc[..., None]
    lmat = jnp.where(strict, jnp.einsum('nbhid,nbhjd->nbhij', kb, kc) * decay, 0.0)
    eye = jnp.broadcast_to(jnp.eye(c, dtype=f32), lmat.shape)
    tmat = lax.linalg.triangular_solve(lmat + eye, eye, left_side=True, lower=True,
                                       unit_diagonal=True)
    u = tmat @ (vc * bc[..., None])
    w = tmat @ (kb * jnp.exp(gc)[..., None])
    attn = jnp.einsum('nbhid,nbhjd->nbhij', qc, kc) * decay

    def step(s, xs):
        q_n, k_n, u_n, w_n, a_n, g_n = xs
        v_new = u_n - w_n @ s
        o = (q_n * jnp.exp(g_n)[..., None]) @ s + a_n @ v_new
        g_last = g_n[..., -1:]
        s = s * jnp.exp(g_last)[..., None] + jnp.einsum(
            'bhcd,bhce->bhde', k_n * jnp.exp(g_last - g_n)[..., None], v_new)
        return s, o

    s, o = lax.scan(step, s0.astype(f32), (qc, kc, u, w, attn, gc))
    o = jnp.transpose(o, (1, 0, 3, 2, 4)).reshape(bsz, n * c, H_B, DV)[:, :t]
    return o, s


def mixer_ab(x, buf_a, buf_qkv, s0, w_in, conv_a, conv_qkv, a_log, dt_bias, norm_g, w_out):
    f32 = jnp.float32
    bsz, t, _ = x.shape
    proj = x @ w_in
    o1 = D_A
    o2 = 2 * D_A
    o3 = 3 * D_A
    o4 = o3 + D_QKV
    o5 = o4 + D_VB
    o6 = o5 + H_B
    bg, cg, h, qkv, z, a, b = jnp.split(proj, [o1, o2, o3, o4, o5, o6], axis=-1)
    ca, new_buf_a = causal_conv(cg * h, buf_a, conv_a)
    ya = bg * ca
    qkv_c, new_buf_qkv = causal_conv(qkv, buf_qkv, conv_qkv)
    qkv_c = jax.nn.silu(qkv_c)
    q, k, v = jnp.split(qkv_c, [D_QK, 2 * D_QK], axis=-1)
    q = q.reshape(bsz, t, H_B, DK)
    k = k.reshape(bsz, t, H_B, DK)
    v = v.reshape(bsz, t, H_B, DV)
    g = -jnp.exp(a_log.astype(f32)) * jax.nn.softplus(a.astype(f32) + dt_bias.astype(f32))
    beta = jax.nn.sigmoid(b.astype(f32))
    o, s_new = gated_delta_rule(q, k, v, g, beta, s0)
    o = o * lax.rsqrt(jnp.mean(jnp.square(o), axis=-1, keepdims=True) + RMS_EPS) * norm_g.astype(f32)
    yb = (o.reshape(bsz, t, D_VB) * jax.nn.silu(z.astype(f32))).astype(x.dtype)
    y = jnp.concatenate([ya, yb], axis=-1) @ w_out
    return y, new_buf_a, new_buf_qkv, s_new


def mixer_c(x, w_in, b_in, ln_g, ln_b, w_s, b_s, w_out):
    bsz, t, _ = x.shape
    hcat = jax.nn.gelu(x @ w_in + b_in, approximate=False)
    u, v = jnp.split(hcat, 2, axis=-1)
    v = layer_norm(v, ln_g, ln_b)
    n = -(-t // C_CHUNK)
    pad = n * C_CHUNK - t
    vc = jnp.pad(v, ((0, 0), (0, pad), (0, 0))).reshape(bsz, n, C_CHUNK, C_GROUPS, D_C // C_GROUPS)
    w_causal = jnp.tril(w_s)
    s = jnp.einsum('gij,bnjgc->bnigc', w_causal, vc) + b_s.T[:, :, None]
    s = s.reshape(bsz, n * C_CHUNK, D_C)[:, :t]
    y = (u * s) @ w_out
    return y, v


def moe(x, w_r, b_r, w1, b1, w2, b2):
    f32 = jnp.float32
    shp = x.shape
    xt = x.reshape(-1, D_MODEL)
    n_tok = xt.shape[0]
    logits = (xt @ w_r + b_r).astype(f32)
    top_v, top_i = lax.top_k(logits, TOP_K)
    gates = jax.nn.softmax(top_v, axis=-1)
    m = n_tok * TOP_K
    e_flat = top_i.reshape(m)
    tok_flat = jnp.repeat(jnp.arange(n_tok, dtype=jnp.int32), TOP_K)
    g_flat = gates.reshape(m)
    order = jnp.argsort(e_flat)
    e_s = e_flat[order]
    counts = jnp.zeros((N_EXP,), jnp.int32).at[e_flat].add(1)
    starts = jnp.cumsum(counts) - counts
    pcounts = (counts + MOE_BLOCK - 1) // MOE_BLOCK * MOE_BLOCK
    pends = jnp.cumsum(pcounts)
    pstarts = pends - pcounts
    dest = pstarts[e_s] + jnp.arange(m, dtype=jnp.int32) - starts[e_s]
    n_blocks = (m + N_EXP * (MOE_BLOCK - 1)) // MOE_BLOCK
    rows = n_blocks * MOE_BLOCK
    row_tok = jnp.zeros((rows,), jnp.int32).at[dest].set(tok_flat[order])
    row_gate = jnp.zeros((rows,), f32).at[dest].set(g_flat[order])
    block_exp = jnp.minimum(
        jnp.searchsorted(pends, jnp.arange(n_blocks, dtype=jnp.int32) * MOE_BLOCK, side='right'),
        N_EXP - 1)
    x_rows = xt[row_tok].reshape(n_blocks, MOE_BLOCK, D_MODEL)

    def expert_block(args):
        xb, e = args
        hdn = xb @ w1[e] + b1[e]
        glu, lin = jnp.split(hdn, 2, axis=-1)
        glu = jnp.minimum(glu, SWIGLU_LIMIT)
        lin = jnp.clip(lin, -SWIGLU_LIMIT, SWIGLU_LIMIT)
        act = glu * jax.nn.sigmoid(SWIGLU_ALPHA * glu) * (lin + 1.0)
        return act @ w2[e] + b2[e]

    out_rows = lax.map(expert_block, (x_rows, block_exp)).reshape(rows, D_MODEL)
    y = jnp.zeros((n_tok, D_MODEL), f32).at[row_tok].add(out_rows.astype(f32) * row_gate[:, None])
    return y.astype(x.dtype).reshape(shp)


def setup_inputs(seed: int = 0) -> dict:
    key = jax.random.key(seed)
    ks = iter(jax.random.split(key, 32))
    f32 = jnp.float32

    def nrm(shape, scale):
        return jax.random.normal(next(ks), shape, f32) * scale

    dt = jnp.exp(jax.random.uniform(next(ks), (N_AB, H_B), f32, math.log(1e-3), math.log(1e-1)))
    a_log = jnp.log(jax.random.uniform(next(ks), (N_AB, H_B), f32, 1.0, 16.0))
    return {
        'x_prompt': nrm((BATCH, SEQ, D_MODEL), 1.0),
        'x_sample': nrm((DEC_BATCH, DEC_SEQ, D_MODEL), 1.0),
        'state_conv_a': nrm((N_AB, DEC_BATCH, CONV_A - 1, D_A), 0.5),
        'state_conv_qkv': nrm((N_AB, DEC_BATCH, CONV_B - 1, D_QKV), 0.5),
        'state_delta': nrm((N_AB, DEC_BATCH, H_B, DK, DV), DK ** -0.5),
        'ab_w_in': nrm((N_AB, D_MODEL, W_AB), D_MODEL ** -0.5),
        'ab_conv_a': nrm((N_AB, CONV_A, D_A), CONV_A ** -0.5),
        'ab_conv_qkv': nrm((N_AB, CONV_B, D_QKV), CONV_B ** -0.5),
        'ab_a_log': a_log,
        'ab_dt_bias': dt + jnp.log(-jnp.expm1(-dt)),
        'ab_norm_g': 1.0 + nrm((N_AB, DV), 0.02),
        'ab_w_out': nrm((N_AB, D_A + D_VB, D_MODEL), (D_A + D_VB) ** -0.5 * DEEPNORM_BETA),
        'c_w_in': nrm((N_C, D_MODEL, 2 * D_C), D_MODEL ** -0.5),
        'c_b_in': nrm((N_C, 2 * D_C), 0.02),
        'c_ln_g': 1.0 + nrm((N_C, D_C), 0.02),
        'c_ln_b': nrm((N_C, D_C), 0.02),
        'c_w_s': nrm((N_C, C_GROUPS, C_CHUNK, C_CHUNK), C_CHUNK ** -0.5),
        'c_b_s': 1.0 + nrm((N_C, C_GROUPS, C_CHUNK), 0.02),
        'c_w_out': nrm((N_C, D_C, D_MODEL), D_C ** -0.5 * DEEPNORM_BETA),
        'moe_w_router': nrm((DEPTH, D_MODEL, N_EXP), D_MODEL ** -0.5),
        'moe_b_router': nrm((DEPTH, N_EXP), 0.01),
        'moe_w1': nrm((DEPTH, N_EXP, D_MODEL, 2 * D_EXP), D_MODEL ** -0.5),
        'moe_b1': nrm((DEPTH, N_EXP, 2 * D_EXP), 0.02),
        'moe_w2': nrm((DEPTH, N_EXP, D_EXP, D_MODEL), D_EXP ** -0.5 * DEEPNORM_BETA),
        'moe_b2': nrm((DEPTH, N_EXP, D_MODEL), 0.02 * DEEPNORM_BETA),
        'ln_g': 1.0 + nrm((DEPTH, 2, D_MODEL), 0.02),
        'ln_b': nrm((DEPTH, 2, D_MODEL), 0.02),
    }


def reference(x_prompt, x_sample, state_conv_a, state_conv_qkv, state_delta,
              ab_w_in, ab_conv_a, ab_conv_qkv, ab_a_log, ab_dt_bias, ab_norm_g, ab_w_out,
              c_w_in, c_b_in, c_ln_g, c_ln_b, c_w_s, c_b_s, c_w_out,
              moe_w_router, moe_b_router, moe_w1, moe_b1, moe_w2, moe_b2, ln_g, ln_b):
    xp, xs = x_prompt, x_sample
    p_conv_a, p_conv_qkv, p_delta = [], [], []
    s_conv_a, s_conv_qkv, s_delta, s_chunk_v = [], [], [], []
    for layer in range(DEPTH):
        i = layer // 2
        if layer % 2 == 0:
            ab = (ab_w_in[i], ab_conv_a[i], ab_conv_qkv[i], ab_a_log[i], ab_dt_bias[i],
                  ab_norm_g[i], ab_w_out[i])
            nb = xp.shape[0]
            zero_a = jnp.zeros((nb, CONV_A - 1, D_A), xp.dtype)
            zero_qkv = jnp.zeros((nb, CONV_B - 1, D_QKV), xp.dtype)
            zero_s = jnp.zeros((nb, H_B, DK, DV), jnp.float32)
            hp, na, nq, ns = mixer_ab(xp, zero_a, zero_qkv, zero_s, *ab)
            hs, ma, mq, ms = mixer_ab(xs, state_conv_a[i], state_conv_qkv[i], state_delta[i], *ab)
            p_conv_a.append(na)
            p_conv_qkv.append(nq)
            p_delta.append(ns)
            s_conv_a.append(ma)
            s_conv_qkv.append(mq)
            s_delta.append(ms)
        else:
            cp = (c_w_in[i], c_b_in[i], c_ln_g[i], c_ln_b[i], c_w_s[i], c_b_s[i], c_w_out[i])
            hp, _ = mixer_c(xp, *cp)
            hs, vs = mixer_c(xs, *cp)
            s_chunk_v.append(vs)
        xp = post_norm(xp, hp, ln_g[layer, 0], ln_b[layer, 0])
        xs = post_norm(xs, hs, ln_g[layer, 0], ln_b[layer, 0])
        ffn = (moe_w_router[layer], moe_b_router[layer], moe_w1[layer], moe_b1[layer],
               moe_w2[layer], moe_b2[layer])
        xp = post_norm(xp, moe(xp, *ffn), ln_g[layer, 1], ln_b[layer, 1])
        xs = post_norm(xs, moe(xs, *ffn), ln_g[layer, 1], ln_b[layer, 1])
    return (xp, xs,
            jnp.stack(p_conv_a), jnp.stack(p_conv_qkv), jnp.stack(p_delta),
            jnp.stack(s_conv_a), jnp.stack(s_conv_qkv), jnp.stack(s_delta),
            jnp.stack(s_chunk_v))
```

```python
import functools
import math

import jax
import jax.numpy as jnp
from jax import lax
from jax.experimental import pallas as pl
from jax.experimental.pallas import tpu as pltpu

F32 = jnp.float32
BF16 = jnp.bfloat16
I32 = jnp.int32

D_MODEL = 1024
BATCH = 8
SEQ = 2048
DEC_BATCH = 128
DEC_SEQ = 8
N_PROMPT = BATCH * SEQ
N_SAMPLE = DEC_BATCH * DEC_SEQ
N_TOK = N_PROMPT + N_SAMPLE
D_A = 512
CONV_A = 3
H_B = 4
DK = 128
DV = 128
D_QKV = 1536
CONV_B = 4
DN_CHUNK = 64
W_MAIN = 3 * D_A + D_QKV + H_B * DV
D_C = 1024
C_GROUPS = 8
C_CHUNK = 128
N_EXP = 32
TOP_K = 4
D_EXP = 1024
SWIGLU_ALPHA = 1.702
SWIGLU_LIMIT = 7.0
DEEPNORM_ALPHA = 4.0 ** 0.25
LN_EPS = 1e-5
RMS_EPS = 1e-6

LANES = 128
SUBLANES = 8
VMEM_LIMIT = 56 * 1024 * 1024

TM = 512
TM_E = 256
TM_C = 256
N_PAIRS = N_TOK * TOP_K
ROWS_MAX = -(-(N_PAIRS + N_EXP * (TM_E - 1)) // TM_E) * TM_E
N_EBLOCKS = ROWS_MAX // TM_E


def _cparams(sem):
    return pltpu.CompilerParams(dimension_semantics=sem, vmem_limit_bytes=VMEM_LIMIT)


def _layer_norm(t, g, b):
    mu = jnp.mean(t, axis=-1, keepdims=True)
    d = t - mu
    var = jnp.mean(d * d, axis=-1, keepdims=True)
    return d * lax.rsqrt(var + LN_EPS) * g + b


def _bdot(a, b):
    return jnp.dot(a.astype(BF16), b.astype(BF16), preferred_element_type=F32)


def _proj_ab_body(x_ref, w_ref, wab_ref, alog_ref, dtb_ref, bg_ref, u_ref, qkv_ref, z_ref, gb_ref):
    xb = x_ref[...].astype(BF16)

    def mm(lo, hi):
        return jnp.dot(xb, w_ref[:, lo:hi], preferred_element_type=F32)

    bg_ref[...] = mm(0, D_A)
    u_ref[...] = mm(D_A, 2 * D_A) * mm(2 * D_A, 3 * D_A)
    for c in range(D_QKV // 512):
        qkv_ref[:, c * 512:(c + 1) * 512] = mm(3 * D_A + c * 512, 3 * D_A + (c + 1) * 512)
    z_ref[...] = mm(3 * D_A + D_QKV, W_MAIN)
    ab = jnp.dot(xb, wab_ref[...], preferred_element_type=F32)
    g = -jnp.exp(alog_ref[...]) * jax.nn.softplus(ab + dtb_ref[...])
    beta = jax.nn.sigmoid(ab)
    lane = lax.broadcasted_iota(I32, ab.shape, 1)
    gb_ref[...] = jnp.where(lane < H_B, g, beta)


def _proj_ab(x, w_main, w_ab, alog_row, dtb_row):
    n = x.shape[0]
    row = lambda w: pl.BlockSpec((TM, w), lambda i: (i, 0))
    full = lambda a: pl.BlockSpec(a.shape, lambda i: (0,) * a.ndim)
    return pl.pallas_call(
        _proj_ab_body,
        grid=(n // TM,),
        in_specs=[row(D_MODEL), full(w_main), full(w_ab), full(alog_row), full(dtb_row)],
        out_specs=[row(D_A), row(D_A), row(D_QKV), row(D_A), row(LANES)],
        out_shape=[jax.ShapeDtypeStruct((n, D_A), F32), jax.ShapeDtypeStruct((n, D_A), F32),
                   jax.ShapeDtypeStruct((n, D_QKV), F32), jax.ShapeDtypeStruct((n, D_A), F32),
                   jax.ShapeDtypeStruct((n, LANES), F32)],
        compiler_params=_cparams(("arbitrary",)),
        name="proj_ab",
    )(x, w_main, w_ab, alog_row, dtb_row)


def _shift_rows(x, prev8, s):
    if s == 0:
        return x
    xr = pltpu.roll(x, s, axis=0)
    pr = pltpu.roll(prev8, s, axis=0)
    rid = lax.broadcasted_iota(I32, pr.shape, 0)
    head = jnp.where(rid < s, pr, xr[0:SUBLANES])
    if x.shape[0] == SUBLANES:
        return head
    return jnp.concatenate([head, xr[SUBLANES:]], axis=0)


def _causal_conv(x, prev8, w_ref, taps):
    y = x * w_ref[taps - 1:taps, :]
    for s in range(1, taps):
        y = y + _shift_rows(x, prev8, s) * w_ref[taps - 1 - s:taps - s, :]
    return y


def _lane_scan(x, pos, chunk, reverse):
    s = 1
    while s < chunk:
        if reverse:
            x = x + jnp.where(pos < chunk - s, pltpu.roll(x, LANES - s, axis=1), 0.0)
        else:
            x = x + jnp.where(pos >= s, pltpu.roll(x, s, axis=1), 0.0)
        s *= 2
    return x


def _per_row(row):
    return jnp.broadcast_to(row, (LANES, LANES)).T


def _gdn_body(chunk, bb, n_chunks,
              qkv_ref, u_ref, bg_ref, z_ref, gbr_ref, pq_ref, pu_ref, s0_ref, wq_ref, wa_ref, ng_ref, y_any,
              ycat_ref, nq_ref, nu_ref, sn_ref, cq_scr, cu_scr, s_scr):
    del y_any
    n = pl.program_id(1)

    @pl.when(n == 0)
    def _():
        cq_scr[...] = pq_ref[...]
        cu_scr[...] = pu_ref[...]
        s_scr[...] = s0_ref[...]

    gsz = LANES // chunk
    n_groups = bb * H_B // gsz
    levels = int(math.log2(chunk))

    ii = lax.broadcasted_iota(I32, (LANES, LANES), 0)
    jj = lax.broadcasted_iota(I32, (LANES, LANES), 1)
    same = (ii // chunk) == (jj // chunk)
    m_incl = same & (ii >= jj)
    m_strict = same & (ii > jj)
    eye = (ii == jj).astype(F32)
    pos = lax.broadcasted_iota(I32, (SUBLANES, LANES), 1) % chunk

    qs, ks, vs = {}, {}, {}
    for b in range(bb):
        rows = slice(b * chunk, (b + 1) * chunk)
        x = qkv_ref[rows, :]
        qc = _causal_conv(x, cq_scr[b], wq_ref, CONV_B)
        qc = qc * jax.nn.sigmoid(qc)
        cq_scr[b] = x[chunk - SUBLANES:chunk]
        uu = u_ref[rows, :]
        ca = _causal_conv(uu, cu_scr[b], wa_ref, CONV_A)
        cu_scr[b] = uu[chunk - SUBLANES:chunk]
        ycat_ref[rows, 0:D_A] = (bg_ref[rows, :] * ca).astype(BF16)
        for h in range(H_B):
            qh = qc[:, h * DK:(h + 1) * DK]
            kh = qc[:, H_B * DK + h * DK:H_B * DK + (h + 1) * DK]
            vh = qc[:, 2 * H_B * DK + h * DV:2 * H_B * DK + (h + 1) * DV]
            qs[b, h] = qh * (lax.rsqrt(jnp.sum(qh * qh, axis=-1, keepdims=True) + RMS_EPS) * (DK ** -0.5))
            ks[b, h] = kh * lax.rsqrt(jnp.sum(kh * kh, axis=-1, keepdims=True) + RMS_EPS)
            vs[b, h] = vh

    for gi in range(n_groups):
        blocks = [divmod(gi * gsz + t, H_B) for t in range(gsz)]
        cat = lambda d: jnp.concatenate([d[bh] for bh in blocks], axis=0) if gsz > 1 else d[blocks[0]]
        qg, kg, vg = cat(qs), cat(ks), cat(vs)

        tile = gbr_ref[gi]
        gc = _lane_scan(tile, pos, chunk, False)
        rs = _lane_scan(tile, pos, chunk, True) - tile
        gc_row = gc[0:1]
        gc_m = _per_row(gc_row)
        rs_m = _per_row(rs[0:1])
        beta_m = _per_row(tile[1:2])
        diff = gc_m - jnp.broadcast_to(gc_row, (LANES, LANES))
        decay = jnp.where(m_incl, jnp.exp(jnp.where(m_incl, diff, 0.0)), 0.0)
        eg = jnp.exp(gc_m)
        etot = jnp.exp(gc_m + rs_m)

        kb = kg * beta_m
        kgb = kg.astype(BF16)
        a_mat = lax.dot_general(kb.astype(BF16), kgb, (((1,), (1,)), ((), ())), preferred_element_type=F32)
        lm = jnp.where(m_strict, a_mat * decay, 0.0)
        attn = lax.dot_general(qg.astype(BF16), kgb, (((1,), (1,)), ((), ())), preferred_element_type=F32) * decay

        p = eye - lm
        m = _bdot(lm, lm)
        for lvl in range(1, levels):
            p = p + _bdot(p, m)
            if lvl < levels - 1:
                m = _bdot(m, m)
        uw = _bdot(p, jnp.concatenate([vg * beta_m, kb * eg], axis=1))
        u_all, w_all = uw[:, :DV], uw[:, DV:]
        qe = qg * eg
        kdec = kg * jnp.exp(rs_m)

        vnew, qsv = [], []
        for t, (b, h) in enumerate(blocks):
            r = slice(t * chunk, (t + 1) * chunk)
            s_bf = s_scr[b, h].astype(BF16)
            lhs = jnp.concatenate([w_all[r], qe[r]], axis=0).astype(BF16)
            both = jnp.dot(lhs, s_bf, preferred_element_type=F32)
            vnew.append(u_all[r] - both[:chunk])
            qsv.append(both[chunk:])
        vnew_g = jnp.concatenate(vnew, axis=0) if gsz > 1 else vnew[0]
        qs_g = jnp.concatenate(qsv, axis=0) if gsz > 1 else qsv[0]
        o = qs_g + _bdot(attn, vnew_g)
        o = o * lax.rsqrt(jnp.mean(o * o, axis=-1, keepdims=True) + RMS_EPS) * ng_ref[...]

        for t, (b, h) in enumerate(blocks):
            r = slice(t * chunk, (t + 1) * chunk)
            rows = slice(b * chunk, (b + 1) * chunk)
            zz = z_ref[rows, h * DV:(h + 1) * DV]
            ycat_ref[rows, D_A + h * DV:D_A + (h + 1) * DV] = (o[r] * (zz * jax.nn.sigmoid(zz))).astype(BF16)
            upd = lax.dot_general(kdec[r].astype(BF16), vnew[t].astype(BF16), (((0,), (0,)), ((), ())),
                                  preferred_element_type=F32)
            scale = jnp.broadcast_to(etot[t * chunk:t * chunk + 1, :], (DK, DV))
            s_scr[b, h] = s_scr[b, h] * scale + upd

    @pl.when(n == n_chunks - 1)
    def _():
        nq_ref[...] = cq_scr[...]
        nu_ref[...] = cu_scr[...]
        sn_ref[...] = s_scr[...]


def _gdn(qkv, u, bg, z, gbr, prev_q, prev_u, s0, wq, wa, ng, ycat, *, chunk, bb, n_seq, n_chunks, row_block0):
    rb = bb * chunk
    gs = bb * H_B * chunk // LANES
    rowmap = lambda i, n: (row_block0 + i * n_chunks + n, 0)
    row = lambda w: pl.BlockSpec((rb, w), rowmap)
    seq3 = lambda w: pl.BlockSpec((bb, SUBLANES, w), lambda i, n: (i, 0, 0))
    full = lambda a: pl.BlockSpec(a.shape, lambda i, n: (0,) * a.ndim)
    st = pl.BlockSpec((bb, H_B, DK, DV), lambda i, n: (i, 0, 0, 0))
    n_steps = n_seq // bb
    return pl.pallas_call(
        functools.partial(_gdn_body, chunk, bb, n_chunks),
        grid=(n_steps, n_chunks),
        in_specs=[row(D_QKV), row(D_A), row(D_A), row(D_A),
                  pl.BlockSpec((None, gs, SUBLANES, LANES), lambda i, n: (i * n_chunks + n, 0, 0, 0)),
                  seq3(D_QKV), seq3(D_A), st, full(wq), full(wa), full(ng),
                  pl.BlockSpec(memory_space=pl.ANY)],
        out_specs=[pl.BlockSpec((rb, D_MODEL), rowmap), seq3(D_QKV), seq3(D_A), st],
        out_shape=[jax.ShapeDtypeStruct(ycat.shape, BF16),
                   jax.ShapeDtypeStruct((n_seq, SUBLANES, D_QKV), F32),
                   jax.ShapeDtypeStruct((n_seq, SUBLANES, D_A), F32),
                   jax.ShapeDtypeStruct((n_seq, H_B, DK, DV), F32)],
        scratch_shapes=[pltpu.VMEM((bb, SUBLANES, D_QKV), F32), pltpu.VMEM((bb, SUBLANES, D_A), F32),
                        pltpu.VMEM((bb, H_B, DK, DV), F32)],
        input_output_aliases={11: 0},
        compiler_params=_cparams(("arbitrary", "arbitrary")),
        name=f"gdn_c{chunk}",
    )(qkv, u, bg, z, gbr, prev_q, prev_u, s0, wq, wa, ng, ycat)


def _group_rows(gb, n_seq, t_len, chunk, bb):
    n_chunks = t_len // chunk
    gs = bb * H_B * chunk // LANES
    g = gb[:, :2 * H_B].reshape(n_seq // bb, bb, n_chunks, chunk, 2, H_B)
    g = jnp.transpose(g, (0, 2, 4, 1, 5, 3))
    g = g.reshape(n_seq // bb * n_chunks, 2, gs, LANES)
    g = jnp.transpose(g, (0, 2, 1, 3))
    return jnp.pad(g, ((0, 0), (0, 0), (0, SUBLANES - 2), (0, 0)))


def _mm_res_ln_body(y_ref, x_ref, w_ref, g_ref, b_ref, o_ref):
    h = jnp.dot(y_ref[...].astype(BF16), w_ref[...], preferred_element_type=F32)
    o_ref[...] = _layer_norm(DEEPNORM_ALPHA * x_ref[...] + h, g_ref[...], b_ref[...])


def _mm_res_ln(y, x, w, g, b):
    n = x.shape[0]
    row = pl.BlockSpec((TM, D_MODEL), lambda i: (i, 0))
    full = lambda a: pl.BlockSpec(a.shape, lambda i: (0,) * a.ndim)
    return pl.pallas_call(
        _mm_res_ln_body,
        grid=(n // TM,),
        in_specs=[row, row, full(w), full(g), full(b)],
        out_specs=row,
        out_shape=jax.ShapeDtypeStruct((n, D_MODEL), F32),
        compiler_params=_cparams(("arbitrary",)),
        name="mm_res_ln",
    )(y, x, w, g, b)


def _gelu(x):
    return 0.5 * x * (1.0 + lax.erf(x * (2.0 ** -0.5)))


def _c_in_body(x_ref, w_ref, b_ref, lg_ref, lb_ref, u_ref, v_ref):
    xb = x_ref[...].astype(BF16)
    u_ref[...] = _gelu(jnp.dot(xb, w_ref[:, :D_C], preferred_element_type=F32) + b_ref[:, :D_C])
    v = _gelu(jnp.dot(xb, w_ref[:, D_C:], preferred_element_type=F32) + b_ref[:, D_C:])
    v_ref[...] = _layer_norm(v, lg_ref[...], lb_ref[...])


def _c_in(x, w, b, lg, lb):
    n = x.shape[0]
    row = pl.BlockSpec((TM, D_MODEL), lambda i: (i, 0))
    full = lambda a: pl.BlockSpec(a.shape, lambda i: (0,) * a.ndim)
    return pl.pallas_call(
        _c_in_body,
        grid=(n // TM,),
        in_specs=[row, full(w), full(b), full(lg), full(lb)],
        out_specs=[row, row],
        out_shape=[jax.ShapeDtypeStruct((n, D_C), F32), jax.ShapeDtypeStruct((n, D_C), F32)],
        compiler_params=_cparams(("arbitrary",)),
        name="c_in",
    )(x, w, b, lg, lb)


def _c_out_body(u_ref, v_ref, x_ref, wmix_ref, mask_ref, bias_ref, wout_ref, g_ref, b_ref, o_ref, us_scr):
    mix = [(wmix_ref[gi] * mask_ref[...]).astype(BF16) for gi in range(C_GROUPS)]
    for t in range(TM // C_CHUNK):
        rows = slice(t * C_CHUNK, (t + 1) * C_CHUNK)
        for gi in range(C_GROUPS):
            cols = slice(gi * LANES, (gi + 1) * LANES)
            s = jnp.dot(mix[gi], v_ref[rows, cols].astype(BF16), preferred_element_type=F32) + bias_ref[:, cols]
            us_scr[rows, cols] = (u_ref[rows, cols] * s).astype(BF16)
    h = jnp.dot(us_scr[...], wout_ref[...], preferred_element_type=F32)
    o_ref[...] = _layer_norm(DEEPNORM_ALPHA * x_ref[...] + h, g_ref[...], b_ref[...])


def _c_out(u, v, x, wmix2, mask2, bias2, wout, g, b):
    n = x.shape[0]
    first_sample_step = N_PROMPT // TM
    sel = lambda i: jnp.where(i >= first_sample_step, 1, 0)
    row = pl.BlockSpec((TM, D_MODEL), lambda i: (i, 0))
    full = lambda a: pl.BlockSpec(a.shape, lambda i: (0,) * a.ndim)
    return pl.pallas_call(
        _c_out_body,
        grid=(n // TM,),
        in_specs=[row, row, row,
                  pl.BlockSpec((None, C_GROUPS, C_CHUNK, C_CHUNK), lambda i: (sel(i), 0, 0, 0)),
                  pl.BlockSpec((None, C_CHUNK, C_CHUNK), lambda i: (sel(i), 0, 0)),
                  pl.BlockSpec((None, C_CHUNK, D_C), lambda i: (sel(i), 0, 0)),
                  full(wout), full(g), full(b)],
        out_specs=row,
        out_shape=jax.ShapeDtypeStruct((n, D_MODEL), F32),
        scratch_shapes=[pltpu.VMEM((TM, D_C), BF16)],
        compiler_params=_cparams(("arbitrary",)),
        name="c_out",
    )(u, v, x, wmix2, mask2, bias2, wout, g, b)


def _router_body(x_ref, w_ref, b_ref, idx_ref, gate_ref):
    logits = jnp.dot(x_ref[...], w_ref[...], preferred_element_type=F32,
                     precision=lax.Precision.HIGHEST) + b_ref[...]
    lane = lax.broadcasted_iota(I32, logits.shape, 1).astype(F32)
    neg = jnp.float32(-jnp.inf)
    l = jnp.where(lane < N_EXP, logits, neg)
    idx_out = jnp.zeros(logits.shape, F32)
    val_out = jnp.full(logits.shape, neg, F32)
    for k in range(TOP_K):
        m = jnp.max(l, axis=-1, keepdims=True)
        sel = jnp.min(jnp.where(l == m, lane, float(LANES)), axis=-1, keepdims=True)
        idx_out = jnp.where(lane == k, sel, idx_out)
        val_out = jnp.where(lane == k, m, val_out)
        l = jnp.where(lane == sel, neg, l)
    e = jnp.exp(val_out - jnp.max(val_out, axis=-1, keepdims=True))
    idx_ref[...] = idx_out.astype(I32)
    gate_ref[...] = e / jnp.sum(e, axis=-1, keepdims=True)


def _router(x, w_pad, b_pad):
    n = x.shape[0]
    full = lambda a: pl.BlockSpec(a.shape, lambda i: (0,) * a.ndim)
    out = pl.BlockSpec((TM, LANES), lambda i: (i, 0))
    return pl.pallas_call(
        _router_body,
        grid=(n // TM,),
        in_specs=[pl.BlockSpec((TM, D_MODEL), lambda i: (i, 0)), full(w_pad), full(b_pad)],
        out_specs=[out, out],
        out_shape=[jax.ShapeDtypeStruct((n, LANES), I32), jax.ShapeDtypeStruct((n, LANES), F32)],
        compiler_params=_cparams(("arbitrary",)),
        name="router",
    )(x, w_pad, b_pad)


def _row_copy(src_hbm, src_row, dst, dst_row, sem):
    return pltpu.make_async_copy(src_hbm.at[pl.ds(src_row, 1)], dst.at[pl.ds(dst_row, 1)], sem)


def _experts_body(bexp_ref, bval_ref, tok_ref, x_hbm, w1_ref, b1_ref, w2_ref, b2_ref, o_ref,
                  xbuf, w1b, w2b, sem):
    i = pl.program_id(0)
    valid = bval_ref[i] != 0
    fresh = jnp.logical_or(i == 0, bexp_ref[i] != bexp_ref[jnp.maximum(i - 1, 0)])

    @pl.when(jnp.logical_and(valid, fresh))
    def _():
        def cast_rows(r, c):
            rows = pl.ds(pl.multiple_of(r * 128, 128), 128)
            w1b[rows, :] = w1_ref[rows, :].astype(BF16)
            w2b[rows, :] = w2_ref[rows, :].astype(BF16)
            return c
        lax.fori_loop(0, D_MODEL // 128, cast_rows, 0)

    @pl.when(valid)
    def _():
        def start(r, c):
            _row_copy(x_hbm, tok_ref[0, 0, r], xbuf, r, sem).start()
            return c
        lax.fori_loop(0, TM_E, start, 0, unroll=8)

        def wait(r, c):
            _row_copy(x_hbm, 0, xbuf, r, sem).wait()
            return c
        lax.fori_loop(0, TM_E, wait, 0, unroll=8)

        xb = xbuf[...].astype(BF16)
        glu = jnp.dot(xb, w1b[:, :D_EXP], preferred_element_type=F32) + b1_ref[:, :D_EXP]
        lin = jnp.dot(xb, w1b[:, D_EXP:], preferred_element_type=F32) + b1_ref[:, D_EXP:]
        glu = jnp.minimum(glu, SWIGLU_LIMIT)
        lin = jnp.clip(lin, -SWIGLU_LIMIT, SWIGLU_LIMIT)
        act = glu * jax.nn.sigmoid(SWIGLU_ALPHA * glu) * (lin + 1.0)
        o_ref[...] = jnp.dot(act.astype(BF16), w2b[...], preferred_element_type=F32) + b2_ref[...]

    @pl.when(jnp.logical_not(valid))
    def _():
        o_ref[...] = jnp.zeros(o_ref.shape, F32)


def _experts(block_exp, block_valid, row_tok3, x, w1, b1, w2, b2):
    wspec = lambda shape: pl.BlockSpec((None,) + shape, lambda i, be, bv: (be[i], 0, 0))
    return pl.pallas_call(
        _experts_body,
        grid_spec=pltpu.PrefetchScalarGridSpec(
            num_scalar_prefetch=2,
            grid=(N_EBLOCKS,),
            in_specs=[pl.BlockSpec((1, 1, TM_E), lambda i, be, bv: (i, 0, 0), memory_space=pltpu.SMEM),
                      pl.BlockSpec(memory_space=pl.ANY),
                      wspec((D_MODEL, 2 * D_EXP)), wspec((1, 2 * D_EXP)),
                      wspec((D_EXP, D_MODEL)), wspec((1, D_MODEL))],
            out_specs=pl.BlockSpec((TM_E, D_MODEL), lambda i, be, bv: (i, 0)),
            scratch_shapes=[pltpu.VMEM((TM_E, D_MODEL), F32),
                            pltpu.VMEM((D_MODEL, 2 * D_EXP), BF16),
                            pltpu.VMEM((D_EXP, D_MODEL), BF16),
                            pltpu.SemaphoreType.DMA(())]),
        out_shape=jax.ShapeDtypeStruct((ROWS_MAX, D_MODEL), F32),
        compiler_params=_cparams(("arbitrary",)),
        name="moe_experts",
    )(block_exp, block_valid, row_tok3, x, w1, b1, w2, b2)


def _combine_ln_body(dest_ref, gate_ref, x_ref, rows_hbm, g_ref, b_ref, o_ref, buf, sem):
    def start(j, c):
        r = j // TOP_K
        k = j % TOP_K
        pltpu.make_async_copy(rows_hbm.at[pl.ds(dest_ref[0, 0, j], 1)], buf.at[k, pl.ds(r, 1)], sem).start()
        return c
    lax.fori_loop(0, TM_C * TOP_K, start, 0, unroll=8)

    def wait(j, c):
        pltpu.make_async_copy(rows_hbm.at[pl.ds(0, 1)], buf.at[0, pl.ds(0, 1)], sem).wait()
        return c
    lax.fori_loop(0, TM_C * TOP_K, wait, 0, unroll=8)

    gates = gate_ref[...]
    y = buf[0] * gates[:, 0:1]
    for k in range(1, TOP_K):
        y = y + buf[k] * gates[:, k:k + 1]
    o_ref[...] = _layer_norm(DEEPNORM_ALPHA * x_ref[...] + y, g_ref[...], b_ref[...])


def _combine_ln(dest3, gates, x, out_rows, g, b):
    n = x.shape[0]
    full = lambda a: pl.BlockSpec(a.shape, lambda i: (0,) * a.ndim)
    row = pl.BlockSpec((TM_C, D_MODEL), lambda i: (i, 0))
    return pl.pallas_call(
        _combine_ln_body,
        grid=(n // TM_C,),
        in_specs=[pl.BlockSpec((1, 1, TM_C * TOP_K), lambda i: (i, 0, 0), memory_space=pltpu.SMEM),
                  pl.BlockSpec((TM_C, LANES), lambda i: (i, 0)), row,
                  pl.BlockSpec(memory_space=pl.ANY), full(g), full(b)],
        out_specs=row,
        out_shape=jax.ShapeDtypeStruct((n, D_MODEL), F32),
        scratch_shapes=[pltpu.VMEM((TOP_K, TM_C, D_MODEL), F32), pltpu.SemaphoreType.DMA(())],
        compiler_params=_cparams(("arbitrary",)),
        name="moe_combine_ln",
    )(dest3, gates, x, out_rows, g, b)


def _moe_positions(top_i):
    e_flat = top_i.reshape(N_PAIRS)
    onehot = (e_flat[:, None] == jnp.arange(N_EXP, dtype=I32)[None, :]).astype(I32)
    csum = jnp.cumsum(onehot, axis=0)
    rank = jnp.sum(csum * onehot, axis=1) - 1
    counts = csum[-1]
    pcounts = (counts + TM_E - 1) // TM_E * TM_E
    pends = jnp.cumsum(pcounts)
    pstarts = pends - pcounts
    dest = jnp.sum(onehot * pstarts[None, :], axis=1) + rank
    row_tok = jnp.zeros((ROWS_MAX,), I32).at[dest].set(jnp.arange(N_PAIRS, dtype=I32) // TOP_K)
    blk_start = jnp.arange(N_EBLOCKS, dtype=I32) * TM_E
    block_exp = jnp.minimum(jnp.searchsorted(pends, blk_start, side='right'), N_EXP - 1).astype(I32)
    block_valid = (blk_start < pends[-1]).astype(I32)
    return dest, row_tok, block_exp, block_valid


def _moe_post_norm(x, w_r, b_r, w1, b1, w2, b2, g, b):
    w_pad = jnp.pad(w_r, ((0, 0), (0, LANES - N_EXP)))
    b_pad = jnp.pad(b_r, (0, LANES - N_EXP)).reshape(1, LANES)
    idx, gates = _router(x, w_pad, b_pad)
    dest, row_tok, block_exp, block_valid = _moe_positions(idx[:, :TOP_K])
    out_rows = _experts(block_exp, block_valid, row_tok.reshape(N_EBLOCKS, 1, TM_E), x,
                        w1, b1.reshape(N_EXP, 1, 2 * D_EXP), w2, b2.reshape(N_EXP, 1, D_MODEL))
    return _combine_ln(dest.reshape(N_TOK // TM_C, 1, TM_C * TOP_K), gates, x, out_rows,
                       g.reshape(1, D_MODEL), b.reshape(1, D_MODEL))


def _tail8(state, keep):
    return jnp.pad(state, ((0, 0), (SUBLANES - keep, 0), (0, 0)))


def kernel(x_prompt, x_sample, state_conv_a, state_conv_qkv, state_delta, ab_w_in, ab_conv_a, ab_conv_qkv,
           ab_a_log, ab_dt_bias, ab_norm_g, ab_w_out, c_w_in, c_b_in, c_ln_g, c_ln_b, c_w_s, c_b_s, c_w_out,
           moe_w_router, moe_b_router, moe_w1, moe_b1, moe_w2, moe_b2, ln_g, ln_b):
    x = jnp.concatenate([x_prompt.reshape(N_PROMPT, D_MODEL), x_sample.reshape(N_SAMPLE, D_MODEL)], axis=0)
    lnrow = lambda layer, j: (ln_g[layer, j].reshape(1, D_MODEL), ln_b[layer, j].reshape(1, D_MODEL))

    w_in = ab_w_in[0]
    w_main = w_in[:, :W_MAIN].astype(BF16)
    w_ab = jnp.pad(w_in[:, W_MAIN:], ((0, 0), (0, LANES - 2 * H_B))).astype(BF16)
    alog_row = jnp.pad(ab_a_log[0], (0, LANES - H_B)).reshape(1, LANES)
    dtb_row = jnp.pad(ab_dt_bias[0], (0, LANES - H_B)).reshape(1, LANES)
    bg, u, qkv, z, gb = _proj_ab(x, w_main, w_ab, alog_row, dtb_row)

    ng = ab_norm_g[0].reshape(1, DV)
    ycat = jnp.zeros((N_TOK, D_MODEL), BF16)
    gbr_p = _group_rows(gb[:N_PROMPT], BATCH, SEQ, DN_CHUNK, 1)
    ycat, pq8, pu8, p_delta = _gdn(
        qkv, u, bg, z, gbr_p,
        jnp.zeros((BATCH, SUBLANES, D_QKV), F32), jnp.zeros((BATCH, SUBLANES, D_A), F32),
        jnp.zeros((BATCH, H_B, DK, DV), F32), ab_conv_qkv[0], ab_conv_a[0], ng, ycat,
        chunk=DN_CHUNK, bb=1, n_seq=BATCH, n_chunks=SEQ // DN_CHUNK, row_block0=0)
    bb_s = 16
    gbr_s = _group_rows(gb[N_PROMPT:], DEC_BATCH, DEC_SEQ, DEC_SEQ, bb_s)
    ycat, sq8, su8, s_delta = _gdn(
        qkv, u, bg, z, gbr_s,
        _tail8(state_conv_qkv[0], CONV_B - 1), _tail8(state_conv_a[0], CONV_A - 1), state_delta[0],
        ab_conv_qkv[0], ab_conv_a[0], ng, ycat,
        chunk=DEC_SEQ, bb=bb_s, n_seq=DEC_BATCH, n_chunks=1, row_block0=N_PROMPT // (bb_s * DEC_SEQ))
    x = _mm_res_ln(ycat, x, ab_w_out[0].astype(BF16), *lnrow(0, 0))
    x = _moe_post_norm(x, moe_w_router[0], moe_b_router[0], moe_w1[0], moe_b1[0], moe_w2[0], moe_b2[0],
                       ln_g[0, 1], ln_b[0, 1])

    uc, vc = _c_in(x, c_w_in[0].astype(BF16), c_b_in[0].reshape(1, 2 * D_C),
                   c_ln_g[0].reshape(1, D_C), c_ln_b[0].reshape(1, D_C))
    ws = c_w_s[0]
    reps = C_CHUNK // DEC_SEQ
    wmix2 = jnp.stack([ws, jnp.tile(ws[:, :DEC_SEQ, :DEC_SEQ], (1, reps, reps))])
    ri = jnp.arange(C_CHUNK)
    tril = ri[:, None] >= ri[None, :]
    mask2 = jnp.stack([tril, tril & ((ri[:, None] // DEC_SEQ) == (ri[None, :] // DEC_SEQ))]).astype(F32)
    bias_p = jnp.repeat(c_b_s[0].T, D_C // C_GROUPS, axis=1)
    bias2 = jnp.stack([bias_p, jnp.tile(bias_p[:DEC_SEQ], (reps, 1))])
    x = _c_out(uc, vc, x, wmix2, mask2, bias2, c_w_out[0].astype(BF16), *lnrow(1, 0))
    x = _moe_post_norm(x, moe_w_router[1], moe_b_router[1], moe_w1[1], moe_b1[1], moe_w2[1], moe_b2[1],
                       ln_g[1, 1], ln_b[1, 1])

    y_prompt = x[:N_PROMPT].reshape(BATCH, SEQ, D_MODEL)
    y_sample = x[N_PROMPT:].reshape(DEC_BATCH, DEC_SEQ, D_MODEL)
    ka, kq = CONV_A - 1, CONV_B - 1
    return (y_prompt, y_sample,
            pu8[None, :, SUBLANES - ka:], pq8[None, :, SUBLANES - kq:], p_delta[None],
            su8[None, :, SUBLANES - ka:], sq8[None, :, SUBLANES - kq:], s_delta[None],
            vc[N_PROMPT:].reshape(1, DEC_BATCH, DEC_SEQ, D_C))
```

```python
import functools
import math

import jax
import jax.numpy as jnp
from jax import lax
from jax.experimental import pallas as pl
from jax.experimental.pallas import tpu as pltpu

F32 = jnp.float32
BF16 = jnp.bfloat16
I32 = jnp.int32

D_MODEL = 1024
BATCH = 8
SEQ = 2048
DEC_BATCH = 128
DEC_SEQ = 8
N_PROMPT = BATCH * SEQ
N_SAMPLE = DEC_BATCH * DEC_SEQ
N_TOK = N_PROMPT + N_SAMPLE
D_A = 512
CONV_A = 3
H_B = 4
DK = 128
DV = 128
D_QKV = 1536
CONV_B = 4
DN_CHUNK = 64
W_MAIN = 3 * D_A + D_QKV + H_B * DV
D_C = 1024
C_GROUPS = 8
C_CHUNK = 128
N_EXP = 32
TOP_K = 4
D_EXP = 1024
SWIGLU_ALPHA = 1.702
SWIGLU_LIMIT = 7.0
DEEPNORM_ALPHA = 4.0 ** 0.25
LN_EPS = 1e-5
RMS_EPS = 1e-6

LANES = 128
SUBLANES = 8
VMEM_LIMIT = 56 * 1024 * 1024

TM = 512
TM_E = 256
TM_C = 256
TM_D = 256
GDN_CPS = 4
N_PAIRS = N_TOK * TOP_K
N_EBLOCKS = (N_PAIRS + N_EXP * (TM_E - 1)) // TM_E
ROWS_TOTAL = N_EBLOCKS * TM_E
RANK_BITS = 15
assert N_TOK <= 1 << RANK_BITS


def _cparams(sem):
    return pltpu.CompilerParams(dimension_semantics=sem, vmem_limit_bytes=VMEM_LIMIT)


def _layer_norm(t, g, b):
    mu = jnp.mean(t, axis=-1, keepdims=True)
    d = t - mu
    var = jnp.mean(d * d, axis=-1, keepdims=True)
    return d * lax.rsqrt(var + LN_EPS) * g + b


def _bdot(a, b):
    return jnp.dot(a.astype(BF16), b.astype(BF16), preferred_element_type=F32)


def _proj_ab_body(x_ref, w_ref, wab_ref, alog_ref, dtb_ref, bg_ref, u_ref, qkv_ref, z_ref, gb_ref):
    xb = x_ref[...].astype(BF16)

    def mm(lo, hi):
        return jnp.dot(xb, w_ref[:, lo:hi], preferred_element_type=F32)

    bg_ref[...] = mm(0, D_A)
    u_ref[...] = mm(D_A, 2 * D_A) * mm(2 * D_A, 3 * D_A)
    for c in range(D_QKV // 512):
        qkv_ref[:, c * 512:(c + 1) * 512] = mm(3 * D_A + c * 512, 3 * D_A + (c + 1) * 512)
    z_ref[...] = mm(3 * D_A + D_QKV, W_MAIN)
    ab = jnp.dot(xb, wab_ref[...], preferred_element_type=F32)
    g = -jnp.exp(alog_ref[...]) * jax.nn.softplus(ab + dtb_ref[...])
    beta = jax.nn.sigmoid(ab)
    lane = lax.broadcasted_iota(I32, ab.shape, 1)
    gb_ref[...] = jnp.where(lane < H_B, g, beta)


def _proj_ab(x, w_main, w_ab, alog_row, dtb_row):
    n = x.shape[0]
    row = lambda w: pl.BlockSpec((TM, w), lambda i: (i, 0))
    full = lambda a: pl.BlockSpec(a.shape, lambda i: (0,) * a.ndim)
    return pl.pallas_call(
        _proj_ab_body,
        grid=(n // TM,),
        in_specs=[row(D_MODEL), full(w_main), full(w_ab), full(alog_row), full(dtb_row)],
        out_specs=[row(D_A), row(D_A), row(D_QKV), row(D_A), row(LANES)],
        out_shape=[jax.ShapeDtypeStruct((n, D_A), F32), jax.ShapeDtypeStruct((n, D_A), F32),
                   jax.ShapeDtypeStruct((n, D_QKV), F32), jax.ShapeDtypeStruct((n, D_A), F32),
                   jax.ShapeDtypeStruct((n, LANES), F32)],
        compiler_params=_cparams(("arbitrary",)),
        name="proj_ab",
    )(x, w_main, w_ab, alog_row, dtb_row)


def _shift_rows(x, prev8, s):
    if s == 0:
        return x
    xr = pltpu.roll(x, s, axis=0)
    pr = pltpu.roll(prev8, s, axis=0)
    rid = lax.broadcasted_iota(I32, pr.shape, 0)
    head = jnp.where(rid < s, pr, xr[0:SUBLANES])
    if x.shape[0] == SUBLANES:
        return head
    return jnp.concatenate([head, xr[SUBLANES:]], axis=0)


def _causal_conv(x, prev8, w_ref, taps):
    y = x * w_ref[taps - 1:taps, :]
    for s in range(1, taps):
        y = y + _shift_rows(x, prev8, s) * w_ref[taps - 1 - s:taps - s, :]
    return y


def _lane_scan(x, pos, chunk, reverse):
    s = 1
    while s < chunk:
        if reverse:
            x = x + jnp.where(pos < chunk - s, pltpu.roll(x, LANES - s, axis=1), 0.0)
        else:
            x = x + jnp.where(pos >= s, pltpu.roll(x, s, axis=1), 0.0)
        s *= 2
    return x


def _per_row(row):
    return jnp.broadcast_to(row, (LANES, LANES)).T


def _gdn_body(chunk, bb, cps, n_steps,
              qkv_ref, u_ref, bg_ref, z_ref, gbr_ref, pq_ref, pu_ref, s0_ref, wq_ref, wa_ref, ng_ref, y_any,
              ycat_ref, nq_ref, nu_ref, sn_ref, cq_scr, cu_scr, s_scr):
    del y_any
    n = pl.program_id(1)

    @pl.when(n == 0)
    def _():
        cq_scr[...] = pq_ref[...]
        cu_scr[...] = pu_ref[...]
        s_scr[...] = s0_ref[...]

    gsz = LANES // chunk
    n_groups = bb * cps * H_B // gsz
    levels = int(math.log2(chunk))
    span = cps * chunk

    ii = lax.broadcasted_iota(I32, (LANES, LANES), 0)
    jj = lax.broadcasted_iota(I32, (LANES, LANES), 1)
    same = (ii // chunk) == (jj // chunk)
    m_incl = same & (ii >= jj)
    m_strict = same & (ii > jj)
    eye = (ii == jj).astype(F32)
    pos = lax.broadcasted_iota(I32, (SUBLANES, LANES), 1) % chunk

    qs, ks, vs = {}, {}, {}
    for b in range(bb):
        rows = slice(b * span, (b + 1) * span)
        x = qkv_ref[rows, :]
        qc = _causal_conv(x, cq_scr[b], wq_ref, CONV_B)
        qc = qc * jax.nn.sigmoid(qc)
        cq_scr[b] = x[span - SUBLANES:span]
        uu = u_ref[rows, :]
        ca = _causal_conv(uu, cu_scr[b], wa_ref, CONV_A)
        cu_scr[b] = uu[span - SUBLANES:span]
        ycat_ref[rows, 0:D_A] = (bg_ref[rows, :] * ca).astype(BF16)
        for j in range(cps):
            r = slice(j * chunk, (j + 1) * chunk)
            for h in range(H_B):
                qh = qc[r, h * DK:(h + 1) * DK]
                kh = qc[r, H_B * DK + h * DK:H_B * DK + (h + 1) * DK]
                vh = qc[r, 2 * H_B * DK + h * DV:2 * H_B * DK + (h + 1) * DV]
                key = (b * cps + j, h)
                qs[key] = qh * (lax.rsqrt(jnp.sum(qh * qh, axis=-1, keepdims=True) + RMS_EPS) * (DK ** -0.5))
                ks[key] = kh * lax.rsqrt(jnp.sum(kh * kh, axis=-1, keepdims=True) + RMS_EPS)
                vs[key] = vh

    for gi in range(n_groups):
        blocks = [divmod(gi * gsz + t, H_B) for t in range(gsz)]
        cat = lambda d: jnp.concatenate([d[uh] for uh in blocks], axis=0) if gsz > 1 else d[blocks[0]]
        qg, kg, vg = cat(qs), cat(ks), cat(vs)

        tile = gbr_ref[gi]
        gc = _lane_scan(tile, pos, chunk, False)
        rs = _lane_scan(tile, pos, chunk, True) - tile
        gc_row = gc[0:1]
        gc_m = _per_row(gc_row)
        rs_m = _per_row(rs[0:1])
        beta_m = _per_row(tile[1:2])
        diff = gc_m - jnp.broadcast_to(gc_row, (LANES, LANES))
        decay = jnp.where(m_incl, jnp.exp(jnp.where(m_incl, diff, 0.0)), 0.0)
        eg = jnp.exp(gc_m)
        etot = jnp.exp(gc_m + rs_m)

        kb = kg * beta_m
        kgb = kg.astype(BF16)
        a_mat = lax.dot_general(kb.astype(BF16), kgb, (((1,), (1,)), ((), ())), preferred_element_type=F32)
        lm = jnp.where(m_strict, a_mat * decay, 0.0)
        attn = lax.dot_general(qg.astype(BF16), kgb, (((1,), (1,)), ((), ())), preferred_element_type=F32) * decay

        p = eye - lm
        m = _bdot(lm, lm)
        for lvl in range(1, levels):
            p = p + _bdot(p, m)
            if lvl < levels - 1:
                m = _bdot(m, m)
        uw = _bdot(p, jnp.concatenate([vg * beta_m, kb * eg], axis=1))
        u_all, w_all = uw[:, :DV], uw[:, DV:]
        qe = qg * eg
        kdec = kg * jnp.exp(rs_m)

        vnew, qsv = [], []
        for t, (unit, h) in enumerate(blocks):
            b = unit // cps
            r = slice(t * chunk, (t + 1) * chunk)
            s_bf = s_scr[b, h].astype(BF16)
            lhs = jnp.concatenate([w_all[r], qe[r]], axis=0).astype(BF16)
            both = jnp.dot(lhs, s_bf, preferred_element_type=F32)
            vnew.append(u_all[r] - both[:chunk])
            qsv.append(both[chunk:])
        vnew_g = jnp.concatenate(vnew, axis=0) if gsz > 1 else vnew[0]
        qs_g = jnp.concatenate(qsv, axis=0) if gsz > 1 else qsv[0]
        o = qs_g + _bdot(attn, vnew_g)
        o = o * lax.rsqrt(jnp.mean(o * o, axis=-1, keepdims=True) + RMS_EPS) * ng_ref[...]

        for t, (unit, h) in enumerate(blocks):
            b = unit // cps
            r = slice(t * chunk, (t + 1) * chunk)
            rows = slice(unit * chunk, (unit + 1) * chunk)
            zz = z_ref[rows, h * DV:(h + 1) * DV]
            ycat_ref[rows, D_A + h * DV:D_A + (h + 1) * DV] = (o[r] * (zz * jax.nn.sigmoid(zz))).astype(BF16)
            upd = lax.dot_general(kdec[r].astype(BF16), vnew[t].astype(BF16), (((0,), (0,)), ((), ())),
                                  preferred_element_type=F32)
            scale = jnp.broadcast_to(etot[t * chunk:t * chunk + 1, :], (DK, DV))
            s_scr[b, h] = s_scr[b, h] * scale + upd

    @pl.when(n == n_steps - 1)
    def _():
        nq_ref[...] = cq_scr[...]
        nu_ref[...] = cu_scr[...]
        sn_ref[...] = s_scr[...]


def _gdn(qkv, u, bg, z, gbr, prev_q, prev_u, s0, wq, wa, ng, ycat, *, chunk, bb, cps, n_seq, t_len, row_block0):
    rb = bb * cps * chunk
    gs = rb * H_B // LANES
    n_steps = t_len // (cps * chunk)
    rowmap = lambda i, n: (row_block0 + i * n_steps + n, 0)
    row = lambda w: pl.BlockSpec((rb, w), rowmap)
    seq3 = lambda w: pl.BlockSpec((bb, SUBLANES, w), lambda i, n: (i, 0, 0))
    full = lambda a: pl.BlockSpec(a.shape, lambda i, n: (0,) * a.ndim)
    st = pl.BlockSpec((bb, H_B, DK, DV), lambda i, n: (i, 0, 0, 0))
    return pl.pallas_call(
        functools.partial(_gdn_body, chunk, bb, cps, n_steps),
        grid=(n_seq // bb, n_steps),
        in_specs=[row(D_QKV), row(D_A), row(D_A), row(D_A),
                  pl.BlockSpec((None, gs, SUBLANES, LANES), lambda i, n: (i * n_steps + n, 0, 0, 0)),
                  seq3(D_QKV), seq3(D_A), st, full(wq), full(wa), full(ng),
                  pl.BlockSpec(memory_space=pl.ANY)],
        out_specs=[pl.BlockSpec((rb, D_MODEL), rowmap), seq3(D_QKV), seq3(D_A), st],
        out_shape=[jax.ShapeDtypeStruct(ycat.shape, BF16),
                   jax.ShapeDtypeStruct((n_seq, SUBLANES, D_QKV), F32),
                   jax.ShapeDtypeStruct((n_seq, SUBLANES, D_A), F32),
                   jax.ShapeDtypeStruct((n_seq, H_B, DK, DV), F32)],
        scratch_shapes=[pltpu.VMEM((bb, SUBLANES, D_QKV), F32), pltpu.VMEM((bb, SUBLANES, D_A), F32),
                        pltpu.VMEM((bb, H_B, DK, DV), F32)],
        input_output_aliases={11: 0},
        compiler_params=_cparams(("arbitrary", "arbitrary")),
        name=f"gdn_c{chunk}",
    )(qkv, u, bg, z, gbr, prev_q, prev_u, s0, wq, wa, ng, ycat)


def _group_rows(gb, n_seq, t_len, chunk, bb, cps):
    n_steps = t_len // (cps * chunk)
    gs = bb * cps * chunk * H_B // LANES
    g = gb[:, :2 * H_B].reshape(n_seq // bb, bb, n_steps, cps, chunk, 2, H_B)
    g = jnp.transpose(g, (0, 2, 5, 1, 3, 6, 4))
    g = g.reshape(n_seq // bb * n_steps, 2, gs, LANES)
    g = jnp.transpose(g, (0, 2, 1, 3))
    return jnp.pad(g, ((0, 0), (0, 0), (0, SUBLANES - 2), (0, 0)))


def _route(x_new, wr_ref, br_ref, tri_ref, cnt_scr, gate_ref, dest_ref, counts_ref):
    @pl.when(pl.program_id(0) == 0)
    def _():
        cnt_scr[...] = jnp.zeros(cnt_scr.shape, F32)

    xh = x_new.astype(BF16)
    xl = (x_new - xh.astype(F32)).astype(BF16)
    p = jnp.dot(xh, wr_ref[...], preferred_element_type=F32)
    logits = (p[:, :LANES] + p[:, LANES:] + jnp.dot(xl, wr_ref[:, :LANES], preferred_element_type=F32)
              + br_ref[...])
    lane = lax.broadcasted_iota(I32, logits.shape, 1).astype(F32)
    neg = jnp.float32(-jnp.inf)
    l = jnp.where(lane < N_EXP, logits, neg)
    val_out = jnp.full(logits.shape, neg, F32)
    sels, hots = [], []
    for k in range(TOP_K):
        m = jnp.max(l, axis=-1, keepdims=True)
        sel = jnp.min(jnp.where(l == m, lane, float(LANES)), axis=-1, keepdims=True)
        hit = lane == sel
        val_out = jnp.where(lane == k, m, val_out)
        l = jnp.where(hit, neg, l)
        sels.append(sel)
        hots.append(hit.astype(F32))
    e = jnp.exp(val_out - jnp.max(val_out, axis=-1, keepdims=True))
    gate_ref[...] = e / jnp.sum(e, axis=-1, keepdims=True)

    hot = hots[0] + hots[1] + hots[2] + hots[3]
    before = jnp.dot(tri_ref[...], hot.astype(BF16), preferred_element_type=F32) + cnt_scr[...]
    dest = jnp.zeros(logits.shape, F32)
    for k in range(TOP_K):
        rank = jnp.sum(hots[k] * before, axis=-1, keepdims=True)
        dest = jnp.where(lane == k, sels[k] * float(1 << RANK_BITS) + rank, dest)
    dest_ref[...] = dest.astype(I32)
    cnt_scr[...] = cnt_scr[...] + jnp.sum(hot, axis=0, keepdims=True)
    counts_ref[...] = cnt_scr[...]


_ROUTE_OUT_SPECS = [pl.BlockSpec((TM, LANES), lambda i: (i, 0)), pl.BlockSpec((TM, LANES), lambda i: (i, 0)),
                    pl.BlockSpec((1, LANES), lambda i: (0, 0))]


def _route_out_shapes(n):
    return [jax.ShapeDtypeStruct((n, LANES), F32), jax.ShapeDtypeStruct((n, LANES), I32),
            jax.ShapeDtypeStruct((1, LANES), F32)]


def _mm_res_ln_body(y_ref, x_ref, w_ref, g_ref, b_ref, wr_ref, br_ref, tri_ref,
                    o_ref, gate_ref, dest_ref, counts_ref, cnt_scr):
    h = jnp.dot(y_ref[...].astype(BF16), w_ref[...], preferred_element_type=F32)
    xn = _layer_norm(DEEPNORM_ALPHA * x_ref[...] + h, g_ref[...], b_ref[...])
    o_ref[...] = xn
    _route(xn, wr_ref, br_ref, tri_ref, cnt_scr, gate_ref, dest_ref, counts_ref)


def _mm_res_ln(y, x, w, g, b, wr, br, tri):
    n = x.shape[0]
    row = pl.BlockSpec((TM, D_MODEL), lambda i: (i, 0))
    full = lambda a: pl.BlockSpec(a.shape, lambda i: (0,) * a.ndim)
    return pl.pallas_call(
        _mm_res_ln_body,
        grid=(n // TM,),
        in_specs=[row, row, full(w), full(g), full(b), full(wr), full(br), full(tri)],
        out_specs=[row] + _ROUTE_OUT_SPECS,
        out_shape=[jax.ShapeDtypeStruct((n, D_MODEL), F32)] + _route_out_shapes(n),
        scratch_shapes=[pltpu.VMEM((1, LANES), F32)],
        compiler_params=_cparams(("arbitrary",)),
        name="mm_res_ln",
    )(y, x, w, g, b, wr, br, tri)


def _gelu(x):
    return 0.5 * x * (1.0 + lax.erf(x * (2.0 ** -0.5)))


def _c_in_body(x_ref, w_ref, b_ref, lg_ref, lb_ref, u_ref, v_ref):
    xb = x_ref[...].astype(BF16)
    u_ref[...] = _gelu(jnp.dot(xb, w_ref[:, :D_C], preferred_element_type=F32) + b_ref[:, :D_C])
    v = _gelu(jnp.dot(xb, w_ref[:, D_C:], preferred_element_type=F32) + b_ref[:, D_C:])
    v_ref[...] = _layer_norm(v, lg_ref[...], lb_ref[...])


def _c_in(x, w, b, lg, lb):
    n = x.shape[0]
    row = pl.BlockSpec((TM, D_MODEL), lambda i: (i, 0))
    full = lambda a: pl.BlockSpec(a.shape, lambda i: (0,) * a.ndim)
    return pl.pallas_call(
        _c_in_body,
        grid=(n // TM,),
        in_specs=[row, full(w), full(b), full(lg), full(lb)],
        out_specs=[row, row],
        out_shape=[jax.ShapeDtypeStruct((n, D_C), F32), jax.ShapeDtypeStruct((n, D_C), F32)],
        compiler_params=_cparams(("arbitrary",)),
        name="c_in",
    )(x, w, b, lg, lb)


def _c_out_body(u_ref, v_ref, x_ref, wmix_ref, mask_ref, bias_ref, wout_ref, g_ref, b_ref, wr_ref, br_ref, tri_ref,
                o_ref, gate_ref, dest_ref, counts_ref, us_scr, cnt_scr):
    mix = [(wmix_ref[gi] * mask_ref[...]).astype(BF16) for gi in range(C_GROUPS)]
    for t in range(TM // C_CHUNK):
        rows = slice(t * C_CHUNK, (t + 1) * C_CHUNK)
        for gi in range(C_GROUPS):
            cols = slice(gi * LANES, (gi + 1) * LANES)
            s = jnp.dot(mix[gi], v_ref[rows, cols].astype(BF16), preferred_element_type=F32) + bias_ref[:, cols]
            us_scr[rows, cols] = (u_ref[rows, cols] * s).astype(BF16)
    h = jnp.dot(us_scr[...], wout_ref[...], preferred_element_type=F32)
    xn = _layer_norm(DEEPNORM_ALPHA * x_ref[...] + h, g_ref[...], b_ref[...])
    o_ref[...] = xn
    _route(xn, wr_ref, br_ref, tri_ref, cnt_scr, gate_ref, dest_ref, counts_ref)


def _c_out(u, v, x, wmix2, mask2, bias2, wout, g, b, wr, br, tri):
    n = x.shape[0]
    first_sample_step = N_PROMPT // TM
    sel = lambda i: jnp.where(i >= first_sample_step, 1, 0)
    row = pl.BlockSpec((TM, D_MODEL), lambda i: (i, 0))
    full = lambda a: pl.BlockSpec(a.shape, lambda i: (0,) * a.ndim)
    return pl.pallas_call(
        _c_out_body,
        grid=(n // TM,),
        in_specs=[row, row, row,
                  pl.BlockSpec((None, C_GROUPS, C_CHUNK, C_CHUNK), lambda i: (sel(i), 0, 0, 0)),
                  pl.BlockSpec((None, C_CHUNK, C_CHUNK), lambda i: (sel(i), 0, 0)),
                  pl.BlockSpec((None, C_CHUNK, D_C), lambda i: (sel(i), 0, 0)),
                  full(wout), full(g), full(b), full(wr), full(br), full(tri)],
        out_specs=[row] + _ROUTE_OUT_SPECS,
        out_shape=[jax.ShapeDtypeStruct((n, D_MODEL), F32)] + _route_out_shapes(n),
        scratch_shapes=[pltpu.VMEM((TM, D_C), BF16), pltpu.VMEM((1, LANES), F32)],
        compiler_params=_cparams(("arbitrary",)),
        name="c_out",
    )(u, v, x, wmix2, mask2, bias2, wout, g, b, wr, br, tri)


def _for_each_row(n_rows, fn):
    def group(t8, c):
        base = pl.multiple_of(t8 * SUBLANES, SUBLANES)
        for s in range(SUBLANES):
            fn(t8, base, s)
        return c
    lax.fori_loop(0, n_rows // SUBLANES, group, 0)


def _dispatch_body(pad_start_ref, pad_n_ref, n_used_ref, dest_ref, x_ref, rows_hbm, zbuf, sem, zsem):
    i = pl.program_id(0)

    @pl.when(i == 0)
    def _():
        zbuf[...] = jnp.zeros(zbuf.shape, F32)

    @pl.when(i < N_EXP)
    def _():
        n = pad_n_ref[i]
        start = pad_start_ref[i]
        odd = n & (SUBLANES - 1)
        copies = [(pltpu.make_async_copy(zbuf.at[pl.ds(0, 1)], rows_hbm.at[pl.ds(start + s, 1)], zsem), s < odd)
                  for s in range(SUBLANES - 1)]
        off = start + odd
        for size in [1 << p for p in reversed(range(3, int(math.log2(TM_E))))]:
            dst = rows_hbm.at[pl.ds(pl.multiple_of(off, SUBLANES), size)]
            copies.append((pltpu.make_async_copy(zbuf.at[pl.ds(0, size)], dst, zsem), (n & size) != 0))
            off = off + (n & size)
        for cp, used in copies:
            pl.when(used)(cp.start)
        for cp, used in copies:
            pl.when(used)(cp.wait)

    blk = n_used_ref[0] + (i - N_EXP)

    @pl.when(jnp.logical_and(i >= N_EXP, blk < N_EBLOCKS))
    def _():
        cp = pltpu.make_async_copy(zbuf, rows_hbm.at[pl.ds(pl.multiple_of(blk * TM_E, TM_E), TM_E)], zsem)
        cp.start()
        cp.wait()

    def push(t8, base, s):
        src = x_ref.at[pl.ds(base, SUBLANES)].at[pl.ds(s, 1)]
        for k in range(TOP_K):
            d = dest_ref[0, 0, (t8 * SUBLANES + s) * TOP_K + k]
            pltpu.make_async_copy(src, rows_hbm.at[pl.ds(d, 1)], sem).start()
    _for_each_row(TM_D, push)

    def drain(t8, base, s):
        for k in range(TOP_K):
            pltpu.make_async_copy(x_ref.at[pl.ds(0, 1)], rows_hbm.at[pl.ds(0, 1)], sem).wait()
    _for_each_row(TM_D, drain)


def _dispatch(pads, dest3, x):
    n = x.shape[0]
    assert n // TM_D >= N_EXP + (N_EBLOCKS - N_PAIRS // TM_E)
    return pl.pallas_call(
        _dispatch_body,
        grid_spec=pltpu.PrefetchScalarGridSpec(
            num_scalar_prefetch=3,
            grid=(n // TM_D,),
            in_specs=[pl.BlockSpec((1, 1, TM_D * TOP_K), lambda i, *_: (i, 0, 0), memory_space=pltpu.SMEM),
                      pl.BlockSpec((TM_D, D_MODEL), lambda i, *_: (i, 0))],
            out_specs=pl.BlockSpec(memory_space=pl.ANY),
            scratch_shapes=[pltpu.VMEM((TM_E, D_MODEL), F32), pltpu.SemaphoreType.DMA(()),
                            pltpu.SemaphoreType.DMA(())]),
        out_shape=jax.ShapeDtypeStruct((ROWS_TOTAL, D_MODEL), F32),
        compiler_params=_cparams(("arbitrary",)),
        name="moe_dispatch",
    )(*pads, dest3, x)


def _experts_body(bexp_ref, bval_ref, x_ref, w1_ref, b1_ref, w2_ref, b2_ref, o_ref, w1b, w2b):
    i = pl.program_id(0)
    valid = bval_ref[i] != 0
    fresh = jnp.logical_or(i == 0, bexp_ref[i] != bexp_ref[jnp.maximum(i - 1, 0)])

    @pl.when(jnp.logical_and(valid, fresh))
    def _():
        def cast_rows(r, c):
            rows = pl.ds(pl.multiple_of(r * 128, 128), 128)
            w1b[rows, :] = w1_ref[rows, :].astype(BF16)
            w2b[rows, :] = w2_ref[rows, :].astype(BF16)
            return c
        lax.fori_loop(0, D_MODEL // 128, cast_rows, 0)

    @pl.when(jnp.logical_not(valid))
    def _():
        o_ref[...] = jnp.zeros(o_ref.shape, F32)

    @pl.when(valid)
    def _():
        xb = x_ref[...].astype(BF16)
        glu = jnp.dot(xb, w1b[:, :D_EXP], preferred_element_type=F32) + b1_ref[:, :D_EXP]
        lin = jnp.dot(xb, w1b[:, D_EXP:], preferred_element_type=F32) + b1_ref[:, D_EXP:]
        glu = jnp.minimum(glu, SWIGLU_LIMIT)
        lin = jnp.clip(lin, -SWIGLU_LIMIT, SWIGLU_LIMIT)
        act = glu * jax.nn.sigmoid(SWIGLU_ALPHA * glu) * (lin + 1.0)
        o_ref[...] = jnp.dot(act.astype(BF16), w2b[...], preferred_element_type=F32) + b2_ref[...]


def _experts(layer, tables, x_rows, w1, b1, w2, b2):
    wspec = lambda shape: pl.BlockSpec((None, None) + shape, lambda i, be, bv: (layer, be[i], 0, 0))
    rows = pl.BlockSpec((TM_E, D_MODEL), lambda i, be, bv: (i, 0))
    return pl.pallas_call(
        _experts_body,
        grid_spec=pltpu.PrefetchScalarGridSpec(
            num_scalar_prefetch=2,
            grid=(N_EBLOCKS,),
            in_specs=[rows, wspec((D_MODEL, 2 * D_EXP)), wspec((1, 2 * D_EXP)),
                      wspec((D_EXP, D_MODEL)), wspec((1, D_MODEL))],
            out_specs=rows,
            scratch_shapes=[pltpu.VMEM((D_MODEL, 2 * D_EXP), BF16), pltpu.VMEM((D_EXP, D_MODEL), BF16)]),
        out_shape=jax.ShapeDtypeStruct((ROWS_TOTAL, D_MODEL), F32),
        compiler_params=_cparams(("arbitrary",)),
        name="moe_experts",
    )(*tables, x_rows, w1, b1, w2, b2)


def _combine_ln_body(dest_ref, gate_ref, x_ref, rows_hbm, g_ref, b_ref, o_ref, buf, sem):
    def pull(t8, base, s):
        for k in range(TOP_K):
            d = dest_ref[0, 0, (t8 * SUBLANES + s) * TOP_K + k]
            dst = buf.at[k].at[pl.ds(base, SUBLANES)].at[pl.ds(s, 1)]
            pltpu.make_async_copy(rows_hbm.at[pl.ds(d, 1)], dst, sem).start()
    _for_each_row(TM_C, pull)

    def drain(t8, base, s):
        for k in range(TOP_K):
            pltpu.make_async_copy(rows_hbm.at[pl.ds(0, 1)], buf.at[0].at[pl.ds(0, 1)], sem).wait()
    _for_each_row(TM_C, drain)

    gates = gate_ref[...]
    y = buf[0] * gates[:, 0:1]
    for k in range(1, TOP_K):
        y = y + buf[k] * gates[:, k:k + 1]
    o_ref[...] = _layer_norm(DEEPNORM_ALPHA * x_ref[...] + y, g_ref[...], b_ref[...])


def _combine_ln(dest3, gates, x, out_rows, g, b):
    n = x.shape[0]
    full = lambda a: pl.BlockSpec(a.shape, lambda i: (0,) * a.ndim)
    row = pl.BlockSpec((TM_C, D_MODEL), lambda i: (i, 0))
    return pl.pallas_call(
        _combine_ln_body,
        grid=(n // TM_C,),
        in_specs=[pl.BlockSpec((1, 1, TM_C * TOP_K), lambda i: (i, 0, 0), memory_space=pltpu.SMEM),
                  pl.BlockSpec((TM_C, LANES), lambda i: (i, 0)), row,
                  pl.BlockSpec(memory_space=pl.ANY), full(g), full(b)],
        out_specs=row,
        out_shape=jax.ShapeDtypeStruct((n, D_MODEL), F32),
        scratch_shapes=[pltpu.VMEM((TOP_K, TM_C, D_MODEL), F32), pltpu.SemaphoreType.DMA(())],
        compiler_params=_cparams(("arbitrary",)),
        name="moe_combine_ln",
    )(dest3, gates, x, out_rows, g, b)


def _positions(enc, counts):
    nb = (counts + TM_E - 1) // TM_E
    cum = jnp.cumsum(nb)
    first_blk = cum - nb
    experts = jnp.arange(N_EXP, dtype=I32)
    pair_hot = ((enc >> RANK_BITS)[:, :, None] == experts[None, None, :]).astype(I32)
    dest = jnp.sum(pair_hot * (first_blk * TM_E)[None, None, :], axis=2) + (enc & ((1 << RANK_BITS) - 1))
    n_used = cum[-1]
    blk = jnp.arange(N_EBLOCKS, dtype=I32)
    exp = jnp.minimum(jnp.sum((cum[None, :] <= blk[:, None]).astype(I32), axis=1), N_EXP - 1)
    pads = (first_blk * TM_E + counts, nb * TM_E - counts, n_used.reshape(1))
    return dest, (exp, (blk < n_used).astype(I32)), pads


def _moe_post_norm(layer, x, gates, enc, counts, w1, b1, w2, b2, g, b):
    dest4, tables, pads = _positions(enc[:, :TOP_K], counts[0, :N_EXP].astype(I32))
    x_rows = _dispatch(pads, dest4.reshape(N_TOK // TM_D, 1, TM_D * TOP_K), x)
    out_rows = _experts(layer, tables, x_rows, w1, b1.reshape(b1.shape[:2] + (1, 2 * D_EXP)),
                        w2, b2.reshape(b2.shape[:2] + (1, D_MODEL)))
    return _combine_ln(dest4.reshape(N_TOK // TM_C, 1, TM_C * TOP_K), gates, x, out_rows,
                       g.reshape(1, D_MODEL), b.reshape(1, D_MODEL))


def _router_weights(w_r, b_r):
    wh = w_r.astype(BF16)
    wl = (w_r - wh.astype(F32)).astype(BF16)
    pad = lambda a: jnp.pad(a, ((0, 0), (0, LANES - N_EXP)))
    return jnp.concatenate([pad(wh), pad(wl)], axis=1), jnp.pad(b_r, (0, LANES - N_EXP)).reshape(1, LANES)


def _tail8(state, keep):
    return jnp.pad(state, ((0, 0), (SUBLANES - keep, 0), (0, 0)))


def kernel(x_prompt, x_sample, state_conv_a, state_conv_qkv, state_delta, ab_w_in, ab_conv_a, ab_conv_qkv,
           ab_a_log, ab_dt_bias, ab_norm_g, ab_w_out, c_w_in, c_b_in, c_ln_g, c_ln_b, c_w_s, c_b_s, c_w_out,
           moe_w_router, moe_b_router, moe_w1, moe_b1, moe_w2, moe_b2, ln_g, ln_b):
    x = jnp.concatenate([x_prompt.reshape(N_PROMPT, D_MODEL), x_sample.reshape(N_SAMPLE, D_MODEL)], axis=0)
    lnrow = lambda layer, j: (ln_g[layer, j].reshape(1, D_MODEL), ln_b[layer, j].reshape(1, D_MODEL))
    ri = jnp.arange(TM)
    tri = (ri[:, None] > ri[None, :]).astype(BF16)

    w_in = ab_w_in[0]
    w_main = w_in[:, :W_MAIN].astype(BF16)
    w_ab = jnp.pad(w_in[:, W_MAIN:], ((0, 0), (0, LANES - 2 * H_B))).astype(BF16)
    alog_row = jnp.pad(ab_a_log[0], (0, LANES - H_B)).reshape(1, LANES)
    dtb_row = jnp.pad(ab_dt_bias[0], (0, LANES - H_B)).reshape(1, LANES)
    bg, u, qkv, z, gb = _proj_ab(x, w_main, w_ab, alog_row, dtb_row)

    ng = ab_norm_g[0].reshape(1, DV)
    ycat = jnp.zeros((N_TOK, D_MODEL), BF16)
    gbr_p = _group_rows(gb[:N_PROMPT], BATCH, SEQ, DN_CHUNK, 1, GDN_CPS)
    ycat, pq8, pu8, p_delta = _gdn(
        qkv, u, bg, z, gbr_p,
        jnp.zeros((BATCH, SUBLANES, D_QKV), F32), jnp.zeros((BATCH, SUBLANES, D_A), F32),
        jnp.zeros((BATCH, H_B, DK, DV), F32), ab_conv_qkv[0], ab_conv_a[0], ng, ycat,
        chunk=DN_CHUNK, bb=1, cps=GDN_CPS, n_seq=BATCH, t_len=SEQ, row_block0=0)
    bb_s = 16
    gbr_s = _group_rows(gb[N_PROMPT:], DEC_BATCH, DEC_SEQ, DEC_SEQ, bb_s, 1)
    ycat, sq8, su8, s_delta = _gdn(
        qkv, u, bg, z, gbr_s,
        _tail8(state_conv_qkv[0], CONV_B - 1), _tail8(state_conv_a[0], CONV_A - 1), state_delta[0],
        ab_conv_qkv[0], ab_conv_a[0], ng, ycat,
        chunk=DEC_SEQ, bb=bb_s, cps=1, n_seq=DEC_BATCH, t_len=DEC_SEQ, row_block0=N_PROMPT // (bb_s * DEC_SEQ))
    wr0, br0 = _router_weights(moe_w_router[0], moe_b_router[0])
    x, gates, dest, counts = _mm_res_ln(ycat, x, ab_w_out[0].astype(BF16), *lnrow(0, 0), wr0, br0, tri)
    x = _moe_post_norm(0, x, gates, dest, counts, moe_w1, moe_b1, moe_w2, moe_b2, ln_g[0, 1], ln_b[0, 1])

    uc, vc = _c_in(x, c_w_in[0].astype(BF16), c_b_in[0].reshape(1, 2 * D_C),
                   c_ln_g[0].reshape(1, D_C), c_ln_b[0].reshape(1, D_C))
    ws = c_w_s[0]
    reps = C_CHUNK // DEC_SEQ
    wmix2 = jnp.stack([ws, jnp.tile(ws[:, :DEC_SEQ, :DEC_SEQ], (1, reps, reps))])
    rc = jnp.arange(C_CHUNK)
    tril = rc[:, None] >= rc[None, :]
    mask2 = jnp.stack([tril, tril & ((rc[:, None] // DEC_SEQ) == (rc[None, :] // DEC_SEQ))]).astype(F32)
    bias_p = jnp.repeat(c_b_s[0].T, D_C // C_GROUPS, axis=1)
    bias2 = jnp.stack([bias_p, jnp.tile(bias_p[:DEC_SEQ], (reps, 1))])
    wr1, br1 = _router_weights(moe_w_router[1], moe_b_router[1])
    x, gates, dest, counts = _c_out(uc, vc, x, wmix2, mask2, bias2, c_w_out[0].astype(BF16), *lnrow(1, 0),
                                    wr1, br1, tri)
    x = _moe_post_norm(1, x, gates, dest, counts, moe_w1, moe_b1, moe_w2, moe_b2, ln_g[1, 1], ln_b[1, 1])

    y_prompt = x[:N_PROMPT].reshape(BATCH, SEQ, D_MODEL)
    y_sample = x[N_PROMPT:].reshape(DEC_BATCH, DEC_SEQ, D_MODEL)
    ka, kq = CONV_A - 1, CONV_B - 1
    return (y_prompt, y_sample,
            pu8[None, :, SUBLANES - ka:], pq8[None, :, SUBLANES - kq:], p_delta[None],
            su8[None, :, SUBLANES - ka:], sq8[None, :, SUBLANES - kq:], s_delta[None],
            vc[N_PROMPT:].reshape(1, DEC_BATCH, DEC_SEQ, D_C))
```

```python
import functools
import math

import jax
import jax.numpy as jnp
from jax import lax
from jax.experimental import pallas as pl
from jax.experimental.pallas import tpu as pltpu

F32 = jnp.float32
BF16 = jnp.bfloat16
I32 = jnp.int32

D_MODEL = 1024
BATCH = 8
SEQ = 2048
DEC_BATCH = 128
DEC_SEQ = 8
N_PROMPT = BATCH * SEQ
N_SAMPLE = DEC_BATCH * DEC_SEQ
N_TOK = N_PROMPT + N_SAMPLE
D_A = 512
CONV_A = 3
H_B = 4
DK = 128
DV = 128
D_QKV = 1536
CONV_B = 4
DN_CHUNK = 64
W_MAIN = 3 * D_A + D_QKV + H_B * DV
D_C = 1024
C_GROUPS = 8
C_CHUNK = 128
N_EXP = 32
TOP_K = 4
D_EXP = 1024
SWIGLU_ALPHA = 1.702
SWIGLU_LIMIT = 7.0
DEEPNORM_ALPHA = 4.0 ** 0.25
LN_EPS = 1e-5
RMS_EPS = 1e-6

LANES = 128
SUBLANES = 8
VMEM_LIMIT = 56 * 1024 * 1024

TM = 512
TM_E = 256
TM_C = 256
TM_D = 256
GDN_CPS = 4
N_PAIRS = N_TOK * TOP_K
N_EBLOCKS = (N_PAIRS + N_EXP * (TM_E - 1)) // TM_E
ROWS_TOTAL = N_EBLOCKS * TM_E
RANK_BITS = 15
assert N_TOK <= 1 << RANK_BITS


def _cparams(sem):
    return pltpu.CompilerParams(dimension_semantics=sem, vmem_limit_bytes=VMEM_LIMIT)


def _layer_norm(t, g, b):
    mu = jnp.mean(t, axis=-1, keepdims=True)
    d = t - mu
    var = jnp.mean(d * d, axis=-1, keepdims=True)
    return d * lax.rsqrt(var + LN_EPS) * g + b


def _bdot(a, b):
    return jnp.dot(a.astype(BF16), b.astype(BF16), preferred_element_type=F32)


def _proj_ab_body(x_ref, w_ref, wab_ref, alog_ref, dtb_ref, bg_ref, u_ref, qkv_ref, z_ref, gb_ref):
    xb = x_ref[...].astype(BF16)

    def mm(lo, hi):
        return jnp.dot(xb, w_ref[:, lo:hi], preferred_element_type=F32)

    bg_ref[...] = mm(0, D_A)
    u_ref[...] = mm(D_A, 2 * D_A) * mm(2 * D_A, 3 * D_A)
    for c in range(D_QKV // 512):
        qkv_ref[:, c * 512:(c + 1) * 512] = mm(3 * D_A + c * 512, 3 * D_A + (c + 1) * 512)
    z_ref[...] = mm(3 * D_A + D_QKV, W_MAIN)
    ab = jnp.dot(xb, wab_ref[...], preferred_element_type=F32)
    g = -jnp.exp(alog_ref[...]) * jax.nn.softplus(ab + dtb_ref[...])
    beta = jax.nn.sigmoid(ab)
    lane = lax.broadcasted_iota(I32, ab.shape, 1)
    gb_ref[...] = jnp.where(lane < H_B, g, beta)


def _proj_ab(x, w_main, w_ab, alog_row, dtb_row):
    n = x.shape[0]
    row = lambda w: pl.BlockSpec((TM, w), lambda i: (i, 0))
    full = lambda a: pl.BlockSpec(a.shape, lambda i: (0,) * a.ndim)
    return pl.pallas_call(
        _proj_ab_body,
        grid=(n // TM,),
        in_specs=[row(D_MODEL), full(w_main), full(w_ab), full(alog_row), full(dtb_row)],
        out_specs=[row(D_A), row(D_A), row(D_QKV), row(D_A), row(LANES)],
        out_shape=[jax.ShapeDtypeStruct((n, D_A), F32), jax.ShapeDtypeStruct((n, D_A), F32),
                   jax.ShapeDtypeStruct((n, D_QKV), F32), jax.ShapeDtypeStruct((n, D_A), F32),
                   jax.ShapeDtypeStruct((n, LANES), F32)],
        compiler_params=_cparams(("arbitrary",)),
        name="proj_ab",
    )(x, w_main, w_ab, alog_row, dtb_row)


def _shift_rows(x, prev8, s):
    if s == 0:
        return x
    xr = pltpu.roll(x, s, axis=0)
    pr = pltpu.roll(prev8, s, axis=0)
    rid = lax.broadcasted_iota(I32, pr.shape, 0)
    head = jnp.where(rid < s, pr, xr[0:SUBLANES])
    if x.shape[0] == SUBLANES:
        return head
    return jnp.concatenate([head, xr[SUBLANES:]], axis=0)


def _causal_conv(x, prev8, w_ref, taps):
    y = x * w_ref[taps - 1:taps, :]
    for s in range(1, taps):
        y = y + _shift_rows(x, prev8, s) * w_ref[taps - 1 - s:taps - s, :]
    return y


def _lane_scan(x, pos, chunk, reverse):
    s = 1
    while s < chunk:
        if reverse:
            x = x + jnp.where(pos < chunk - s, pltpu.roll(x, LANES - s, axis=1), 0.0)
        else:
            x = x + jnp.where(pos >= s, pltpu.roll(x, s, axis=1), 0.0)
        s *= 2
    return x


def _per_row(row):
    return jnp.broadcast_to(row, (LANES, LANES)).T


def _gdn_body(chunk, bb, cps, n_steps,
              qkv_ref, u_ref, bg_ref, z_ref, gbr_ref, pq_ref, pu_ref, s0_ref, wq_ref, wa_ref, ng_ref, y_any,
              ycat_ref, nq_ref, nu_ref, sn_ref, cq_scr, cu_scr, s_scr):
    del y_any
    n = pl.program_id(1)

    @pl.when(n == 0)
    def _():
        cq_scr[...] = pq_ref[...]
        cu_scr[...] = pu_ref[...]
        s_scr[...] = s0_ref[...]

    gsz = LANES // chunk
    n_groups = bb * cps * H_B // gsz
    levels = int(math.log2(chunk))
    span = cps * chunk

    ii = lax.broadcasted_iota(I32, (LANES, LANES), 0)
    jj = lax.broadcasted_iota(I32, (LANES, LANES), 1)
    same = (ii // chunk) == (jj // chunk)
    m_incl = same & (ii >= jj)
    m_strict = same & (ii > jj)
    eye = (ii == jj).astype(F32)
    pos = lax.broadcasted_iota(I32, (SUBLANES, LANES), 1) % chunk

    qs, ks, vs = {}, {}, {}
    for b in range(bb):
        rows = slice(b * span, (b + 1) * span)
        x = qkv_ref[rows, :]
        qc = _causal_conv(x, cq_scr[b], wq_ref, CONV_B)
        qc = qc * jax.nn.sigmoid(qc)
        cq_scr[b] = x[span - SUBLANES:span]
        uu = u_ref[rows, :]
        ca = _causal_conv(uu, cu_scr[b], wa_ref, CONV_A)
        cu_scr[b] = uu[span - SUBLANES:span]
        ycat_ref[rows, 0:D_A] = (bg_ref[rows, :] * ca).astype(BF16)
        for j in range(cps):
            r = slice(j * chunk, (j + 1) * chunk)
            for h in range(H_B):
                qh = qc[r, h * DK:(h + 1) * DK]
                kh = qc[r, H_B * DK + h * DK:H_B * DK + (h + 1) * DK]
                vh = qc[r, 2 * H_B * DK + h * DV:2 * H_B * DK + (h + 1) * DV]
                key = (b * cps + j, h)
                qs[key] = qh * (lax.rsqrt(jnp.sum(qh * qh, axis=-1, keepdims=True) + RMS_EPS) * (DK ** -0.5))
                ks[key] = kh * lax.rsqrt(jnp.sum(kh * kh, axis=-1, keepdims=True) + RMS_EPS)
                vs[key] = vh

    for gi in range(n_groups):
        blocks = [divmod(gi * gsz + t, H_B) for t in range(gsz)]
        cat = lambda d: jnp.concatenate([d[uh] for uh in blocks], axis=0) if gsz > 1 else d[blocks[0]]
        qg, kg, vg = cat(qs), cat(ks), cat(vs)

        tile = gbr_ref[gi]
        gc = _lane_scan(tile, pos, chunk, False)
        rs = _lane_scan(tile, pos, chunk, True) - tile
        gc_row = gc[0:1]
        gc_m = _per_row(gc_row)
        rs_m = _per_row(rs[0:1])
        beta_m = _per_row(tile[1:2])
        diff = gc_m - jnp.broadcast_to(gc_row, (LANES, LANES))
        decay = jnp.where(m_incl, jnp.exp(jnp.where(m_incl, diff, 0.0)), 0.0)
        eg = jnp.exp(gc_m)
        etot = jnp.exp(gc_m + rs_m)

        kb = kg * beta_m
        kgb = kg.astype(BF16)
        a_mat = lax.dot_general(kb.astype(BF16), kgb, (((1,), (1,)), ((), ())), preferred_element_type=F32)
        lm = jnp.where(m_strict, a_mat * decay, 0.0)
        attn = lax.dot_general(qg.astype(BF16), kgb, (((1,), (1,)), ((), ())), preferred_element_type=F32) * decay

        p = eye - lm
        m = _bdot(lm, lm)
        for lvl in range(1, levels):
            p = p + _bdot(p, m)
            if lvl < levels - 1:
                m = _bdot(m, m)
        uw = _bdot(p, jnp.concatenate([vg * beta_m, kb * eg], axis=1))
        u_all, w_all = uw[:, :DV], uw[:, DV:]
        qe = qg * eg
        kdec = kg * jnp.exp(rs_m)

        vnew, qsv = [], []
        for t, (unit, h) in enumerate(blocks):
            b = unit // cps
            r = slice(t * chunk, (t + 1) * chunk)
            s_bf = s_scr[b, h].astype(BF16)
            lhs = jnp.concatenate([w_all[r], qe[r]], axis=0).astype(BF16)
            both = jnp.dot(lhs, s_bf, preferred_element_type=F32)
            vnew.append(u_all[r] - both[:chunk])
            qsv.append(both[chunk:])
        vnew_g = jnp.concatenate(vnew, axis=0) if gsz > 1 else vnew[0]
        qs_g = jnp.concatenate(qsv, axis=0) if gsz > 1 else qsv[0]
        o = qs_g + _bdot(attn, vnew_g)
        o = o * lax.rsqrt(jnp.mean(o * o, axis=-1, keepdims=True) + RMS_EPS) * ng_ref[...]

        for t, (unit, h) in enumerate(blocks):
            b = unit // cps
            r = slice(t * chunk, (t + 1) * chunk)
            rows = slice(unit * chunk, (unit + 1) * chunk)
            zz = z_ref[rows, h * DV:(h + 1) * DV]
            ycat_ref[rows, D_A + h * DV:D_A + (h + 1) * DV] = (o[r] * (zz * jax.nn.sigmoid(zz))).astype(BF16)
            upd = lax.dot_general(kdec[r].astype(BF16), vnew[t].astype(BF16), (((0,), (0,)), ((), ())),
                                  preferred_element_type=F32)
            scale = jnp.broadcast_to(etot[t * chunk:t * chunk + 1, :], (DK, DV))
            s_scr[b, h] = s_scr[b, h] * scale + upd

    @pl.when(n == n_steps - 1)
    def _():
        nq_ref[...] = cq_scr[...]
        nu_ref[...] = cu_scr[...]
        sn_ref[...] = s_scr[...]


def _gdn_phased_body(chunk, cps, n_steps,
                     qkv_ref, u_ref, bg_ref, z_ref, gbr_ref, pq_ref, pu_ref, s0_ref, wq_ref, wa_ref, ng_ref, y_any,
                     ycat_ref, nq_ref, nu_ref, sn_ref, cq_scr, cu_scr, s_scr,
                     kf, qf, vf, k16, kb16, q16, kdec16, att16, m16, c16, vk16, uw16,
                     dec, pm, qe, etot, dm, om, n16, bm):
    del y_any
    n = pl.program_id(1)

    @pl.when(n == 0)
    def _():
        cq_scr[...] = pq_ref[...]
        cu_scr[...] = pu_ref[...]
        s_scr[...] = s0_ref[...]

    gsz = LANES // chunk
    n_groups = cps * H_B // gsz
    levels = int(math.log2(chunk))
    span = cps * chunk
    groups = range(n_groups)
    blocks_of = lambda gi: [divmod(gi * gsz + t, H_B) for t in range(gsz)]
    rows_of = lambda t: slice(t * chunk, (t + 1) * chunk)

    ii = lax.broadcasted_iota(I32, (LANES, LANES), 0)
    jj = lax.broadcasted_iota(I32, (LANES, LANES), 1)
    same = (ii // chunk) == (jj // chunk)
    m_incl = same & (ii >= jj)
    m_strict = same & (ii > jj)
    eye = (ii == jj).astype(F32)
    pos = lax.broadcasted_iota(I32, (SUBLANES, LANES), 1) % chunk
    nt = (((1,), (1,)), ((), ()))
    tn = (((0,), (0,)), ((), ()))

    x = qkv_ref[...]
    qc = _causal_conv(x, cq_scr[0], wq_ref, CONV_B)
    qc = qc * jax.nn.sigmoid(qc)
    cq_scr[0] = x[span - SUBLANES:span]
    uu = u_ref[...]
    ca = _causal_conv(uu, cu_scr[0], wa_ref, CONV_A)
    cu_scr[0] = uu[span - SUBLANES:span]
    ycat_ref[:, 0:D_A] = (bg_ref[...] * ca).astype(BF16)
    for gi in groups:
        for t, (j, h) in enumerate(blocks_of(gi)):
            r = rows_of(j)
            qh = qc[r, h * DK:(h + 1) * DK]
            kh = qc[r, H_B * DK + h * DK:H_B * DK + (h + 1) * DK]
            qf[gi, rows_of(t), :] = qh * (lax.rsqrt(jnp.sum(qh * qh, axis=-1, keepdims=True) + RMS_EPS)
                                          * (DK ** -0.5))
            kf[gi, rows_of(t), :] = kh * lax.rsqrt(jnp.sum(kh * kh, axis=-1, keepdims=True) + RMS_EPS)
            vf[gi, rows_of(t), :] = qc[r, 2 * H_B * DK + h * DV:2 * H_B * DK + (h + 1) * DV]

    for gi in groups:
        tile = gbr_ref[gi]
        gc = _lane_scan(tile, pos, chunk, False)
        rs = _lane_scan(tile, pos, chunk, True) - tile
        gc_row = gc[0:1]
        gc_m = _per_row(gc_row)
        rs_m = _per_row(rs[0:1])
        beta_m = _per_row(tile[1:2])
        diff = gc_m - jnp.broadcast_to(gc_row, (LANES, LANES))
        dec[gi] = jnp.where(m_incl, jnp.exp(jnp.where(m_incl, diff, 0.0)), 0.0)
        eg = jnp.exp(gc_m)
        etot[gi] = jnp.exp(gc_m + rs_m)
        kg = kf[gi]
        kb = kg * beta_m
        k16[gi] = kg.astype(BF16)
        kb16[gi] = kb.astype(BF16)
        q16[gi] = qf[gi].astype(BF16)
        qe[gi] = qf[gi] * eg
        kdec16[gi] = (kg * jnp.exp(rs_m)).astype(BF16)
        vk16[gi, :, 0:DV] = (vf[gi] * beta_m).astype(BF16)
        vk16[gi, :, DV:] = (kb * eg).astype(BF16)

    for gi in groups:
        a_mat = lax.dot_general(kb16[gi], k16[gi], nt, preferred_element_type=F32)
        lm = jnp.where(m_strict, a_mat * dec[gi], 0.0)
        pm[gi] = eye - lm
        lm16 = lm.astype(BF16)
        m16[gi] = jnp.dot(lm16, lm16, preferred_element_type=F32).astype(BF16)
        att16[gi] = (lax.dot_general(q16[gi], k16[gi], nt, preferred_element_type=F32) * dec[gi]).astype(BF16)

    for lvl in range(1, levels):
        for gi in groups:
            pm[gi] = pm[gi] + jnp.dot(pm[gi].astype(BF16), m16[gi], preferred_element_type=F32)
        if lvl < levels - 1:
            for gi in groups:
                m16[gi] = jnp.dot(m16[gi], m16[gi], preferred_element_type=F32).astype(BF16)

    for gi in groups:
        uw16[gi] = jnp.dot(pm[gi].astype(BF16), vk16[gi], preferred_element_type=F32).astype(BF16)
    for gi in groups:
        au = jnp.dot(att16[gi], uw16[gi], preferred_element_type=F32)
        dm[gi] = au[:, :DV]
        c16[gi] = (qe[gi] - au[:, DV:]).astype(BF16)
        for t in range(gsz):
            r = rows_of(t)
            nb = lax.dot_general(kdec16[gi, r, :], uw16[gi, r, :], tn, preferred_element_type=F32)
            bm[gi * gsz + t] = nb[:, :DV]
            n16[gi * gsz + t] = nb[:, DV:].astype(BF16)

    for gi in groups:
        for t, (j, h) in enumerate(blocks_of(gi)):
            r = rows_of(t)
            s_old = s_scr[0, h]
            lhs = jnp.concatenate([c16[gi, r, :], n16[gi * gsz + t]], axis=0)
            both = jnp.dot(lhs, s_old.astype(BF16), preferred_element_type=F32)
            om[gi, r, :] = both[:chunk] + dm[gi, r, :]
            scale = jnp.broadcast_to(etot[gi, t * chunk:t * chunk + 1, :], (DK, DV))
            s_scr[0, h] = s_old * scale - both[chunk:] + bm[gi * gsz + t]

    for gi in groups:
        o = om[gi]
        o = o * lax.rsqrt(jnp.mean(o * o, axis=-1, keepdims=True) + RMS_EPS) * ng_ref[...]
        for t, (j, h) in enumerate(blocks_of(gi)):
            zz = z_ref[rows_of(j), h * DV:(h + 1) * DV]
            ycat_ref[rows_of(j), D_A + h * DV:D_A + (h + 1) * DV] = (
                o[rows_of(t)] * (zz * jax.nn.sigmoid(zz))).astype(BF16)

    @pl.when(n == n_steps - 1)
    def _():
        nq_ref[...] = cq_scr[...]
        nu_ref[...] = cu_scr[...]
        sn_ref[...] = s_scr[...]


def _gdn_phased_scratch(chunk, cps):
    g = cps * H_B * chunk // LANES
    nblk = cps * H_B
    mat = lambda dt, n=g, w=LANES: pltpu.VMEM((n, LANES, w), dt)
    return ([mat(F32)] * 3 + [mat(BF16)] * 7 + [mat(BF16, w=2 * LANES)] * 2 + [mat(F32)] * 6
            + [mat(BF16, n=nblk), mat(F32, n=nblk)])


def _gdn(qkv, u, bg, z, gbr, prev_q, prev_u, s0, wq, wa, ng, ycat, *, chunk, bb, cps, n_seq, t_len, row_block0,
         phased=False):
    rb = bb * cps * chunk
    gs = rb * H_B // LANES
    n_steps = t_len // (cps * chunk)
    rowmap = lambda i, n: (row_block0 + i * n_steps + n, 0)
    row = lambda w: pl.BlockSpec((rb, w), rowmap)
    seq3 = lambda w: pl.BlockSpec((bb, SUBLANES, w), lambda i, n: (i, 0, 0))
    full = lambda a: pl.BlockSpec(a.shape, lambda i, n: (0,) * a.ndim)
    st = pl.BlockSpec((bb, H_B, DK, DV), lambda i, n: (i, 0, 0, 0))
    if phased:
        assert bb == 1
        body = functools.partial(_gdn_phased_body, chunk, cps, n_steps)
        extra_scratch = _gdn_phased_scratch(chunk, cps)
    else:
        body = functools.partial(_gdn_body, chunk, bb, cps, n_steps)
        extra_scratch = []
    return pl.pallas_call(
        body,
        grid=(n_seq // bb, n_steps),
        in_specs=[row(D_QKV), row(D_A), row(D_A), row(D_A),
                  pl.BlockSpec((None, gs, SUBLANES, LANES), lambda i, n: (i * n_steps + n, 0, 0, 0)),
                  seq3(D_QKV), seq3(D_A), st, full(wq), full(wa), full(ng),
                  pl.BlockSpec(memory_space=pl.ANY)],
        out_specs=[pl.BlockSpec((rb, D_MODEL), rowmap), seq3(D_QKV), seq3(D_A), st],
        out_shape=[jax.ShapeDtypeStruct(ycat.shape, BF16),
                   jax.ShapeDtypeStruct((n_seq, SUBLANES, D_QKV), F32),
                   jax.ShapeDtypeStruct((n_seq, SUBLANES, D_A), F32),
                   jax.ShapeDtypeStruct((n_seq, H_B, DK, DV), F32)],
        scratch_shapes=[pltpu.VMEM((bb, SUBLANES, D_QKV), F32), pltpu.VMEM((bb, SUBLANES, D_A), F32),
                        pltpu.VMEM((bb, H_B, DK, DV), F32)] + extra_scratch,
        input_output_aliases={11: 0},
        compiler_params=_cparams(("arbitrary", "arbitrary")),
        name=f"gdn_c{chunk}",
    )(qkv, u, bg, z, gbr, prev_q, prev_u, s0, wq, wa, ng, ycat)


def _group_rows(gb, n_seq, t_len, chunk, bb, cps):
    n_steps = t_len // (cps * chunk)
    gs = bb * cps * chunk * H_B // LANES
    g = gb[:, :2 * H_B].reshape(n_seq // bb, bb, n_steps, cps, chunk, 2, H_B)
    g = jnp.transpose(g, (0, 2, 5, 1, 3, 6, 4))
    g = g.reshape(n_seq // bb * n_steps, 2, gs, LANES)
    g = jnp.transpose(g, (0, 2, 1, 3))
    return jnp.pad(g, ((0, 0), (0, 0), (0, SUBLANES - 2), (0, 0)))


def _route(x_new, wr_ref, br_ref, tri_ref, cnt_scr, gate_ref, dest_ref, counts_ref):
    @pl.when(pl.program_id(0) == 0)
    def _():
        cnt_scr[...] = jnp.zeros(cnt_scr.shape, F32)

    xh = x_new.astype(BF16)
    xl = (x_new - xh.astype(F32)).astype(BF16)
    p = jnp.dot(xh, wr_ref[...], preferred_element_type=F32)
    logits = (p[:, :LANES] + p[:, LANES:] + jnp.dot(xl, wr_ref[:, :LANES], preferred_element_type=F32)
              + br_ref[...])
    lane = lax.broadcasted_iota(I32, logits.shape, 1).astype(F32)
    neg = jnp.float32(-jnp.inf)
    l = jnp.where(lane < N_EXP, logits, neg)
    val_out = jnp.full(logits.shape, neg, F32)
    sels, hots = [], []
    for k in range(TOP_K):
        m = jnp.max(l, axis=-1, keepdims=True)
        sel = jnp.min(jnp.where(l == m, lane, float(LANES)), axis=-1, keepdims=True)
        hit = lane == sel
        val_out = jnp.where(lane == k, m, val_out)
        l = jnp.where(hit, neg, l)
        sels.append(sel)
        hots.append(hit.astype(F32))
    e = jnp.exp(val_out - jnp.max(val_out, axis=-1, keepdims=True))
    gate_ref[...] = e / jnp.sum(e, axis=-1, keepdims=True)

    hot = hots[0] + hots[1] + hots[2] + hots[3]
    before = jnp.dot(tri_ref[...], hot.astype(BF16), preferred_element_type=F32) + cnt_scr[...]
    dest = jnp.zeros(logits.shape, F32)
    for k in range(TOP_K):
        rank = jnp.sum(hots[k] * before, axis=-1, keepdims=True)
        dest = jnp.where(lane == k, sels[k] * float(1 << RANK_BITS) + rank, dest)
    dest_ref[...] = dest.astype(I32)
    cnt_scr[...] = cnt_scr[...] + jnp.sum(hot, axis=0, keepdims=True)
    counts_ref[...] = cnt_scr[...]


_ROUTE_OUT_SPECS = [pl.BlockSpec((TM, LANES), lambda i: (i, 0)), pl.BlockSpec((TM, LANES), lambda i: (i, 0)),
                    pl.BlockSpec((1, LANES), lambda i: (0, 0))]


def _route_out_shapes(n):
    return [jax.ShapeDtypeStruct((n, LANES), F32), jax.ShapeDtypeStruct((n, LANES), I32),
            jax.ShapeDtypeStruct((1, LANES), F32)]


def _mm_res_ln_body(y_ref, x_ref, w_ref, g_ref, b_ref, wr_ref, br_ref, tri_ref,
                    o_ref, gate_ref, dest_ref, counts_ref, cnt_scr):
    h = jnp.dot(y_ref[...].astype(BF16), w_ref[...], preferred_element_type=F32)
    xn = _layer_norm(DEEPNORM_ALPHA * x_ref[...] + h, g_ref[...], b_ref[...])
    o_ref[...] = xn
    _route(xn, wr_ref, br_ref, tri_ref, cnt_scr, gate_ref, dest_ref, counts_ref)


def _mm_res_ln(y, x, w, g, b, wr, br, tri):
    n = x.shape[0]
    row = pl.BlockSpec((TM, D_MODEL), lambda i: (i, 0))
    full = lambda a: pl.BlockSpec(a.shape, lambda i: (0,) * a.ndim)
    return pl.pallas_call(
        _mm_res_ln_body,
        grid=(n // TM,),
        in_specs=[row, row, full(w), full(g), full(b), full(wr), full(br), full(tri)],
        out_specs=[row] + _ROUTE_OUT_SPECS,
        out_shape=[jax.ShapeDtypeStruct((n, D_MODEL), F32)] + _route_out_shapes(n),
        scratch_shapes=[pltpu.VMEM((1, LANES), F32)],
        compiler_params=_cparams(("arbitrary",)),
        name="mm_res_ln",
    )(y, x, w, g, b, wr, br, tri)


def _gelu(x):
    return 0.5 * x * (1.0 + lax.erf(x * (2.0 ** -0.5)))


def _c_in_body(x_ref, w_ref, b_ref, lg_ref, lb_ref, u_ref, v_ref):
    xb = x_ref[...].astype(BF16)
    u_ref[...] = _gelu(jnp.dot(xb, w_ref[:, :D_C], preferred_element_type=F32) + b_ref[:, :D_C])
    v = _gelu(jnp.dot(xb, w_ref[:, D_C:], preferred_element_type=F32) + b_ref[:, D_C:])
    v_ref[...] = _layer_norm(v, lg_ref[...], lb_ref[...])


def _c_in(x, w, b, lg, lb):
    n = x.shape[0]
    row = pl.BlockSpec((TM, D_MODEL), lambda i: (i, 0))
    full = lambda a: pl.BlockSpec(a.shape, lambda i: (0,) * a.ndim)
    return pl.pallas_call(
        _c_in_body,
        grid=(n // TM,),
        in_specs=[row, full(w), full(b), full(lg), full(lb)],
        out_specs=[row, row],
        out_shape=[jax.ShapeDtypeStruct((n, D_C), F32), jax.ShapeDtypeStruct((n, D_C), F32)],
        compiler_params=_cparams(("arbitrary",)),
        name="c_in",
    )(x, w, b, lg, lb)


def _c_out_body(u_ref, v_ref, x_ref, wmix_ref, mask_ref, bias_ref, wout_ref, g_ref, b_ref, wr_ref, br_ref, tri_ref,
                o_ref, gate_ref, dest_ref, counts_ref, us_scr, cnt_scr):
    mix = [(wmix_ref[gi] * mask_ref[...]).astype(BF16) for gi in range(C_GROUPS)]
    for t in range(TM // C_CHUNK):
        rows = slice(t * C_CHUNK, (t + 1) * C_CHUNK)
        for gi in range(C_GROUPS):
            cols = slice(gi * LANES, (gi + 1) * LANES)
            s = jnp.dot(mix[gi], v_ref[rows, cols].astype(BF16), preferred_element_type=F32) + bias_ref[:, cols]
            us_scr[rows, cols] = (u_ref[rows, cols] * s).astype(BF16)
    h = jnp.dot(us_scr[...], wout_ref[...], preferred_element_type=F32)
    xn = _layer_norm(DEEPNORM_ALPHA * x_ref[...] + h, g_ref[...], b_ref[...])
    o_ref[...] = xn
    _route(xn, wr_ref, br_ref, tri_ref, cnt_scr, gate_ref, dest_ref, counts_ref)


def _c_out(u, v, x, wmix2, mask2, bias2, wout, g, b, wr, br, tri):
    n = x.shape[0]
    first_sample_step = N_PROMPT // TM
    sel = lambda i: jnp.where(i >= first_sample_step, 1, 0)
    row = pl.BlockSpec((TM, D_MODEL), lambda i: (i, 0))
    full = lambda a: pl.BlockSpec(a.shape, lambda i: (0,) * a.ndim)
    return pl.pallas_call(
        _c_out_body,
        grid=(n // TM,),
        in_specs=[row, row, row,
                  pl.BlockSpec((None, C_GROUPS, C_CHUNK, C_CHUNK), lambda i: (sel(i), 0, 0, 0)),
                  pl.BlockSpec((None, C_CHUNK, C_CHUNK), lambda i: (sel(i), 0, 0)),
                  pl.BlockSpec((None, C_CHUNK, D_C), lambda i: (sel(i), 0, 0)),
                  full(wout), full(g), full(b), full(wr), full(br), full(tri)],
        out_specs=[row] + _ROUTE_OUT_SPECS,
        out_shape=[jax.ShapeDtypeStruct((n, D_MODEL), F32)] + _route_out_shapes(n),
        scratch_shapes=[pltpu.VMEM((TM, D_C), BF16), pltpu.VMEM((1, LANES), F32)],
        compiler_params=_cparams(("arbitrary",)),
        name="c_out",
    )(u, v, x, wmix2, mask2, bias2, wout, g, b, wr, br, tri)


def _for_each_row(n_rows, fn):
    def group(t8, c):
        base = pl.multiple_of(t8 * SUBLANES, SUBLANES)
        for s in range(SUBLANES):
            fn(t8, base, s)
        return c
    lax.fori_loop(0, n_rows // SUBLANES, group, 0)


def _dispatch_body(pad_start_ref, pad_n_ref, n_used_ref, dest_ref, x_ref, rows_hbm, zbuf, sem, zsem):
    i = pl.program_id(0)

    @pl.when(i == 0)
    def _():
        zbuf[...] = jnp.zeros(zbuf.shape, F32)

    @pl.when(i < N_EXP)
    def _():
        n = pad_n_ref[i]
        start = pad_start_ref[i]
        odd = n & (SUBLANES - 1)
        copies = [(pltpu.make_async_copy(zbuf.at[pl.ds(0, 1)], rows_hbm.at[pl.ds(start + s, 1)], zsem), s < odd)
                  for s in range(SUBLANES - 1)]
        off = start + odd
        for size in [1 << p for p in reversed(range(3, int(math.log2(TM_E))))]:
            dst = rows_hbm.at[pl.ds(pl.multiple_of(off, SUBLANES), size)]
            copies.append((pltpu.make_async_copy(zbuf.at[pl.ds(0, size)], dst, zsem), (n & size) != 0))
            off = off + (n & size)
        for cp, used in copies:
            pl.when(used)(cp.start)
        for cp, used in copies:
            pl.when(used)(cp.wait)

    blk = n_used_ref[0] + (i - N_EXP)

    @pl.when(jnp.logical_and(i >= N_EXP, blk < N_EBLOCKS))
    def _():
        cp = pltpu.make_async_copy(zbuf, rows_hbm.at[pl.ds(pl.multiple_of(blk * TM_E, TM_E), TM_E)], zsem)
        cp.start()
        cp.wait()

    def push(t8, base, s):
        src = x_ref.at[pl.ds(base, SUBLANES)].at[pl.ds(s, 1)]
        for k in range(TOP_K):
            d = dest_ref[0, 0, (t8 * SUBLANES + s) * TOP_K + k]
            pltpu.make_async_copy(src, rows_hbm.at[pl.ds(d, 1)], sem).start(priority=k % 2)
    _for_each_row(TM_D, push)

    def drain(t8, base, s):
        for k in range(TOP_K):
            pltpu.make_async_copy(x_ref.at[pl.ds(0, 1)], rows_hbm.at[pl.ds(0, 1)], sem).wait()
    _for_each_row(TM_D, drain)


def _dispatch(pads, dest3, x):
    n = x.shape[0]
    assert n // TM_D >= N_EXP + (N_EBLOCKS - N_PAIRS // TM_E)
    return pl.pallas_call(
        _dispatch_body,
        grid_spec=pltpu.PrefetchScalarGridSpec(
            num_scalar_prefetch=3,
            grid=(n // TM_D,),
            in_specs=[pl.BlockSpec((1, 1, TM_D * TOP_K), lambda i, *_: (i, 0, 0), memory_space=pltpu.SMEM),
                      pl.BlockSpec((TM_D, D_MODEL), lambda i, *_: (i, 0))],
            out_specs=pl.BlockSpec(memory_space=pl.ANY),
            scratch_shapes=[pltpu.VMEM((TM_E, D_MODEL), F32), pltpu.SemaphoreType.DMA(()),
                            pltpu.SemaphoreType.DMA(())]),
        out_shape=jax.ShapeDtypeStruct((ROWS_TOTAL, D_MODEL), F32),
        compiler_params=_cparams(("arbitrary",)),
        name="moe_dispatch",
    )(*pads, dest3, x)


def _experts_body(bexp_ref, bval_ref, x_ref, w1_ref, b1_ref, w2_ref, b2_ref, o_ref, w1b, w2b):
    i = pl.program_id(0)
    valid = bval_ref[i] != 0
    fresh = jnp.logical_or(i == 0, bexp_ref[i] != bexp_ref[jnp.maximum(i - 1, 0)])

    @pl.when(jnp.logical_and(valid, fresh))
    def _():
        def cast_rows(r, c):
            rows = pl.ds(pl.multiple_of(r * 128, 128), 128)
            w1b[rows, :] = w1_ref[rows, :].astype(BF16)
            w2b[rows, :] = w2_ref[rows, :].astype(BF16)
            return c
        lax.fori_loop(0, D_MODEL // 128, cast_rows, 0)

    @pl.when(jnp.logical_not(valid))
    def _():
        o_ref[...] = jnp.zeros(o_ref.shape, F32)

    @pl.when(valid)
    def _():
        xb = x_ref[...].astype(BF16)
        glu = jnp.dot(xb, w1b[:, :D_EXP], preferred_element_type=F32) + b1_ref[:, :D_EXP]
        lin = jnp.dot(xb, w1b[:, D_EXP:], preferred_element_type=F32) + b1_ref[:, D_EXP:]
        glu = jnp.minimum(glu, SWIGLU_LIMIT)
        lin = jnp.clip(lin, -SWIGLU_LIMIT, SWIGLU_LIMIT)
        act = glu * jax.nn.sigmoid(SWIGLU_ALPHA * glu) * (lin + 1.0)
        o_ref[...] = jnp.dot(act.astype(BF16), w2b[...], preferred_element_type=F32) + b2_ref[...]


def _experts(layer, tables, x_rows, w1, b1, w2, b2):
    wspec = lambda shape: pl.BlockSpec((None, None) + shape, lambda i, be, bv: (layer, be[i], 0, 0))
    rows = pl.BlockSpec((TM_E, D_MODEL), lambda i, be, bv: (i, 0))
    return pl.pallas_call(
        _experts_body,
        grid_spec=pltpu.PrefetchScalarGridSpec(
            num_scalar_prefetch=2,
            grid=(N_EBLOCKS,),
            in_specs=[rows, wspec((D_MODEL, 2 * D_EXP)), wspec((1, 2 * D_EXP)),
                      wspec((D_EXP, D_MODEL)), wspec((1, D_MODEL))],
            out_specs=rows,
            scratch_shapes=[pltpu.VMEM((D_MODEL, 2 * D_EXP), BF16), pltpu.VMEM((D_EXP, D_MODEL), BF16)]),
        out_shape=jax.ShapeDtypeStruct((ROWS_TOTAL, D_MODEL), F32),
        compiler_params=_cparams(("arbitrary",)),
        name="moe_experts",
    )(*tables, x_rows, w1, b1, w2, b2)


def _combine_ln_body(dest_ref, gate_ref, x_ref, rows_hbm, g_ref, b_ref, o_ref, buf, sem):
    def pull(t8, base, s):
        for k in range(TOP_K):
            d = dest_ref[0, 0, (t8 * SUBLANES + s) * TOP_K + k]
            dst = buf.at[k].at[pl.ds(base, SUBLANES)].at[pl.ds(s, 1)]
            pltpu.make_async_copy(rows_hbm.at[pl.ds(d, 1)], dst, sem).start(priority=k % 2)
    _for_each_row(TM_C, pull)

    def drain(t8, base, s):
        for k in range(TOP_K):
            pltpu.make_async_copy(rows_hbm.at[pl.ds(0, 1)], buf.at[0].at[pl.ds(0, 1)], sem).wait()
    _for_each_row(TM_C, drain)

    gates = gate_ref[...]
    y = buf[0] * gates[:, 0:1]
    for k in range(1, TOP_K):
        y = y + buf[k] * gates[:, k:k + 1]
    o_ref[...] = _layer_norm(DEEPNORM_ALPHA * x_ref[...] + y, g_ref[...], b_ref[...])


def _combine_ln(dest3, gates, x, out_rows, g, b):
    n = x.shape[0]
    full = lambda a: pl.BlockSpec(a.shape, lambda i: (0,) * a.ndim)
    row = pl.BlockSpec((TM_C, D_MODEL), lambda i: (i, 0))
    return pl.pallas_call(
        _combine_ln_body,
        grid=(n // TM_C,),
        in_specs=[pl.BlockSpec((1, 1, TM_C * TOP_K), lambda i: (i, 0, 0), memory_space=pltpu.SMEM),
                  pl.BlockSpec((TM_C, LANES), lambda i: (i, 0)), row,
                  pl.BlockSpec(memory_space=pl.ANY), full(g), full(b)],
        out_specs=row,
        out_shape=jax.ShapeDtypeStruct((n, D_MODEL), F32),
        scratch_shapes=[pltpu.VMEM((TOP_K, TM_C, D_MODEL), F32), pltpu.SemaphoreType.DMA(())],
        compiler_params=_cparams(("arbitrary",)),
        name="moe_combine_ln",
    )(dest3, gates, x, out_rows, g, b)


def _positions(enc, counts):
    nb = (counts + TM_E - 1) // TM_E
    cum = jnp.cumsum(nb)
    first_blk = cum - nb
    experts = jnp.arange(N_EXP, dtype=I32)
    pair_hot = ((enc >> RANK_BITS)[:, :, None] == experts[None, None, :]).astype(I32)
    dest = jnp.sum(pair_hot * (first_blk * TM_E)[None, None, :], axis=2) + (enc & ((1 << RANK_BITS) - 1))
    n_used = cum[-1]
    blk = jnp.arange(N_EBLOCKS, dtype=I32)
    exp = jnp.minimum(jnp.sum((cum[None, :] <= blk[:, None]).astype(I32), axis=1), N_EXP - 1)
    pads = (first_blk * TM_E + counts, nb * TM_E - counts, n_used.reshape(1))
    return dest, (exp, (blk < n_used).astype(I32)), pads


def _moe_post_norm(layer, x, gates, enc, counts, w1, b1, w2, b2, g, b):
    dest4, tables, pads = _positions(enc[:, :TOP_K], counts[0, :N_EXP].astype(I32))
    x_rows = _dispatch(pads, dest4.reshape(N_TOK // TM_D, 1, TM_D * TOP_K), x)
    out_rows = _experts(layer, tables, x_rows, w1, b1.reshape(b1.shape[:2] + (1, 2 * D_EXP)),
                        w2, b2.reshape(b2.shape[:2] + (1, D_MODEL)))
    return _combine_ln(dest4.reshape(N_TOK // TM_C, 1, TM_C * TOP_K), gates, x, out_rows,
                       g.reshape(1, D_MODEL), b.reshape(1, D_MODEL))


def _router_weights(w_r, b_r):
    wh = w_r.astype(BF16)
    wl = (w_r - wh.astype(F32)).astype(BF16)
    pad = lambda a: jnp.pad(a, ((0, 0), (0, LANES - N_EXP)))
    return jnp.concatenate([pad(wh), pad(wl)], axis=1), jnp.pad(b_r, (0, LANES - N_EXP)).reshape(1, LANES)


def _tail8(state, keep):
    return jnp.pad(state, ((0, 0), (SUBLANES - keep, 0), (0, 0)))


def kernel(x_prompt, x_sample, state_conv_a, state_conv_qkv, state_delta, ab_w_in, ab_conv_a, ab_conv_qkv,
           ab_a_log, ab_dt_bias, ab_norm_g, ab_w_out, c_w_in, c_b_in, c_ln_g, c_ln_b, c_w_s, c_b_s, c_w_out,
           moe_w_router, moe_b_router, moe_w1, moe_b1, moe_w2, moe_b2, ln_g, ln_b):
    x = jnp.concatenate([x_prompt.reshape(N_PROMPT, D_MODEL), x_sample.reshape(N_SAMPLE, D_MODEL)], axis=0)
    lnrow = lambda layer, j: (ln_g[layer, j].reshape(1, D_MODEL), ln_b[layer, j].reshape(1, D_MODEL))
    ri = jnp.arange(TM)
    tri = (ri[:, None] > ri[None, :]).astype(BF16)

    w_in = ab_w_in[0]
    w_main = w_in[:, :W_MAIN].astype(BF16)
    w_ab = jnp.pad(w_in[:, W_MAIN:], ((0, 0), (0, LANES - 2 * H_B))).astype(BF16)
    alog_row = jnp.pad(ab_a_log[0], (0, LANES - H_B)).reshape(1, LANES)
    dtb_row = jnp.pad(ab_dt_bias[0], (0, LANES - H_B)).reshape(1, LANES)
    bg, u, qkv, z, gb = _proj_ab(x, w_main, w_ab, alog_row, dtb_row)

    ng = ab_norm_g[0].reshape(1, DV)
    ycat = jnp.zeros((N_TOK, D_MODEL), BF16)
    gbr_p = _group_rows(gb[:N_PROMPT], BATCH, SEQ, DN_CHUNK, 1, GDN_CPS)
    ycat, pq8, pu8, p_delta = _gdn(
        qkv, u, bg, z, gbr_p,
        jnp.zeros((BATCH, SUBLANES, D_QKV), F32), jnp.zeros((BATCH, SUBLANES, D_A), F32),
        jnp.zeros((BATCH, H_B, DK, DV), F32), ab_conv_qkv[0], ab_conv_a[0], ng, ycat,
        chunk=DN_CHUNK, bb=1, cps=GDN_CPS, n_seq=BATCH, t_len=SEQ, row_block0=0, phased=True)
    bb_s = 16
    gbr_s = _group_rows(gb[N_PROMPT:], DEC_BATCH, DEC_SEQ, DEC_SEQ, bb_s, 1)
    ycat, sq8, su8, s_delta = _gdn(
        qkv, u, bg, z, gbr_s,
        _tail8(state_conv_qkv[0], CONV_B - 1), _tail8(state_conv_a[0], CONV_A - 1), state_delta[0],
        ab_conv_qkv[0], ab_conv_a[0], ng, ycat,
        chunk=DEC_SEQ, bb=bb_s, cps=1, n_seq=DEC_BATCH, t_len=DEC_SEQ, row_block0=N_PROMPT // (bb_s * DEC_SEQ))
    wr0, br0 = _router_weights(moe_w_router[0], moe_b_router[0])
    x, gates, dest, counts = _mm_res_ln(ycat, x, ab_w_out[0].astype(BF16), *lnrow(0, 0), wr0, br0, tri)
    x = _moe_post_norm(0, x, gates, dest, counts, moe_w1, moe_b1, moe_w2, moe_b2, ln_g[0, 1], ln_b[0, 1])

    uc, vc = _c_in(x, c_w_in[0].astype(BF16), c_b_in[0].reshape(1, 2 * D_C),
                   c_ln_g[0].reshape(1, D_C), c_ln_b[0].reshape(1, D_C))
    ws = c_w_s[0]
    reps = C_CHUNK // DEC_SEQ
    wmix2 = jnp.stack([ws, jnp.tile(ws[:, :DEC_SEQ, :DEC_SEQ], (1, reps, reps))])
    rc = jnp.arange(C_CHUNK)
    tril = rc[:, None] >= rc[None, :]
    mask2 = jnp.stack([tril, tril & ((rc[:, None] // DEC_SEQ) == (rc[None, :] // DEC_SEQ))]).astype(F32)
    bias_p = jnp.repeat(c_b_s[0].T, D_C // C_GROUPS, axis=1)
    bias2 = jnp.stack([bias_p, jnp.tile(bias_p[:DEC_SEQ], (reps, 1))])
    wr1, br1 = _router_weights(moe_w_router[1], moe_b_router[1])
    x, gates, dest, counts = _c_out(uc, vc, x, wmix2, mask2, bias2, c_w_out[0].astype(BF16), *lnrow(1, 0),
                                    wr1, br1, tri)
    x = _moe_post_norm(1, x, gates, dest, counts, moe_w1, moe_b1, moe_w2, moe_b2, ln_g[1, 1], ln_b[1, 1])

    y_prompt = x[:N_PROMPT].reshape(BATCH, SEQ, D_MODEL)
    y_sample = x[N_PROMPT:].reshape(DEC_BATCH, DEC_SEQ, D_MODEL)
    ka, kq = CONV_A - 1, CONV_B - 1
    return (y_prompt, y_sample,
            pu8[None, :, SUBLANES - ka:], pq8[None, :, SUBLANES - kq:], p_delta[None],
            su8[None, :, SUBLANES - ka:], sq8[None, :, SUBLANES - kq:], s_delta[None],
            vc[N_PROMPT:].reshape(1, DEC_BATCH, DEC_SEQ, D_C))
```

```python
import functools
import math

import jax
import jax.numpy as jnp
from jax import lax
from jax.experimental import pallas as pl
from jax.experimental.pallas import tpu as pltpu

F32 = jnp.float32
BF16 = jnp.bfloat16
I32 = jnp.int32

D_MODEL = 1024
BATCH = 8
SEQ = 2048
DEC_BATCH = 128
DEC_SEQ = 8
N_PROMPT = BATCH * SEQ
N_SAMPLE = DEC_BATCH * DEC_SEQ
N_TOK = N_PROMPT + N_SAMPLE
D_A = 512
CONV_A = 3
H_B = 4
DK = 128
DV = 128
D_QKV = 1536
CONV_B = 4
DN_CHUNK = 64
W_MAIN = 3 * D_A + D_QKV + H_B * DV
D_C = 1024
C_GROUPS = 8
C_CHUNK = 128
N_EXP = 32
TOP_K = 4
D_EXP = 1024
SWIGLU_ALPHA = 1.702
SWIGLU_LIMIT = 7.0
DEEPNORM_ALPHA = 4.0 ** 0.25
LN_EPS = 1e-5
RMS_EPS = 1e-6

LANES = 128
SUBLANES = 8
VMEM_LIMIT = 56 * 1024 * 1024

TM = 512
TM_E = 256
TM_C = 256
TM_D = 256
GDN_CPS = 4
N_PAIRS = N_TOK * TOP_K
N_EBLOCKS = (N_PAIRS + N_EXP * (TM_E - 1)) // TM_E
ROWS_TOTAL = N_EBLOCKS * TM_E
RANK_BITS = 15
assert N_TOK <= 1 << RANK_BITS


def _cparams(sem):
    return pltpu.CompilerParams(dimension_semantics=sem, vmem_limit_bytes=VMEM_LIMIT)


def _layer_norm(t, g, b):
    mu = jnp.mean(t, axis=-1, keepdims=True)
    d = t - mu
    var = jnp.mean(d * d, axis=-1, keepdims=True)
    return d * lax.rsqrt(var + LN_EPS) * g + b


def _bdot(a, b):
    return jnp.dot(a.astype(BF16), b.astype(BF16), preferred_element_type=F32)


def _proj_ab_body(x_ref, w_ref, wab_ref, alog_ref, dtb_ref, bg_ref, u_ref, qkv_ref, z_ref, gb_ref):
    xb = x_ref[...].astype(BF16)

    def mm(lo, hi):
        return jnp.dot(xb, w_ref[:, lo:hi], preferred_element_type=F32)

    bg_ref[...] = mm(0, D_A)
    u_ref[...] = mm(D_A, 2 * D_A) * mm(2 * D_A, 3 * D_A)
    for c in range(D_QKV // 512):
        qkv_ref[:, c * 512:(c + 1) * 512] = mm(3 * D_A + c * 512, 3 * D_A + (c + 1) * 512)
    z_ref[...] = mm(3 * D_A + D_QKV, W_MAIN)
    ab = jnp.dot(xb, wab_ref[...], preferred_element_type=F32)
    g = -jnp.exp(alog_ref[...]) * jax.nn.softplus(ab + dtb_ref[...])
    beta = jax.nn.sigmoid(ab)
    lane = lax.broadcasted_iota(I32, ab.shape, 1)
    gb_ref[...] = jnp.where(lane < H_B, g, beta)


def _proj_ab(x, w_main, w_ab, alog_row, dtb_row):
    n = x.shape[0]
    row = lambda w: pl.BlockSpec((TM, w), lambda i: (i, 0))
    full = lambda a: pl.BlockSpec(a.shape, lambda i: (0,) * a.ndim)
    return pl.pallas_call(
        _proj_ab_body,
        grid=(n // TM,),
        in_specs=[row(D_MODEL), full(w_main), full(w_ab), full(alog_row), full(dtb_row)],
        out_specs=[row(D_A), row(D_A), row(D_QKV), row(D_A), row(LANES)],
        out_shape=[jax.ShapeDtypeStruct((n, D_A), F32), jax.ShapeDtypeStruct((n, D_A), F32),
                   jax.ShapeDtypeStruct((n, D_QKV), F32), jax.ShapeDtypeStruct((n, D_A), F32),
                   jax.ShapeDtypeStruct((n, LANES), F32)],
        compiler_params=_cparams(("arbitrary",)),
        name="proj_ab",
    )(x, w_main, w_ab, alog_row, dtb_row)


def _shift_rows(x, prev8, s):
    if s == 0:
        return x
    xr = pltpu.roll(x, s, axis=0)
    pr = pltpu.roll(prev8, s, axis=0)
    rid = lax.broadcasted_iota(I32, pr.shape, 0)
    head = jnp.where(rid < s, pr, xr[0:SUBLANES])
    if x.shape[0] == SUBLANES:
        return head
    return jnp.concatenate([head, xr[SUBLANES:]], axis=0)


def _causal_conv(x, prev8, w_ref, taps):
    y = x * w_ref[taps - 1:taps, :]
    for s in range(1, taps):
        y = y + _shift_rows(x, prev8, s) * w_ref[taps - 1 - s:taps - s, :]
    return y


def _lane_scan(x, pos, chunk, reverse):
    s = 1
    while s < chunk:
        if reverse:
            x = x + jnp.where(pos < chunk - s, pltpu.roll(x, LANES - s, axis=1), 0.0)
        else:
            x = x + jnp.where(pos >= s, pltpu.roll(x, s, axis=1), 0.0)
        s *= 2
    return x


def _per_row(row):
    return jnp.broadcast_to(row, (LANES, LANES)).T


def _gdn_body(chunk, bb, cps, n_steps,
              qkv_ref, u_ref, bg_ref, z_ref, gbr_ref, pq_ref, pu_ref, s0_ref, wq_ref, wa_ref, ng_ref, y_any,
              ycat_ref, nq_ref, nu_ref, sn_ref, cq_scr, cu_scr, s_scr):
    del y_any
    n = pl.program_id(1)

    @pl.when(n == 0)
    def _():
        cq_scr[...] = pq_ref[...]
        cu_scr[...] = pu_ref[...]
        s_scr[...] = s0_ref[...]

    gsz = LANES // chunk
    n_groups = bb * cps * H_B // gsz
    levels = int(math.log2(chunk))
    span = cps * chunk

    ii = lax.broadcasted_iota(I32, (LANES, LANES), 0)
    jj = lax.broadcasted_iota(I32, (LANES, LANES), 1)
    same = (ii // chunk) == (jj // chunk)
    m_incl = same & (ii >= jj)
    m_strict = same & (ii > jj)
    eye = (ii == jj).astype(F32)
    pos = lax.broadcasted_iota(I32, (SUBLANES, LANES), 1) % chunk

    qs, ks, vs = {}, {}, {}
    for b in range(bb):
        rows = slice(b * span, (b + 1) * span)
        x = qkv_ref[rows, :]
        qc = _causal_conv(x, cq_scr[b], wq_ref, CONV_B)
        qc = qc * jax.nn.sigmoid(qc)
        cq_scr[b] = x[span - SUBLANES:span]
        uu = u_ref[rows, :]
        ca = _causal_conv(uu, cu_scr[b], wa_ref, CONV_A)
        cu_scr[b] = uu[span - SUBLANES:span]
        ycat_ref[rows, 0:D_A] = (bg_ref[rows, :] * ca).astype(BF16)
        for j in range(cps):
            r = slice(j * chunk, (j + 1) * chunk)
            for h in range(H_B):
                qh = qc[r, h * DK:(h + 1) * DK]
                kh = qc[r, H_B * DK + h * DK:H_B * DK + (h + 1) * DK]
                vh = qc[r, 2 * H_B * DK + h * DV:2 * H_B * DK + (h + 1) * DV]
                key = (b * cps + j, h)
                qs[key] = qh * (lax.rsqrt(jnp.sum(qh * qh, axis=-1, keepdims=True) + RMS_EPS) * (DK ** -0.5))
                ks[key] = kh * lax.rsqrt(jnp.sum(kh * kh, axis=-1, keepdims=True) + RMS_EPS)
                vs[key] = vh

    for gi in range(n_groups):
        blocks = [divmod(gi * gsz + t, H_B) for t in range(gsz)]
        cat = lambda d: jnp.concatenate([d[uh] for uh in blocks], axis=0) if gsz > 1 else d[blocks[0]]
        qg, kg, vg = cat(qs), cat(ks), cat(vs)

        tile = gbr_ref[gi]
        gc = _lane_scan(tile, pos, chunk, False)
        rs = _lane_scan(tile, pos, chunk, True) - tile
        gc_row = gc[0:1]
        gc_m = _per_row(gc_row)
        rs_m = _per_row(rs[0:1])
        beta_m = _per_row(tile[1:2])
        diff = gc_m - jnp.broadcast_to(gc_row, (LANES, LANES))
        decay = jnp.where(m_incl, jnp.exp(jnp.where(m_incl, diff, 0.0)), 0.0)
        eg = jnp.exp(gc_m)
        etot = jnp.exp(gc_m + rs_m)

        kb = kg * beta_m
        kgb = kg.astype(BF16)
        a_mat = lax.dot_general(kb.astype(BF16), kgb, (((1,), (1,)), ((), ())), preferred_element_type=F32)
        lm = jnp.where(m_strict, a_mat * decay, 0.0)
        attn = lax.dot_general(qg.astype(BF16), kgb, (((1,), (1,)), ((), ())), preferred_element_type=F32) * decay

        p = eye - lm
        m = _bdot(lm, lm)
        for lvl in range(1, levels):
            p = p + _bdot(p, m)
            if lvl < levels - 1:
                m = _bdot(m, m)
        uw = _bdot(p, jnp.concatenate([vg * beta_m, kb * eg], axis=1))
        u_all, w_all = uw[:, :DV], uw[:, DV:]
        qe = qg * eg
        kdec = kg * jnp.exp(rs_m)

        vnew, qsv = [], []
        for t, (unit, h) in enumerate(blocks):
            b = unit // cps
            r = slice(t * chunk, (t + 1) * chunk)
            s_bf = s_scr[b, h].astype(BF16)
            lhs = jnp.concatenate([w_all[r], qe[r]], axis=0).astype(BF16)
            both = jnp.dot(lhs, s_bf, preferred_element_type=F32)
            vnew.append(u_all[r] - both[:chunk])
            qsv.append(both[chunk:])
        vnew_g = jnp.concatenate(vnew, axis=0) if gsz > 1 else vnew[0]
        qs_g = jnp.concatenate(qsv, axis=0) if gsz > 1 else qsv[0]
        o = qs_g + _bdot(attn, vnew_g)
        o = o * lax.rsqrt(jnp.mean(o * o, axis=-1, keepdims=True) + RMS_EPS) * ng_ref[...]

        for t, (unit, h) in enumerate(blocks):
            b = unit // cps
            r = slice(t * chunk, (t + 1) * chunk)
            rows = slice(unit * chunk, (unit + 1) * chunk)
            zz = z_ref[rows, h * DV:(h + 1) * DV]
            ycat_ref[rows, D_A + h * DV:D_A + (h + 1) * DV] = (o[r] * (zz * jax.nn.sigmoid(zz))).astype(BF16)
            upd = lax.dot_general(kdec[r].astype(BF16), vnew[t].astype(BF16), (((0,), (0,)), ((), ())),
                                  preferred_element_type=F32)
            scale = jnp.broadcast_to(etot[t * chunk:t * chunk + 1, :], (DK, DV))
            s_scr[b, h] = s_scr[b, h] * scale + upd

    @pl.when(n == n_steps - 1)
    def _():
        nq_ref[...] = cq_scr[...]
        nu_ref[...] = cu_scr[...]
        sn_ref[...] = s_scr[...]


def _gdn_phased_body(chunk, cps, n_steps,
                     qkv_ref, u_ref, bg_ref, z_ref, gbr_ref, pq_ref, pu_ref, s0_ref, wq_ref, wa_ref, ng_ref, y_any,
                     ycat_ref, nq_ref, nu_ref, sn_ref, cq_scr, cu_scr, s_scr,
                     kf, qf, vf, k16, kb16, q16, kdec16, att16, m16, c16, vk16, uw16,
                     dec, pm, qe, etot, dm, om, n16, bm):
    del y_any
    n = pl.program_id(1)

    @pl.when(n == 0)
    def _():
        cq_scr[...] = pq_ref[...]
        cu_scr[...] = pu_ref[...]
        s_scr[...] = s0_ref[...]

    gsz = LANES // chunk
    n_groups = cps * H_B // gsz
    levels = int(math.log2(chunk))
    span = cps * chunk
    groups = range(n_groups)
    blocks_of = lambda gi: [divmod(gi * gsz + t, H_B) for t in range(gsz)]
    rows_of = lambda t: slice(t * chunk, (t + 1) * chunk)

    ii = lax.broadcasted_iota(I32, (LANES, LANES), 0)
    jj = lax.broadcasted_iota(I32, (LANES, LANES), 1)
    same = (ii // chunk) == (jj // chunk)
    m_incl = same & (ii >= jj)
    m_strict = same & (ii > jj)
    eye = (ii == jj).astype(F32)
    pos = lax.broadcasted_iota(I32, (SUBLANES, LANES), 1) % chunk
    nt = (((1,), (1,)), ((), ()))
    tn = (((0,), (0,)), ((), ()))

    x = qkv_ref[...]
    qc = _causal_conv(x, cq_scr[0], wq_ref, CONV_B)
    qc = qc * jax.nn.sigmoid(qc)
    cq_scr[0] = x[span - SUBLANES:span]
    uu = u_ref[...]
    ca = _causal_conv(uu, cu_scr[0], wa_ref, CONV_A)
    cu_scr[0] = uu[span - SUBLANES:span]
    ycat_ref[:, 0:D_A] = (bg_ref[...] * ca).astype(BF16)
    for gi in groups:
        for t, (j, h) in enumerate(blocks_of(gi)):
            r = rows_of(j)
            qh = qc[r, h * DK:(h + 1) * DK]
            kh = qc[r, H_B * DK + h * DK:H_B * DK + (h + 1) * DK]
            qf[gi, rows_of(t), :] = qh * (lax.rsqrt(jnp.sum(qh * qh, axis=-1, keepdims=True) + RMS_EPS)
                                          * (DK ** -0.5))
            kf[gi, rows_of(t), :] = kh * lax.rsqrt(jnp.sum(kh * kh, axis=-1, keepdims=True) + RMS_EPS)
            vf[gi, rows_of(t), :] = qc[r, 2 * H_B * DK + h * DV:2 * H_B * DK + (h + 1) * DV]

    for gi in groups:
        tile = gbr_ref[gi]
        gc = _lane_scan(tile, pos, chunk, False)
        rs = _lane_scan(tile, pos, chunk, True) - tile
        gc_row = gc[0:1]
        gc_m = _per_row(gc_row)
        rs_m = _per_row(rs[0:1])
        beta_m = _per_row(tile[1:2])
        diff = gc_m - jnp.broadcast_to(gc_row, (LANES, LANES))
        dec[gi] = jnp.where(m_incl, jnp.exp(jnp.where(m_incl, diff, 0.0)), 0.0)
        eg = jnp.exp(gc_m)
        etot[gi] = jnp.exp(gc_m + rs_m)
        kg = kf[gi]
        kb = kg * beta_m
        k16[gi] = kg.astype(BF16)
        kb16[gi] = kb.astype(BF16)
        q16[gi] = qf[gi].astype(BF16)
        qe[gi] = qf[gi] * eg
        kdec16[gi] = (kg * jnp.exp(rs_m)).astype(BF16)
        vk16[gi, :, 0:DV] = (vf[gi] * beta_m).astype(BF16)
        vk16[gi, :, DV:] = (kb * eg).astype(BF16)

    for gi in groups:
        a_mat = lax.dot_general(kb16[gi], k16[gi], nt, preferred_element_type=F32)
        lm = jnp.where(m_strict, a_mat * dec[gi], 0.0)
        pm[gi] = eye - lm
        lm16 = lm.astype(BF16)
        m16[gi] = jnp.dot(lm16, lm16, preferred_element_type=F32).astype(BF16)
        att16[gi] = (lax.dot_general(q16[gi], k16[gi], nt, preferred_element_type=F32) * dec[gi]).astype(BF16)

    for lvl in range(1, levels):
        for gi in groups:
            pm[gi] = pm[gi] + jnp.dot(pm[gi].astype(BF16), m16[gi], preferred_element_type=F32)
        if lvl < levels - 1:
            for gi in groups:
                m16[gi] = jnp.dot(m16[gi], m16[gi], preferred_element_type=F32).astype(BF16)

    for gi in groups:
        uw16[gi] = jnp.dot(pm[gi].astype(BF16), vk16[gi], preferred_element_type=F32).astype(BF16)
    for gi in groups:
        au = jnp.dot(att16[gi], uw16[gi], preferred_element_type=F32)
        dm[gi] = au[:, :DV]
        c16[gi] = (qe[gi] - au[:, DV:]).astype(BF16)
        for t in range(gsz):
            r = rows_of(t)
            nb = lax.dot_general(kdec16[gi, r, :], uw16[gi, r, :], tn, preferred_element_type=F32)
            bm[gi * gsz + t] = nb[:, :DV]
            n16[gi * gsz + t] = nb[:, DV:].astype(BF16)

    for gi in groups:
        for t, (j, h) in enumerate(blocks_of(gi)):
            r = rows_of(t)
            s_old = s_scr[0, h]
            lhs = jnp.concatenate([c16[gi, r, :], n16[gi * gsz + t]], axis=0)
            both = jnp.dot(lhs, s_old.astype(BF16), preferred_element_type=F32)
            om[gi, r, :] = both[:chunk] + dm[gi, r, :]
            scale = jnp.broadcast_to(etot[gi, t * chunk:t * chunk + 1, :], (DK, DV))
            s_scr[0, h] = s_old * scale - both[chunk:] + bm[gi * gsz + t]

    for gi in groups:
        o = om[gi]
        o = o * lax.rsqrt(jnp.mean(o * o, axis=-1, keepdims=True) + RMS_EPS) * ng_ref[...]
        for t, (j, h) in enumerate(blocks_of(gi)):
            zz = z_ref[rows_of(j), h * DV:(h + 1) * DV]
            ycat_ref[rows_of(j), D_A + h * DV:D_A + (h + 1) * DV] = (
                o[rows_of(t)] * (zz * jax.nn.sigmoid(zz))).astype(BF16)

    @pl.when(n == n_steps - 1)
    def _():
        nq_ref[...] = cq_scr[...]
        nu_ref[...] = cu_scr[...]
        sn_ref[...] = s_scr[...]


def _gdn_phased_scratch(chunk, cps):
    g = cps * H_B * chunk // LANES
    nblk = cps * H_B
    mat = lambda dt, n=g, w=LANES: pltpu.VMEM((n, LANES, w), dt)
    return ([mat(F32)] * 3 + [mat(BF16)] * 7 + [mat(BF16, w=2 * LANES)] * 2 + [mat(F32)] * 6
            + [mat(BF16, n=nblk), mat(F32, n=nblk)])


def _gdn(qkv, u, bg, z, gbr, prev_q, prev_u, s0, wq, wa, ng, ycat, *, chunk, bb, cps, n_seq, t_len, row_block0,
         phased=False):
    rb = bb * cps * chunk
    gs = rb * H_B // LANES
    n_steps = t_len // (cps * chunk)
    rowmap = lambda i, n: (row_block0 + i * n_steps + n, 0)
    row = lambda w: pl.BlockSpec((rb, w), rowmap)
    seq3 = lambda w: pl.BlockSpec((bb, SUBLANES, w), lambda i, n: (i, 0, 0))
    full = lambda a: pl.BlockSpec(a.shape, lambda i, n: (0,) * a.ndim)
    st = pl.BlockSpec((bb, H_B, DK, DV), lambda i, n: (i, 0, 0, 0))
    if phased:
        assert bb == 1
        body = functools.partial(_gdn_phased_body, chunk, cps, n_steps)
        extra_scratch = _gdn_phased_scratch(chunk, cps)
    else:
        body = functools.partial(_gdn_body, chunk, bb, cps, n_steps)
        extra_scratch = []
    return pl.pallas_call(
        body,
        grid=(n_seq // bb, n_steps),
        in_specs=[row(D_QKV), row(D_A), row(D_A), row(D_A),
                  pl.BlockSpec((None, gs, SUBLANES, LANES), lambda i, n: (i * n_steps + n, 0, 0, 0)),
                  seq3(D_QKV), seq3(D_A), st, full(wq), full(wa), full(ng),
                  pl.BlockSpec(memory_space=pl.ANY)],
        out_specs=[pl.BlockSpec((rb, D_MODEL), rowmap), seq3(D_QKV), seq3(D_A), st],
        out_shape=[jax.ShapeDtypeStruct(ycat.shape, BF16),
                   jax.ShapeDtypeStruct((n_seq, SUBLANES, D_QKV), F32),
                   jax.ShapeDtypeStruct((n_seq, SUBLANES, D_A), F32),
                   jax.ShapeDtypeStruct((n_seq, H_B, DK, DV), F32)],
        scratch_shapes=[pltpu.VMEM((bb, SUBLANES, D_QKV), F32), pltpu.VMEM((bb, SUBLANES, D_A), F32),
                        pltpu.VMEM((bb, H_B, DK, DV), F32)] + extra_scratch,
        input_output_aliases={11: 0},
        compiler_params=_cparams(("arbitrary", "arbitrary")),
        name=f"gdn_c{chunk}",
    )(qkv, u, bg, z, gbr, prev_q, prev_u, s0, wq, wa, ng, ycat)


def _group_rows(gb, n_seq, t_len, chunk, bb, cps):
    n_steps = t_len // (cps * chunk)
    gs = bb * cps * chunk * H_B // LANES
    g = gb[:, :2 * H_B].reshape(n_seq // bb, bb, n_steps, cps, chunk, 2, H_B)
    g = jnp.transpose(g, (0, 2, 5, 1, 3, 6, 4))
    g = g.reshape(n_seq // bb * n_steps, 2, gs, LANES)
    g = jnp.transpose(g, (0, 2, 1, 3))
    return jnp.pad(g, ((0, 0), (0, 0), (0, SUBLANES - 2), (0, 0)))


def _route(x_new, wr_ref, br_ref, tri_ref, cnt_scr, gate_ref, dest_ref, counts_ref):
    @pl.when(pl.program_id(0) == 0)
    def _():
        cnt_scr[...] = jnp.zeros(cnt_scr.shape, F32)

    xh = x_new.astype(BF16)
    xl = (x_new - xh.astype(F32)).astype(BF16)
    p = jnp.dot(xh, wr_ref[...], preferred_element_type=F32)
    logits = (p[:, :LANES] + p[:, LANES:] + jnp.dot(xl, wr_ref[:, :LANES], preferred_element_type=F32)
              + br_ref[...])
    lane = lax.broadcasted_iota(I32, logits.shape, 1).astype(F32)
    neg = jnp.float32(-jnp.inf)
    l = jnp.where(lane < N_EXP, logits, neg)
    val_out = jnp.full(logits.shape, neg, F32)
    sels, hots = [], []
    for k in range(TOP_K):
        m = jnp.max(l, axis=-1, keepdims=True)
        sel = jnp.min(jnp.where(l == m, lane, float(LANES)), axis=-1, keepdims=True)
        hit = lane == sel
        val_out = jnp.where(lane == k, m, val_out)
        l = jnp.where(hit, neg, l)
        sels.append(sel)
        hots.append(hit.astype(F32))
    e = jnp.exp(val_out - jnp.max(val_out, axis=-1, keepdims=True))
    gate_ref[...] = e / jnp.sum(e, axis=-1, keepdims=True)

    hot = hots[0] + hots[1] + hots[2] + hots[3]
    before = jnp.dot(tri_ref[...], hot.astype(BF16), preferred_element_type=F32) + cnt_scr[...]
    dest = jnp.zeros(logits.shape, F32)
    for k in range(TOP_K):
        rank = jnp.sum(hots[k] * before, axis=-1, keepdims=True)
        dest = jnp.where(lane == k, sels[k] * float(1 << RANK_BITS) + rank, dest)
    dest_ref[...] = dest.astype(I32)
    cnt_scr[...] = cnt_scr[...] + jnp.sum(hot, axis=0, keepdims=True)
    counts_ref[...] = cnt_scr[...]


_ROUTE_OUT_SPECS = [pl.BlockSpec((TM, LANES), lambda i: (i, 0)), pl.BlockSpec((TM, LANES), lambda i: (i, 0)),
                    pl.BlockSpec((1, LANES), lambda i: (0, 0))]


def _route_out_shapes(n):
    return [jax.ShapeDtypeStruct((n, LANES), F32), jax.ShapeDtypeStruct((n, LANES), I32),
            jax.ShapeDtypeStruct((1, LANES), F32)]


def _mm_res_ln_body(y_ref, x_ref, w_ref, g_ref, b_ref, wr_ref, br_ref, tri_ref,
                    o_ref, gate_ref, dest_ref, counts_ref, cnt_scr):
    h = jnp.dot(y_ref[...].astype(BF16), w_ref[...], preferred_element_type=F32)
    xn = _layer_norm(DEEPNORM_ALPHA * x_ref[...] + h, g_ref[...], b_ref[...])
    o_ref[...] = xn
    _route(xn, wr_ref, br_ref, tri_ref, cnt_scr, gate_ref, dest_ref, counts_ref)


def _mm_res_ln(y, x, w, g, b, wr, br, tri):
    n = x.shape[0]
    row = pl.BlockSpec((TM, D_MODEL), lambda i: (i, 0))
    full = lambda a: pl.BlockSpec(a.shape, lambda i: (0,) * a.ndim)
    return pl.pallas_call(
        _mm_res_ln_body,
        grid=(n // TM,),
        in_specs=[row, row, full(w), full(g), full(b), full(wr), full(br), full(tri)],
        out_specs=[row] + _ROUTE_OUT_SPECS,
        out_shape=[jax.ShapeDtypeStruct((n, D_MODEL), F32)] + _route_out_shapes(n),
        scratch_shapes=[pltpu.VMEM((1, LANES), F32)],
        compiler_params=_cparams(("arbitrary",)),
        name="mm_res_ln",
    )(y, x, w, g, b, wr, br, tri)


def _gelu(x):
    return 0.5 * x * (1.0 + lax.erf(x * (2.0 ** -0.5)))


def _c_in_body(x_ref, w_ref, b_ref, lg_ref, lb_ref, u_ref, v_ref):
    xb = x_ref[...].astype(BF16)
    u_ref[...] = _gelu(jnp.dot(xb, w_ref[:, :D_C], preferred_element_type=F32) + b_ref[:, :D_C])
    v = _gelu(jnp.dot(xb, w_ref[:, D_C:], preferred_element_type=F32) + b_ref[:, D_C:])
    v_ref[...] = _layer_norm(v, lg_ref[...], lb_ref[...])


def _c_in(x, w, b, lg, lb):
    n = x.shape[0]
    row = pl.BlockSpec((TM, D_MODEL), lambda i: (i, 0))
    full = lambda a: pl.BlockSpec(a.shape, lambda i: (0,) * a.ndim)
    return pl.pallas_call(
        _c_in_body,
        grid=(n // TM,),
        in_specs=[row, full(w), full(b), full(lg), full(lb)],
        out_specs=[row, row],
        out_shape=[jax.ShapeDtypeStruct((n, D_C), F32), jax.ShapeDtypeStruct((n, D_C), F32)],
        compiler_params=_cparams(("arbitrary",)),
        name="c_in",
    )(x, w, b, lg, lb)


def _c_out_body(u_ref, v_ref, x_ref, wmix_ref, mask_ref, bias_ref, wout_ref, g_ref, b_ref, wr_ref, br_ref, tri_ref,
                o_ref, gate_ref, dest_ref, counts_ref, us_scr, cnt_scr):
    mix = [(wmix_ref[gi] * mask_ref[...]).astype(BF16) for gi in range(C_GROUPS)]
    for t in range(TM // C_CHUNK):
        rows = slice(t * C_CHUNK, (t + 1) * C_CHUNK)
        for gi in range(C_GROUPS):
            cols = slice(gi * LANES, (gi + 1) * LANES)
            s = jnp.dot(mix[gi], v_ref[rows, cols].astype(BF16), preferred_element_type=F32) + bias_ref[:, cols]
            us_scr[rows, cols] = (u_ref[rows, cols] * s).astype(BF16)
    h = jnp.dot(us_scr[...], wout_ref[...], preferred_element_type=F32)
    xn = _layer_norm(DEEPNORM_ALPHA * x_ref[...] + h, g_ref[...], b_ref[...])
    o_ref[...] = xn
    _route(xn, wr_ref, br_ref, tri_ref, cnt_scr, gate_ref, dest_ref, counts_ref)


def _c_out(u, v, x, wmix2, mask2, bias2, wout, g, b, wr, br, tri):
    n = x.shape[0]
    first_sample_step = N_PROMPT // TM
    sel = lambda i: jnp.where(i >= first_sample_step, 1, 0)
    row = pl.BlockSpec((TM, D_MODEL), lambda i: (i, 0))
    full = lambda a: pl.BlockSpec(a.shape, lambda i: (0,) * a.ndim)
    return pl.pallas_call(
        _c_out_body,
        grid=(n // TM,),
        in_specs=[row, row, row,
                  pl.BlockSpec((None, C_GROUPS, C_CHUNK, C_CHUNK), lambda i: (sel(i), 0, 0, 0)),
                  pl.BlockSpec((None, C_CHUNK, C_CHUNK), lambda i: (sel(i), 0, 0)),
                  pl.BlockSpec((None, C_CHUNK, D_C), lambda i: (sel(i), 0, 0)),
                  full(wout), full(g), full(b), full(wr), full(br), full(tri)],
        out_specs=[row] + _ROUTE_OUT_SPECS,
        out_shape=[jax.ShapeDtypeStruct((n, D_MODEL), F32)] + _route_out_shapes(n),
        scratch_shapes=[pltpu.VMEM((TM, D_C), BF16), pltpu.VMEM((1, LANES), F32)],
        compiler_params=_cparams(("arbitrary",)),
        name="c_out",
    )(u, v, x, wmix2, mask2, bias2, wout, g, b, wr, br, tri)


ROW_TILE = D_MODEL // LANES
assert ROW_TILE == SUBLANES


def _to_tiles(dst_ref, row0, x):
    t = x.shape[0]
    for c in range(ROW_TILE):
        dst_ref[pl.ds(row0 * ROW_TILE + c, t, stride=ROW_TILE), :] = x[:, c * LANES:(c + 1) * LANES]


def _from_tiles(src_ref, row0, t):
    return jnp.concatenate([src_ref[pl.ds(row0 * ROW_TILE + c, t, stride=ROW_TILE), :] for c in range(ROW_TILE)],
                           axis=1)


def _tiles(ref, row, n_rows):
    return ref.at[pl.ds(pl.multiple_of(row * ROW_TILE, ROW_TILE), n_rows * ROW_TILE)]


def _dispatch_body(pad_start_ref, pad_n_ref, n_used_ref, dest_ref, x_ref, rows_hbm, xt, zbuf, sem, zsem):
    i = pl.program_id(0)

    @pl.when(i == 0)
    def _():
        zbuf[...] = jnp.zeros(zbuf.shape, F32)

    @pl.when(i < N_EXP)
    def _():
        n = pad_n_ref[i]
        off = pad_start_ref[i]
        copies = []
        for size in [1 << p for p in reversed(range(int(math.log2(TM_E))))]:
            copies.append((pltpu.make_async_copy(_tiles(zbuf, 0, size), _tiles(rows_hbm, off, size), zsem),
                           (n & size) != 0))
            off = off + (n & size)
        for cp, used in copies:
            pl.when(used)(cp.start)
        for cp, used in copies:
            pl.when(used)(cp.wait)

    blk = n_used_ref[0] + (i - N_EXP)

    @pl.when(jnp.logical_and(i >= N_EXP, blk < N_EBLOCKS))
    def _():
        cp = pltpu.make_async_copy(zbuf, _tiles(rows_hbm, blk * TM_E, TM_E), zsem)
        cp.start()
        cp.wait()

    _to_tiles(xt, 0, x_ref[...])

    def push(t, c):
        src = _tiles(xt, t, 1)
        for k in range(TOP_K):
            pltpu.make_async_copy(src, _tiles(rows_hbm, dest_ref[0, 0, t * TOP_K + k], 1), sem).start(priority=k % 2)
        return c
    lax.fori_loop(0, TM_D, push, 0, unroll=8)

    def drain(t, c):
        for k in range(TOP_K):
            pltpu.make_async_copy(_tiles(xt, 0, 1), _tiles(rows_hbm, 0, 1), sem).wait()
        return c
    lax.fori_loop(0, TM_D, drain, 0, unroll=8)


def _dispatch(pads, dest3, x):
    n = x.shape[0]
    assert n // TM_D >= N_EXP + (N_EBLOCKS - N_PAIRS // TM_E)
    return pl.pallas_call(
        _dispatch_body,
        grid_spec=pltpu.PrefetchScalarGridSpec(
            num_scalar_prefetch=3,
            grid=(n // TM_D,),
            in_specs=[pl.BlockSpec((1, 1, TM_D * TOP_K), lambda i, *_: (i, 0, 0), memory_space=pltpu.SMEM),
                      pl.BlockSpec((TM_D, D_MODEL), lambda i, *_: (i, 0))],
            out_specs=pl.BlockSpec(memory_space=pl.ANY),
            scratch_shapes=[pltpu.VMEM((TM_D * ROW_TILE, LANES), F32), pltpu.VMEM((TM_E * ROW_TILE, LANES), F32),
                            pltpu.SemaphoreType.DMA(()), pltpu.SemaphoreType.DMA(())]),
        out_shape=jax.ShapeDtypeStruct((ROWS_TOTAL * ROW_TILE, LANES), F32),
        compiler_params=_cparams(("arbitrary",)),
        name="moe_dispatch",
    )(*pads, dest3, x)


def _experts_body(bexp_ref, bval_ref, x_ref, w1_ref, b1_ref, w2_ref, b2_ref, o_ref, w1b, w2b):
    i = pl.program_id(0)
    valid = bval_ref[i] != 0
    fresh = jnp.logical_or(i == 0, bexp_ref[i] != bexp_ref[jnp.maximum(i - 1, 0)])

    @pl.when(jnp.logical_and(valid, fresh))
    def _():
        def cast_rows(r, c):
            rows = pl.ds(pl.multiple_of(r * 128, 128), 128)
            w1b[rows, :] = w1_ref[rows, :].astype(BF16)
            w2b[rows, :] = w2_ref[rows, :].astype(BF16)
            return c
        lax.fori_loop(0, D_MODEL // 128, cast_rows, 0)

    @pl.when(jnp.logical_not(valid))
    def _():
        o_ref[...] = jnp.zeros(o_ref.shape, F32)

    @pl.when(valid)
    def _():
        half = TM_E // 2
        for r0 in (0, half):
            xb = _from_tiles(x_ref, r0, half).astype(BF16)
            glu = jnp.dot(xb, w1b[:, :D_EXP], preferred_element_type=F32) + b1_ref[:, :D_EXP]
            lin = jnp.dot(xb, w1b[:, D_EXP:], preferred_element_type=F32) + b1_ref[:, D_EXP:]
            glu = jnp.minimum(glu, SWIGLU_LIMIT)
            lin = jnp.clip(lin, -SWIGLU_LIMIT, SWIGLU_LIMIT)
            act = glu * jax.nn.sigmoid(SWIGLU_ALPHA * glu) * (lin + 1.0)
            _to_tiles(o_ref, r0, jnp.dot(act.astype(BF16), w2b[...], preferred_element_type=F32) + b2_ref[...])


def _experts(layer, tables, x_rows, w1, b1, w2, b2):
    wspec = lambda shape: pl.BlockSpec((None, None) + shape, lambda i, be, bv: (layer, be[i], 0, 0))
    rows = pl.BlockSpec((TM_E * ROW_TILE, LANES), lambda i, be, bv: (i, 0))
    return pl.pallas_call(
        _experts_body,
        grid_spec=pltpu.PrefetchScalarGridSpec(
            num_scalar_prefetch=2,
            grid=(N_EBLOCKS,),
            in_specs=[rows, wspec((D_MODEL, 2 * D_EXP)), wspec((1, 2 * D_EXP)),
                      wspec((D_EXP, D_MODEL)), wspec((1, D_MODEL))],
            out_specs=rows,
            scratch_shapes=[pltpu.VMEM((D_MODEL, 2 * D_EXP), BF16), pltpu.VMEM((D_EXP, D_MODEL), BF16)]),
        out_shape=jax.ShapeDtypeStruct((ROWS_TOTAL * ROW_TILE, LANES), F32),
        compiler_params=_cparams(("arbitrary",)),
        name="moe_experts",
    )(*tables, x_rows, w1, b1, w2, b2)


def _combine_ln_body(split, dest_ref, gate_ref, x_ref, rows_hbm, g_ref, b_ref, *refs):
    *o_refs, buf, sem = refs

    def pull(t, c):
        for k in range(TOP_K):
            src = _tiles(rows_hbm, dest_ref[0, 0, t * TOP_K + k], 1)
            pltpu.make_async_copy(src, _tiles(buf.at[k], t, 1), sem).start(priority=k % 2)
        return c
    lax.fori_loop(0, TM_C, pull, 0, unroll=8)

    def drain(t, c):
        for k in range(TOP_K):
            pltpu.make_async_copy(_tiles(rows_hbm, 0, 1), _tiles(buf.at[0], 0, 1), sem).wait()
        return c
    lax.fori_loop(0, TM_C, drain, 0, unroll=8)

    gates = gate_ref[...]
    y = _from_tiles(buf.at[0], 0, TM_C) * gates[:, 0:1]
    for k in range(1, TOP_K):
        y = y + _from_tiles(buf.at[k], 0, TM_C) * gates[:, k:k + 1]
    res = _layer_norm(DEEPNORM_ALPHA * x_ref[...] + y, g_ref[...], b_ref[...])
    if not split:
        o_refs[0][...] = res
    else:
        is_prompt = pl.program_id(0) < N_PROMPT // TM_C

        @pl.when(is_prompt)
        def _():
            o_refs[0][...] = res

        @pl.when(jnp.logical_not(is_prompt))
        def _():
            o_refs[1][...] = res


def _combine_ln(dest3, gates, x, out_rows, g, b, split=False):
    n = x.shape[0]
    full = lambda a: pl.BlockSpec(a.shape, lambda i: (0,) * a.ndim)
    row = pl.BlockSpec((TM_C, D_MODEL), lambda i: (i, 0))
    if split:
        per_seq = SEQ // TM_C
        last = N_PROMPT // TM_C - 1
        out_specs = [pl.BlockSpec((None, TM_C, D_MODEL),
                                  lambda i: (jnp.minimum(i, last) // per_seq, jnp.minimum(i, last) % per_seq, 0)),
                     pl.BlockSpec((TM_C, D_MODEL), lambda i: (jnp.maximum(i - last - 1, 0), 0))]
        out_shape = [jax.ShapeDtypeStruct((BATCH, SEQ, D_MODEL), F32),
                     jax.ShapeDtypeStruct((N_SAMPLE, D_MODEL), F32)]
    else:
        out_specs, out_shape = row, jax.ShapeDtypeStruct((n, D_MODEL), F32)
    return pl.pallas_call(
        functools.partial(_combine_ln_body, split),
        grid=(n // TM_C,),
        in_specs=[pl.BlockSpec((1, 1, TM_C * TOP_K), lambda i: (i, 0, 0), memory_space=pltpu.SMEM),
                  pl.BlockSpec((TM_C, LANES), lambda i: (i, 0)), row,
                  pl.BlockSpec(memory_space=pl.ANY), full(g), full(b)],
        out_specs=out_specs,
        out_shape=out_shape,
        scratch_shapes=[pltpu.VMEM((TOP_K, TM_C * ROW_TILE, LANES), F32), pltpu.SemaphoreType.DMA(())],
        compiler_params=_cparams(("arbitrary",)),
        name="moe_combine_ln",
    )(dest3, gates, x, out_rows, g, b)


def _positions(enc, counts):
    nb = (counts + TM_E - 1) // TM_E
    cum = jnp.cumsum(nb)
    first_blk = cum - nb
    experts = jnp.arange(N_EXP, dtype=I32)
    pair_hot = ((enc >> RANK_BITS)[:, :, None] == experts[None, None, :]).astype(I32)
    dest = jnp.sum(pair_hot * (first_blk * TM_E)[None, None, :], axis=2) + (enc & ((1 << RANK_BITS) - 1))
    n_used = cum[-1]
    blk = jnp.arange(N_EBLOCKS, dtype=I32)
    exp = jnp.minimum(jnp.sum((cum[None, :] <= blk[:, None]).astype(I32), axis=1), N_EXP - 1)
    pads = (first_blk * TM_E + counts, nb * TM_E - counts, n_used.reshape(1))
    return dest, (exp, (blk < n_used).astype(I32)), pads


def _moe_post_norm(layer, x, gates, enc, counts, w1, b1, w2, b2, g, b, split=False):
    dest4, tables, pads = _positions(enc[:, :TOP_K], counts[0, :N_EXP].astype(I32))
    x_rows = _dispatch(pads, dest4.reshape(N_TOK // TM_D, 1, TM_D * TOP_K), x)
    out_rows = _experts(layer, tables, x_rows, w1, b1.reshape(b1.shape[:2] + (1, 2 * D_EXP)),
                        w2, b2.reshape(b2.shape[:2] + (1, D_MODEL)))
    return _combine_ln(dest4.reshape(N_TOK // TM_C, 1, TM_C * TOP_K), gates, x, out_rows,
                       g.reshape(1, D_MODEL), b.reshape(1, D_MODEL), split=split)


def _router_weights(w_r, b_r):
    wh = w_r.astype(BF16)
    wl = (w_r - wh.astype(F32)).astype(BF16)
    pad = lambda a: jnp.pad(a, ((0, 0), (0, LANES - N_EXP)))
    return jnp.concatenate([pad(wh), pad(wl)], axis=1), jnp.pad(b_r, (0, LANES - N_EXP)).reshape(1, LANES)


def _tail8(state, keep):
    return jnp.pad(state, ((0, 0), (SUBLANES - keep, 0), (0, 0)))


def kernel(x_prompt, x_sample, state_conv_a, state_conv_qkv, state_delta, ab_w_in, ab_conv_a, ab_conv_qkv,
           ab_a_log, ab_dt_bias, ab_norm_g, ab_w_out, c_w_in, c_b_in, c_ln_g, c_ln_b, c_w_s, c_b_s, c_w_out,
           moe_w_router, moe_b_router, moe_w1, moe_b1, moe_w2, moe_b2, ln_g, ln_b):
    x = jnp.concatenate([x_prompt.reshape(N_PROMPT, D_MODEL), x_sample.reshape(N_SAMPLE, D_MODEL)], axis=0)
    lnrow = lambda layer, j: (ln_g[layer, j].reshape(1, D_MODEL), ln_b[layer, j].reshape(1, D_MODEL))
    ri = jnp.arange(TM)
    tri = (ri[:, None] > ri[None, :]).astype(BF16)

    w_in = ab_w_in[0]
    w_main = w_in[:, :W_MAIN].astype(BF16)
    w_ab = jnp.pad(w_in[:, W_MAIN:], ((0, 0), (0, LANES - 2 * H_B))).astype(BF16)
    alog_row = jnp.pad(ab_a_log[0], (0, LANES - H_B)).reshape(1, LANES)
    dtb_row = jnp.pad(ab_dt_bias[0], (0, LANES - H_B)).reshape(1, LANES)
    bg, u, qkv, z, gb = _proj_ab(x, w_main, w_ab, alog_row, dtb_row)

    ng = ab_norm_g[0].reshape(1, DV)
    ycat = jnp.zeros((N_TOK, D_MODEL), BF16)
    gbr_p = _group_rows(gb[:N_PROMPT], BATCH, SEQ, DN_CHUNK, 1, GDN_CPS)
    ycat, pq8, pu8, p_delta = _gdn(
        qkv, u, bg, z, gbr_p,
        jnp.zeros((BATCH, SUBLANES, D_QKV), F32), jnp.zeros((BATCH, SUBLANES, D_A), F32),
        jnp.zeros((BATCH, H_B, DK, DV), F32), ab_conv_qkv[0], ab_conv_a[0], ng, ycat,
        chunk=DN_CHUNK, bb=1, cps=GDN_CPS, n_seq=BATCH, t_len=SEQ, row_block0=0, phased=True)
    bb_s = 16
    gbr_s = _group_rows(gb[N_PROMPT:], DEC_BATCH, DEC_SEQ, DEC_SEQ, bb_s, 1)
    ycat, sq8, su8, s_delta = _gdn(
        qkv, u, bg, z, gbr_s,
        _tail8(state_conv_qkv[0], CONV_B - 1), _tail8(state_conv_a[0], CONV_A - 1), state_delta[0],
        ab_conv_qkv[0], ab_conv_a[0], ng, ycat,
        chunk=DEC_SEQ, bb=bb_s, cps=1, n_seq=DEC_BATCH, t_len=DEC_SEQ, row_block0=N_PROMPT // (bb_s * DEC_SEQ))
    wr0, br0 = _router_weights(moe_w_router[0], moe_b_router[0])
    x, gates, dest, counts = _mm_res_ln(ycat, x, ab_w_out[0].astype(BF16), *lnrow(0, 0), wr0, br0, tri)
    x = _moe_post_norm(0, x, gates, dest, counts, moe_w1, moe_b1, moe_w2, moe_b2, ln_g[0, 1], ln_b[0, 1])

    uc, vc = _c_in(x, c_w_in[0].astype(BF16), c_b_in[0].reshape(1, 2 * D_C),
                   c_ln_g[0].reshape(1, D_C), c_ln_b[0].reshape(1, D_C))
    ws = c_w_s[0]
    reps = C_CHUNK // DEC_SEQ
    wmix2 = jnp.stack([ws, jnp.tile(ws[:, :DEC_SEQ, :DEC_SEQ], (1, reps, reps))])
    rc = jnp.arange(C_CHUNK)
    tril = rc[:, None] >= rc[None, :]
    mask2 = jnp.stack([tril, tril & ((rc[:, None] // DEC_SEQ) == (rc[None, :] // DEC_SEQ))]).astype(F32)
    bias_p = jnp.repeat(c_b_s[0].T, D_C // C_GROUPS, axis=1)
    bias2 = jnp.stack([bias_p, jnp.tile(bias_p[:DEC_SEQ], (reps, 1))])
    wr1, br1 = _router_weights(moe_w_router[1], moe_b_router[1])
    x, gates, dest, counts = _c_out(uc, vc, x, wmix2, mask2, bias2, c_w_out[0].astype(BF16), *lnrow(1, 0),
                                    wr1, br1, tri)
    y_prompt, y_sample = _moe_post_norm(1, x, gates, dest, counts, moe_w1, moe_b1, moe_w2, moe_b2,
                                        ln_g[1, 1], ln_b[1, 1], split=True)
    y_sample = y_sample.reshape(DEC_BATCH, DEC_SEQ, D_MODEL)
    ka, kq = CONV_A - 1, CONV_B - 1
    return (y_prompt, y_sample,
            pu8[None, :, SUBLANES - ka:], pq8[None, :, SUBLANES - kq:], p_delta[None],
            su8[None, :, SUBLANES - ka:], sq8[None, :, SUBLANES - kq:], s_delta[None],
            vc[N_PROMPT:].reshape(1, DEC_BATCH, DEC_SEQ, D_C))
```

```python
import functools
import math

import jax
import jax.numpy as jnp
from jax import lax
from jax.experimental import pallas as pl
from jax.experimental.pallas import tpu as pltpu

F32 = jnp.float32
BF16 = jnp.bfloat16
I32 = jnp.int32

D_MODEL = 1024
BATCH = 8
SEQ = 2048
DEC_BATCH = 128
DEC_SEQ = 8
N_PROMPT = BATCH * SEQ
N_SAMPLE = DEC_BATCH * DEC_SEQ
N_TOK = N_PROMPT + N_SAMPLE
D_A = 512
CONV_A = 3
H_B = 4
DK = 128
DV = 128
D_QKV = 1536
CONV_B = 4
DN_CHUNK = 64
W_MAIN = 3 * D_A + D_QKV + H_B * DV
D_C = 1024
C_GROUPS = 8
C_CHUNK = 128
N_EXP = 32
TOP_K = 4
D_EXP = 1024
SWIGLU_ALPHA = 1.702
SWIGLU_LIMIT = 7.0
DEEPNORM_ALPHA = 4.0 ** 0.25
LN_EPS = 1e-5
RMS_EPS = 1e-6

LANES = 128
SUBLANES = 8
VMEM_LIMIT = 56 * 1024 * 1024

TM = 512
TM_E = 512
TM_C = 256
TM_D = 256
GDN_CPS = 4
N_PAIRS = N_TOK * TOP_K
N_EBLOCKS = (N_PAIRS + N_EXP * (TM_E - 1)) // TM_E
ROWS_TOTAL = N_EBLOCKS * TM_E
RANK_BITS = 15
assert N_TOK <= 1 << RANK_BITS


def _cparams(sem):
    return pltpu.CompilerParams(dimension_semantics=sem, vmem_limit_bytes=VMEM_LIMIT)


def _layer_norm(t, g, b):
    mu = jnp.mean(t, axis=-1, keepdims=True)
    d = t - mu
    var = jnp.mean(d * d, axis=-1, keepdims=True)
    return d * lax.rsqrt(var + LN_EPS) * g + b


def _bdot(a, b):
    return jnp.dot(a.astype(BF16), b.astype(BF16), preferred_element_type=F32)


def _token_tile(xp_ref, xs_ref):
    return jnp.where(pl.program_id(0) < N_PROMPT // TM, xp_ref[...], xs_ref[...])


def _token_specs():
    n_p = N_PROMPT // TM
    return [pl.BlockSpec((TM, D_MODEL), lambda i: (jnp.minimum(i, n_p - 1), 0)),
            pl.BlockSpec((TM, D_MODEL), lambda i: (jnp.maximum(i - n_p, 0), 0))]


def _proj_ab_body(xp_ref, xs_ref, w_ref, wab_ref, alog_ref, dtb_ref, bg_ref, u_ref, qkv_ref, z_ref, gb_ref):
    xb = _token_tile(xp_ref, xs_ref).astype(BF16)

    def mm(lo, hi):
        return jnp.dot(xb, w_ref[:, lo:hi], preferred_element_type=F32)

    bg_ref[...] = mm(0, D_A)
    u_ref[...] = mm(D_A, 2 * D_A) * mm(2 * D_A, 3 * D_A)
    for c in range(D_QKV // 512):
        qkv_ref[:, c * 512:(c + 1) * 512] = mm(3 * D_A + c * 512, 3 * D_A + (c + 1) * 512)
    z_ref[...] = mm(3 * D_A + D_QKV, W_MAIN)
    ab = jnp.dot(xb, wab_ref[...], preferred_element_type=F32)
    g = -jnp.exp(alog_ref[...]) * jax.nn.softplus(ab + dtb_ref[...])
    beta = jax.nn.sigmoid(ab)
    lane = lax.broadcasted_iota(I32, ab.shape, 1)
    gb_ref[...] = jnp.where(lane < H_B, g, beta)


def _proj_ab(xp, xs, w_main, w_ab, alog_row, dtb_row):
    n = xp.shape[0] + xs.shape[0]
    row = lambda w: pl.BlockSpec((TM, w), lambda i: (i, 0))
    full = lambda a: pl.BlockSpec(a.shape, lambda i: (0,) * a.ndim)
    return pl.pallas_call(
        _proj_ab_body,
        grid=(n // TM,),
        in_specs=_token_specs() + [full(w_main), full(w_ab), full(alog_row), full(dtb_row)],
        out_specs=[row(D_A), row(D_A), row(D_QKV), row(D_A), row(LANES)],
        out_shape=[jax.ShapeDtypeStruct((n, D_A), F32), jax.ShapeDtypeStruct((n, D_A), F32),
                   jax.ShapeDtypeStruct((n, D_QKV), F32), jax.ShapeDtypeStruct((n, D_A), F32),
                   jax.ShapeDtypeStruct((n, LANES), F32)],
        compiler_params=_cparams(("arbitrary",)),
        name="proj_ab",
    )(xp, xs, w_main, w_ab, alog_row, dtb_row)


def _shift_rows(x, prev8, s):
    if s == 0:
        return x
    xr = pltpu.roll(x, s, axis=0)
    pr = pltpu.roll(prev8, s, axis=0)
    rid = lax.broadcasted_iota(I32, pr.shape, 0)
    head = jnp.where(rid < s, pr, xr[0:SUBLANES])
    if x.shape[0] == SUBLANES:
        return head
    return jnp.concatenate([head, xr[SUBLANES:]], axis=0)


def _causal_conv(x, prev8, w_ref, taps):
    y = x * w_ref[taps - 1:taps, :]
    for s in range(1, taps):
        y = y + _shift_rows(x, prev8, s) * w_ref[taps - 1 - s:taps - s, :]
    return y


def _lane_scan(x, pos, chunk, reverse):
    s = 1
    while s < chunk:
        if reverse:
            x = x + jnp.where(pos < chunk - s, pltpu.roll(x, LANES - s, axis=1), 0.0)
        else:
            x = x + jnp.where(pos >= s, pltpu.roll(x, s, axis=1), 0.0)
        s *= 2
    return x


def _per_row(row):
    return jnp.broadcast_to(row, (LANES, LANES)).T


def _gdn_body(chunk, bb, cps, n_steps,
              qkv_ref, u_ref, bg_ref, z_ref, gbr_ref, pq_ref, pu_ref, s0_ref, wq_ref, wa_ref, ng_ref, y_any,
              ycat_ref, nq_ref, nu_ref, sn_ref, cq_scr, cu_scr, s_scr):
    del y_any
    n = pl.program_id(1)

    @pl.when(n == 0)
    def _():
        cq_scr[...] = pq_ref[...]
        cu_scr[...] = pu_ref[...]
        s_scr[...] = s0_ref[...]

    gsz = LANES // chunk
    n_groups = bb * cps * H_B // gsz
    levels = int(math.log2(chunk))
    span = cps * chunk

    ii = lax.broadcasted_iota(I32, (LANES, LANES), 0)
    jj = lax.broadcasted_iota(I32, (LANES, LANES), 1)
    same = (ii // chunk) == (jj // chunk)
    m_incl = same & (ii >= jj)
    m_strict = same & (ii > jj)
    eye = (ii == jj).astype(F32)
    pos = lax.broadcasted_iota(I32, (SUBLANES, LANES), 1) % chunk

    qs, ks, vs = {}, {}, {}
    for b in range(bb):
        rows = slice(b * span, (b + 1) * span)
        x = qkv_ref[rows, :]
        qc = _causal_conv(x, cq_scr[b], wq_ref, CONV_B)
        qc = qc * jax.nn.sigmoid(qc)
        cq_scr[b] = x[span - SUBLANES:span]
        uu = u_ref[rows, :]
        ca = _causal_conv(uu, cu_scr[b], wa_ref, CONV_A)
        cu_scr[b] = uu[span - SUBLANES:span]
        ycat_ref[rows, 0:D_A] = (bg_ref[rows, :] * ca).astype(BF16)
        for j in range(cps):
            r = slice(j * chunk, (j + 1) * chunk)
            for h in range(H_B):
                qh = qc[r, h * DK:(h + 1) * DK]
                kh = qc[r, H_B * DK + h * DK:H_B * DK + (h + 1) * DK]
                vh = qc[r, 2 * H_B * DK + h * DV:2 * H_B * DK + (h + 1) * DV]
                key = (b * cps + j, h)
                qs[key] = qh * (lax.rsqrt(jnp.sum(qh * qh, axis=-1, keepdims=True) + RMS_EPS) * (DK ** -0.5))
                ks[key] = kh * lax.rsqrt(jnp.sum(kh * kh, axis=-1, keepdims=True) + RMS_EPS)
                vs[key] = vh

    for gi in range(n_groups):
        blocks = [divmod(gi * gsz + t, H_B) for t in range(gsz)]
        cat = lambda d: jnp.concatenate([d[uh] for uh in blocks], axis=0) if gsz > 1 else d[blocks[0]]
        qg, kg, vg = cat(qs), cat(ks), cat(vs)

        tile = gbr_ref[gi]
        gc = _lane_scan(tile, pos, chunk, False)
        rs = _lane_scan(tile, pos, chunk, True) - tile
        gc_row = gc[0:1]
        gc_m = _per_row(gc_row)
        rs_m = _per_row(rs[0:1])
        beta_m = _per_row(tile[1:2])
        diff = gc_m - jnp.broadcast_to(gc_row, (LANES, LANES))
        decay = jnp.where(m_incl, jnp.exp(jnp.where(m_incl, diff, 0.0)), 0.0)
        eg = jnp.exp(gc_m)
        etot = jnp.exp(gc_m + rs_m)

        kb = kg * beta_m
        kgb = kg.astype(BF16)
        a_mat = lax.dot_general(kb.astype(BF16), kgb, (((1,), (1,)), ((), ())), preferred_element_type=F32)
        lm = jnp.where(m_strict, a_mat * decay, 0.0)
        attn = lax.dot_general(qg.astype(BF16), kgb, (((1,), (1,)), ((), ())), preferred_element_type=F32) * decay

        p = eye - lm
        m = _bdot(lm, lm)
        for lvl in range(1, levels):
            p = p + _bdot(p, m)
            if lvl < levels - 1:
                m = _bdot(m, m)
        uw = _bdot(p, jnp.concatenate([vg * beta_m, kb * eg], axis=1))
        u_all, w_all = uw[:, :DV], uw[:, DV:]
        qe = qg * eg
        kdec = kg * jnp.exp(rs_m)

        vnew, qsv = [], []
        for t, (unit, h) in enumerate(blocks):
            b = unit // cps
            r = slice(t * chunk, (t + 1) * chunk)
            s_bf = s_scr[b, h].astype(BF16)
            lhs = jnp.concatenate([w_all[r], qe[r]], axis=0).astype(BF16)
            both = jnp.dot(lhs, s_bf, preferred_element_type=F32)
            vnew.append(u_all[r] - both[:chunk])
            qsv.append(both[chunk:])
        vnew_g = jnp.concatenate(vnew, axis=0) if gsz > 1 else vnew[0]
        qs_g = jnp.concatenate(qsv, axis=0) if gsz > 1 else qsv[0]
        o = qs_g + _bdot(attn, vnew_g)
        o = o * lax.rsqrt(jnp.mean(o * o, axis=-1, keepdims=True) + RMS_EPS) * ng_ref[...]

        for t, (unit, h) in enumerate(blocks):
            b = unit // cps
            r = slice(t * chunk, (t + 1) * chunk)
            rows = slice(unit * chunk, (unit + 1) * chunk)
            zz = z_ref[rows, h * DV:(h + 1) * DV]
            ycat_ref[rows, D_A + h * DV:D_A + (h + 1) * DV] = (o[r] * (zz * jax.nn.sigmoid(zz))).astype(BF16)
            upd = lax.dot_general(kdec[r].astype(BF16), vnew[t].astype(BF16), (((0,), (0,)), ((), ())),
                                  preferred_element_type=F32)
            scale = jnp.broadcast_to(etot[t * chunk:t * chunk + 1, :], (DK, DV))
            s_scr[b, h] = s_scr[b, h] * scale + upd

    @pl.when(n == n_steps - 1)
    def _():
        nq_ref[...] = cq_scr[...]
        nu_ref[...] = cu_scr[...]
        sn_ref[...] = s_scr[...]


def _gdn_phased_body(chunk, cps, n_steps,
                     qkv_ref, u_ref, bg_ref, z_ref, gbr_ref, pq_ref, pu_ref, s0_ref, wq_ref, wa_ref, ng_ref, y_any,
                     ycat_ref, nq_ref, nu_ref, sn_ref, cq_scr, cu_scr, s_scr,
                     kf, qf, vf, k16, kb16, q16, kdec16, att16, m16, c16, vk16, uw16,
                     dec, pm, qe, etot, dm, om, n16, bm):
    del y_any
    n = pl.program_id(1)

    @pl.when(n == 0)
    def _():
        cq_scr[...] = pq_ref[...]
        cu_scr[...] = pu_ref[...]
        s_scr[...] = s0_ref[...]

    gsz = LANES // chunk
    n_groups = cps * H_B // gsz
    levels = int(math.log2(chunk))
    span = cps * chunk
    groups = range(n_groups)
    blocks_of = lambda gi: [divmod(gi * gsz + t, H_B) for t in range(gsz)]
    rows_of = lambda t: slice(t * chunk, (t + 1) * chunk)

    ii = lax.broadcasted_iota(I32, (LANES, LANES), 0)
    jj = lax.broadcasted_iota(I32, (LANES, LANES), 1)
    same = (ii // chunk) == (jj // chunk)
    m_incl = same & (ii >= jj)
    m_strict = same & (ii > jj)
    eye = (ii == jj).astype(F32)
    pos = lax.broadcasted_iota(I32, (SUBLANES, LANES), 1) % chunk
    nt = (((1,), (1,)), ((), ()))
    tn = (((0,), (0,)), ((), ()))

    x = qkv_ref[...]
    qc = _causal_conv(x, cq_scr[0], wq_ref, CONV_B)
    qc = qc * jax.nn.sigmoid(qc)
    cq_scr[0] = x[span - SUBLANES:span]
    uu = u_ref[...]
    ca = _causal_conv(uu, cu_scr[0], wa_ref, CONV_A)
    cu_scr[0] = uu[span - SUBLANES:span]
    ycat_ref[:, 0:D_A] = (bg_ref[...] * ca).astype(BF16)
    for gi in groups:
        for t, (j, h) in enumerate(blocks_of(gi)):
            r = rows_of(j)
            qh = qc[r, h * DK:(h + 1) * DK]
            kh = qc[r, H_B * DK + h * DK:H_B * DK + (h + 1) * DK]
            qf[gi, rows_of(t), :] = qh * (lax.rsqrt(jnp.sum(qh * qh, axis=-1, keepdims=True) + RMS_EPS)
                                          * (DK ** -0.5))
            kf[gi, rows_of(t), :] = kh * lax.rsqrt(jnp.sum(kh * kh, axis=-1, keepdims=True) + RMS_EPS)
            vf[gi, rows_of(t), :] = qc[r, 2 * H_B * DK + h * DV:2 * H_B * DK + (h + 1) * DV]

    for gi in groups:
        tile = gbr_ref[gi]
        gc = _lane_scan(tile, pos, chunk, False)
        rs = _lane_scan(tile, pos, chunk, True) - tile
        gc_row = gc[0:1]
        gc_m = _per_row(gc_row)
        rs_m = _per_row(rs[0:1])
        beta_m = _per_row(tile[1:2])
        diff = gc_m - jnp.broadcast_to(gc_row, (LANES, LANES))
        dec[gi] = jnp.where(m_incl, jnp.exp(jnp.where(m_incl, diff, 0.0)), 0.0)
        eg = jnp.exp(gc_m)
        etot[gi] = jnp.exp(gc_m + rs_m)
        kg = kf[gi]
        kb = kg * beta_m
        k16[gi] = kg.astype(BF16)
        kb16[gi] = kb.astype(BF16)
        q16[gi] = qf[gi].astype(BF16)
        qe[gi] = qf[gi] * eg
        kdec16[gi] = (kg * jnp.exp(rs_m)).astype(BF16)
        vk16[gi, :, 0:DV] = (vf[gi] * beta_m).astype(BF16)
        vk16[gi, :, DV:] = (kb * eg).astype(BF16)

    for gi in groups:
        a_mat = lax.dot_general(kb16[gi], k16[gi], nt, preferred_element_type=F32)
        lm = jnp.where(m_strict, a_mat * dec[gi], 0.0)
        pm[gi] = eye - lm
        lm16 = lm.astype(BF16)
        m16[gi] = jnp.dot(lm16, lm16, preferred_element_type=F32).astype(BF16)
        att16[gi] = (lax.dot_general(q16[gi], k16[gi], nt, preferred_element_type=F32) * dec[gi]).astype(BF16)

    for lvl in range(1, levels):
        for gi in groups:
            pm[gi] = pm[gi] + jnp.dot(pm[gi].astype(BF16), m16[gi], preferred_element_type=F32)
        if lvl < levels - 1:
            for gi in groups:
                m16[gi] = jnp.dot(m16[gi], m16[gi], preferred_element_type=F32).astype(BF16)

    for gi in groups:
        uw16[gi] = jnp.dot(pm[gi].astype(BF16), vk16[gi], preferred_element_type=F32).astype(BF16)
    for gi in groups:
        au = jnp.dot(att16[gi], uw16[gi], preferred_element_type=F32)
        dm[gi] = au[:, :DV]
        c16[gi] = (qe[gi] - au[:, DV:]).astype(BF16)
        for t in range(gsz):
            r = rows_of(t)
            nb = lax.dot_general(kdec16[gi, r, :], uw16[gi, r, :], tn, preferred_element_type=F32)
            bm[gi * gsz + t] = nb[:, :DV]
            n16[gi * gsz + t] = nb[:, DV:].astype(BF16)

    for gi in groups:
        for t, (j, h) in enumerate(blocks_of(gi)):
            r = rows_of(t)
            s_old = s_scr[0, h]
            lhs = jnp.concatenate([c16[gi, r, :], n16[gi * gsz + t]], axis=0)
            both = jnp.dot(lhs, s_old.astype(BF16), preferred_element_type=F32)
            om[gi, r, :] = both[:chunk] + dm[gi, r, :]
            scale = jnp.broadcast_to(etot[gi, t * chunk:t * chunk + 1, :], (DK, DV))
            s_scr[0, h] = s_old * scale - both[chunk:] + bm[gi * gsz + t]

    for gi in groups:
        o = om[gi]
        o = o * lax.rsqrt(jnp.mean(o * o, axis=-1, keepdims=True) + RMS_EPS) * ng_ref[...]
        for t, (j, h) in enumerate(blocks_of(gi)):
            zz = z_ref[rows_of(j), h * DV:(h + 1) * DV]
            ycat_ref[rows_of(j), D_A + h * DV:D_A + (h + 1) * DV] = (
                o[rows_of(t)] * (zz * jax.nn.sigmoid(zz))).astype(BF16)

    @pl.when(n == n_steps - 1)
    def _():
        nq_ref[...] = cq_scr[...]
        nu_ref[...] = cu_scr[...]
        sn_ref[...] = s_scr[...]


def _gdn_phased_scratch(chunk, cps):
    g = cps * H_B * chunk // LANES
    nblk = cps * H_B
    mat = lambda dt, n=g, w=LANES: pltpu.VMEM((n, LANES, w), dt)
    return ([mat(F32)] * 3 + [mat(BF16)] * 7 + [mat(BF16, w=2 * LANES)] * 2 + [mat(F32)] * 6
            + [mat(BF16, n=nblk), mat(F32, n=nblk)])


def _gdn(qkv, u, bg, z, gbr, prev_q, prev_u, s0, wq, wa, ng, ycat, *, chunk, bb, cps, n_seq, t_len, row_block0,
         phased=False):
    rb = bb * cps * chunk
    gs = rb * H_B // LANES
    n_steps = t_len // (cps * chunk)
    rowmap = lambda i, n: (row_block0 + i * n_steps + n, 0)
    row = lambda w: pl.BlockSpec((rb, w), rowmap)
    seq3 = lambda w: pl.BlockSpec((bb, SUBLANES, w), lambda i, n: (i, 0, 0))
    full = lambda a: pl.BlockSpec(a.shape, lambda i, n: (0,) * a.ndim)
    st = pl.BlockSpec((bb, H_B, DK, DV), lambda i, n: (i, 0, 0, 0))
    if phased:
        assert bb == 1
        body = functools.partial(_gdn_phased_body, chunk, cps, n_steps)
        extra_scratch = _gdn_phased_scratch(chunk, cps)
    else:
        body = functools.partial(_gdn_body, chunk, bb, cps, n_steps)
        extra_scratch = []
    return pl.pallas_call(
        body,
        grid=(n_seq // bb, n_steps),
        in_specs=[row(D_QKV), row(D_A), row(D_A), row(D_A),
                  pl.BlockSpec((None, gs, SUBLANES, LANES), lambda i, n: (i * n_steps + n, 0, 0, 0)),
                  seq3(D_QKV), seq3(D_A), st, full(wq), full(wa), full(ng),
                  pl.BlockSpec(memory_space=pl.ANY)],
        out_specs=[pl.BlockSpec((rb, D_MODEL), rowmap), seq3(D_QKV), seq3(D_A), st],
        out_shape=[jax.ShapeDtypeStruct(ycat.shape, BF16),
                   jax.ShapeDtypeStruct((n_seq, SUBLANES, D_QKV), F32),
                   jax.ShapeDtypeStruct((n_seq, SUBLANES, D_A), F32),
                   jax.ShapeDtypeStruct((n_seq, H_B, DK, DV), F32)],
        scratch_shapes=[pltpu.VMEM((bb, SUBLANES, D_QKV), F32), pltpu.VMEM((bb, SUBLANES, D_A), F32),
                        pltpu.VMEM((bb, H_B, DK, DV), F32)] + extra_scratch,
        input_output_aliases={11: 0},
        compiler_params=_cparams(("arbitrary", "arbitrary")),
        name=f"gdn_c{chunk}",
    )(qkv, u, bg, z, gbr, prev_q, prev_u, s0, wq, wa, ng, ycat)


def _group_rows(gb, n_seq, t_len, chunk, bb, cps):
    n_steps = t_len // (cps * chunk)
    gs = bb * cps * chunk * H_B // LANES
    g = gb[:, :2 * H_B].reshape(n_seq // bb, bb, n_steps, cps, chunk, 2, H_B)
    g = jnp.transpose(g, (0, 2, 5, 1, 3, 6, 4))
    g = g.reshape(n_seq // bb * n_steps, 2, gs, LANES)
    g = jnp.transpose(g, (0, 2, 1, 3))
    return jnp.pad(g, ((0, 0), (0, 0), (0, SUBLANES - 2), (0, 0)))


def _route(x_new, wr_ref, br_ref, tri_ref, cnt_scr, gate_ref, dest_ref, counts_ref):
    @pl.when(pl.program_id(0) == 0)
    def _():
        cnt_scr[...] = jnp.zeros(cnt_scr.shape, F32)

    xh = x_new.astype(BF16)
    xl = (x_new - xh.astype(F32)).astype(BF16)
    p = jnp.dot(xh, wr_ref[...], preferred_element_type=F32)
    logits = (p[:, :LANES] + p[:, LANES:] + jnp.dot(xl, wr_ref[:, :LANES], preferred_element_type=F32)
              + br_ref[...])
    lane = lax.broadcasted_iota(I32, logits.shape, 1).astype(F32)
    neg = jnp.float32(-jnp.inf)
    l = jnp.where(lane < N_EXP, logits, neg)
    val_out = jnp.full(logits.shape, neg, F32)
    sels, hots = [], []
    for k in range(TOP_K):
        m = jnp.max(l, axis=-1, keepdims=True)
        sel = jnp.min(jnp.where(l == m, lane, float(LANES)), axis=-1, keepdims=True)
        hit = lane == sel
        val_out = jnp.where(lane == k, m, val_out)
        l = jnp.where(hit, neg, l)
        sels.append(sel)
        hots.append(hit.astype(F32))
    e = jnp.exp(val_out - jnp.max(val_out, axis=-1, keepdims=True))
    gate_ref[...] = e / jnp.sum(e, axis=-1, keepdims=True)

    hot = hots[0] + hots[1] + hots[2] + hots[3]
    before = jnp.dot(tri_ref[...], hot.astype(BF16), preferred_element_type=F32) + cnt_scr[...]
    dest = jnp.zeros(logits.shape, F32)
    for k in range(TOP_K):
        rank = jnp.sum(hots[k] * before, axis=-1, keepdims=True)
        dest = jnp.where(lane == k, sels[k] * float(1 << RANK_BITS) + rank, dest)
    dest_ref[...] = dest.astype(I32)
    cnt_scr[...] = cnt_scr[...] + jnp.sum(hot, axis=0, keepdims=True)
    counts_ref[...] = cnt_scr[...]


_ROUTE_OUT_SPECS = [pl.BlockSpec((TM, LANES), lambda i: (i, 0)), pl.BlockSpec((TM, LANES), lambda i: (i, 0)),
                    pl.BlockSpec((1, LANES), lambda i: (0, 0))]


def _route_out_shapes(n):
    return [jax.ShapeDtypeStruct((n, LANES), F32), jax.ShapeDtypeStruct((n, LANES), I32),
            jax.ShapeDtypeStruct((1, LANES), F32)]


def _mm_res_ln_body(y_ref, xp_ref, xs_ref, w_ref, g_ref, b_ref, wr_ref, br_ref, tri_ref,
                    o_ref, gate_ref, dest_ref, counts_ref, cnt_scr):
    h = jnp.dot(y_ref[...].astype(BF16), w_ref[...], preferred_element_type=F32)
    xn = _layer_norm(DEEPNORM_ALPHA * _token_tile(xp_ref, xs_ref) + h, g_ref[...], b_ref[...])
    o_ref[...] = xn
    _route(xn, wr_ref, br_ref, tri_ref, cnt_scr, gate_ref, dest_ref, counts_ref)


def _mm_res_ln(y, xp, xs, w, g, b, wr, br, tri):
    n = xp.shape[0] + xs.shape[0]
    row = pl.BlockSpec((TM, D_MODEL), lambda i: (i, 0))
    full = lambda a: pl.BlockSpec(a.shape, lambda i: (0,) * a.ndim)
    return pl.pallas_call(
        _mm_res_ln_body,
        grid=(n // TM,),
        in_specs=[row] + _token_specs() + [full(w), full(g), full(b), full(wr), full(br), full(tri)],
        out_specs=[row] + _ROUTE_OUT_SPECS,
        out_shape=[jax.ShapeDtypeStruct((n, D_MODEL), F32)] + _route_out_shapes(n),
        scratch_shapes=[pltpu.VMEM((1, LANES), F32)],
        compiler_params=_cparams(("arbitrary",)),
        name="mm_res_ln",
    )(y, xp, xs, w, g, b, wr, br, tri)


def _gelu(x):
    return 0.5 * x * (1.0 + lax.erf(x * (2.0 ** -0.5)))


def _c_in_body(x_ref, w_ref, b_ref, lg_ref, lb_ref, u_ref, v_ref):
    xb = x_ref[...].astype(BF16)
    u_ref[...] = _gelu(jnp.dot(xb, w_ref[:, :D_C], preferred_element_type=F32) + b_ref[:, :D_C])
    v = _gelu(jnp.dot(xb, w_ref[:, D_C:], preferred_element_type=F32) + b_ref[:, D_C:])
    v_ref[...] = _layer_norm(v, lg_ref[...], lb_ref[...])


def _c_in(x, w, b, lg, lb):
    n = x.shape[0]
    row = pl.BlockSpec((TM, D_MODEL), lambda i: (i, 0))
    full = lambda a: pl.BlockSpec(a.shape, lambda i: (0,) * a.ndim)
    return pl.pallas_call(
        _c_in_body,
        grid=(n // TM,),
        in_specs=[row, full(w), full(b), full(lg), full(lb)],
        out_specs=[row, row],
        out_shape=[jax.ShapeDtypeStruct((n, D_C), F32), jax.ShapeDtypeStruct((n, D_C), F32)],
        compiler_params=_cparams(("arbitrary",)),
        name="c_in",
    )(x, w, b, lg, lb)


def _c_out_body(u_ref, v_ref, x_ref, wmix_ref, mask_ref, bias_ref, wout_ref, g_ref, b_ref, wr_ref, br_ref, tri_ref,
                o_ref, gate_ref, dest_ref, counts_ref, us_scr, cnt_scr):
    mix = [(wmix_ref[gi] * mask_ref[...]).astype(BF16) for gi in range(C_GROUPS)]
    for t in range(TM // C_CHUNK):
        rows = slice(t * C_CHUNK, (t + 1) * C_CHUNK)
        for gi in range(C_GROUPS):
            cols = slice(gi * LANES, (gi + 1) * LANES)
            s = jnp.dot(mix[gi], v_ref[rows, cols].astype(BF16), preferred_element_type=F32) + bias_ref[:, cols]
            us_scr[rows, cols] = (u_ref[rows, cols] * s).astype(BF16)
    h = jnp.dot(us_scr[...], wout_ref[...], preferred_element_type=F32)
    xn = _layer_norm(DEEPNORM_ALPHA * x_ref[...] + h, g_ref[...], b_ref[...])
    o_ref[...] = xn
    _route(xn, wr_ref, br_ref, tri_ref, cnt_scr, gate_ref, dest_ref, counts_ref)


def _c_out(u, v, x, wmix2, mask2, bias2, wout, g, b, wr, br, tri):
    n = x.shape[0]
    first_sample_step = N_PROMPT // TM
    sel = lambda i: jnp.where(i >= first_sample_step, 1, 0)
    row = pl.BlockSpec((TM, D_MODEL), lambda i: (i, 0))
    full = lambda a: pl.BlockSpec(a.shape, lambda i: (0,) * a.ndim)
    return pl.pallas_call(
        _c_out_body,
        grid=(n // TM,),
        in_specs=[row, row, row,
                  pl.BlockSpec((None, C_GROUPS, C_CHUNK, C_CHUNK), lambda i: (sel(i), 0, 0, 0)),
                  pl.BlockSpec((None, C_CHUNK, C_CHUNK), lambda i: (sel(i), 0, 0)),
                  pl.BlockSpec((None, C_CHUNK, D_C), lambda i: (sel(i), 0, 0)),
                  full(wout), full(g), full(b), full(wr), full(br), full(tri)],
        out_specs=[row] + _ROUTE_OUT_SPECS,
        out_shape=[jax.ShapeDtypeStruct((n, D_MODEL), F32)] + _route_out_shapes(n),
        scratch_shapes=[pltpu.VMEM((TM, D_C), BF16), pltpu.VMEM((1, LANES), F32)],
        compiler_params=_cparams(("arbitrary",)),
        name="c_out",
    )(u, v, x, wmix2, mask2, bias2, wout, g, b, wr, br, tri)


def _for_each_row(n_rows, fn):
    def group(t8, c):
        base = pl.multiple_of(t8 * SUBLANES, SUBLANES)
        for s in range(SUBLANES):
            fn(t8, base, s)
        return c
    lax.fori_loop(0, n_rows // SUBLANES, group, 0)


def _dispatch_body(pad_start_ref, pad_n_ref, n_used_ref, dest_ref, x_ref, rows_hbm, zbuf, sem, zsem):
    i = pl.program_id(0)

    @pl.when(i == 0)
    def _():
        zbuf[...] = jnp.zeros(zbuf.shape, F32)

    @pl.when(i < N_EXP)
    def _():
        n = pad_n_ref[i]
        start = pad_start_ref[i]
        odd = n & (SUBLANES - 1)
        copies = [(pltpu.make_async_copy(zbuf.at[pl.ds(0, 1)], rows_hbm.at[pl.ds(start + s, 1)], zsem), s < odd)
                  for s in range(SUBLANES - 1)]
        off = start + odd
        for size in [1 << p for p in reversed(range(3, int(math.log2(TM_E))))]:
            dst = rows_hbm.at[pl.ds(pl.multiple_of(off, SUBLANES), size)]
            copies.append((pltpu.make_async_copy(zbuf.at[pl.ds(0, size)], dst, zsem), (n & size) != 0))
            off = off + (n & size)
        for cp, used in copies:
            pl.when(used)(cp.start)
        for cp, used in copies:
            pl.when(used)(cp.wait)

    blk = n_used_ref[0] + (i - N_EXP)

    @pl.when(jnp.logical_and(i >= N_EXP, blk < N_EBLOCKS))
    def _():
        cp = pltpu.make_async_copy(zbuf, rows_hbm.at[pl.ds(pl.multiple_of(blk * TM_E, TM_E), TM_E)], zsem)
        cp.start()
        cp.wait()

    def push(t8, base, s):
        src = x_ref.at[pl.ds(base, SUBLANES)].at[pl.ds(s, 1)]
        for k in range(TOP_K):
            d = dest_ref[0, 0, (t8 * SUBLANES + s) * TOP_K + k]
            pltpu.make_async_copy(src, rows_hbm.at[pl.ds(d, 1)], sem).start(priority=k % 2)
    _for_each_row(TM_D, push)

    def drain(t8, base, s):
        for k in range(TOP_K):
            pltpu.make_async_copy(x_ref.at[pl.ds(0, 1)], rows_hbm.at[pl.ds(0, 1)], sem).wait()
    _for_each_row(TM_D, drain)


def _dispatch(pads, dest3, x):
    n = x.shape[0]
    assert n // TM_D >= N_EXP + (N_EBLOCKS - N_PAIRS // TM_E)
    return pl.pallas_call(
        _dispatch_body,
        grid_spec=pltpu.PrefetchScalarGridSpec(
            num_scalar_prefetch=3,
            grid=(n // TM_D,),
            in_specs=[pl.BlockSpec((1, 1, TM_D * TOP_K), lambda i, *_: (i, 0, 0), memory_space=pltpu.SMEM),
                      pl.BlockSpec((TM_D, D_MODEL), lambda i, *_: (i, 0))],
            out_specs=pl.BlockSpec(memory_space=pl.ANY),
            scratch_shapes=[pltpu.VMEM((TM_E, D_MODEL), F32), pltpu.SemaphoreType.DMA(()),
                            pltpu.SemaphoreType.DMA(())]),
        out_shape=jax.ShapeDtypeStruct((ROWS_TOTAL, D_MODEL), F32),
        compiler_params=_cparams(("arbitrary",)),
        name="moe_dispatch",
    )(*pads, dest3, x)


def _experts_body(layer, bexp_ref, bval_ref, bnext_ref, x_ref, w1_hbm, b1_ref, w2_hbm, b2_ref, o_ref,
                  w1s, w2s, w1b, w2b, sem):
    i = pl.program_id(0)
    valid = bval_ref[i] != 0
    fresh = jnp.logical_or(i == 0, bexp_ref[i] != bexp_ref[jnp.maximum(i - 1, 0)])

    def weight_copies(e):
        return (pltpu.make_async_copy(w1_hbm.at[layer, e], w1s, sem.at[0]),
                pltpu.make_async_copy(w2_hbm.at[layer, e], w2s, sem.at[1]))

    @pl.when(jnp.logical_and(valid, fresh))
    def _():
        @pl.when(i == 0)
        def _():
            for cp in weight_copies(bexp_ref[i]):
                cp.start()
        for cp in weight_copies(bexp_ref[i]):
            cp.wait()

        def cast_rows(r, c):
            rows = pl.ds(pl.multiple_of(r * 128, 128), 128)
            w1b[rows, :] = w1s[rows, :].astype(BF16)
            w2b[rows, :] = w2s[rows, :].astype(BF16)
            return c
        lax.fori_loop(0, D_MODEL // 128, cast_rows, 0)

        @pl.when(bnext_ref[i] >= 0)
        def _():
            for cp in weight_copies(bnext_ref[i]):
                cp.start()

    @pl.when(jnp.logical_not(valid))
    def _():
        o_ref[...] = jnp.zeros(o_ref.shape, F32)

    @pl.when(valid)
    def _():
        xb = x_ref[...].astype(BF16)
        glu = jnp.dot(xb, w1b[:, :D_EXP], preferred_element_type=F32) + b1_ref[:, :D_EXP]
        lin = jnp.dot(xb, w1b[:, D_EXP:], preferred_element_type=F32) + b1_ref[:, D_EXP:]
        glu = jnp.minimum(glu, SWIGLU_LIMIT)
        lin = jnp.clip(lin, -SWIGLU_LIMIT, SWIGLU_LIMIT)
        act = glu * jax.nn.sigmoid(SWIGLU_ALPHA * glu) * (lin + 1.0)
        o_ref[...] = jnp.dot(act.astype(BF16), w2b[...], preferred_element_type=F32) + b2_ref[...]


def _experts(layer, tables, x_rows, w1, b1, w2, b2):
    bspec = lambda w: pl.BlockSpec((None, None, 1, w), lambda i, be, bv, bn: (layer, be[i], 0, 0))
    rows = pl.BlockSpec((TM_E, D_MODEL), lambda i, be, bv, bn: (i, 0))
    hbm = pl.BlockSpec(memory_space=pl.ANY)
    return pl.pallas_call(
        functools.partial(_experts_body, layer),
        grid_spec=pltpu.PrefetchScalarGridSpec(
            num_scalar_prefetch=3,
            grid=(N_EBLOCKS,),
            in_specs=[rows, hbm, bspec(2 * D_EXP), hbm, bspec(D_MODEL)],
            out_specs=rows,
            scratch_shapes=[pltpu.VMEM((D_MODEL, 2 * D_EXP), F32), pltpu.VMEM((D_EXP, D_MODEL), F32),
                            pltpu.VMEM((D_MODEL, 2 * D_EXP), BF16), pltpu.VMEM((D_EXP, D_MODEL), BF16),
                            pltpu.SemaphoreType.DMA((2,))]),
        out_shape=jax.ShapeDtypeStruct((ROWS_TOTAL, D_MODEL), F32),
        compiler_params=_cparams(("arbitrary",)),
        name="moe_experts",
    )(*tables, x_rows, w1, b1, w2, b2)


def _combine_ln_body(split, dest_ref, gate_ref, x_ref, rows_hbm, g_ref, b_ref, *refs):
    *o_refs, buf, sem = refs

    def pull(t8, base, s):
        for k in range(TOP_K):
            d = dest_ref[0, 0, (t8 * SUBLANES + s) * TOP_K + k]
            dst = buf.at[k].at[pl.ds(base, SUBLANES)].at[pl.ds(s, 1)]
            pltpu.make_async_copy(rows_hbm.at[pl.ds(d, 1)], dst, sem).start(priority=k % 2)
    _for_each_row(TM_C, pull)

    def drain(t8, base, s):
        for k in range(TOP_K):
            pltpu.make_async_copy(rows_hbm.at[pl.ds(0, 1)], buf.at[0].at[pl.ds(0, 1)], sem).wait()
    _for_each_row(TM_C, drain)

    gates = gate_ref[...]
    y = buf[0] * gates[:, 0:1]
    for k in range(1, TOP_K):
        y = y + buf[k] * gates[:, k:k + 1]
    res = _layer_norm(DEEPNORM_ALPHA * x_ref[...] + y, g_ref[...], b_ref[...])
    if not split:
        o_refs[0][...] = res
    else:
        is_prompt = pl.program_id(0) < N_PROMPT // TM_C

        @pl.when(is_prompt)
        def _():
            o_refs[0][...] = res

        @pl.when(jnp.logical_not(is_prompt))
        def _():
            o_refs[1][...] = res


def _combine_ln(dest3, gates, x, out_rows, g, b, split=False):
    n = x.shape[0]
    full = lambda a: pl.BlockSpec(a.shape, lambda i: (0,) * a.ndim)
    row = pl.BlockSpec((TM_C, D_MODEL), lambda i: (i, 0))
    if split:
        per_seq = SEQ // TM_C
        last = N_PROMPT // TM_C - 1
        out_specs = [pl.BlockSpec((None, TM_C, D_MODEL),
                                  lambda i: (jnp.minimum(i, last) // per_seq, jnp.minimum(i, last) % per_seq, 0)),
                     pl.BlockSpec((TM_C, D_MODEL), lambda i: (jnp.maximum(i - last - 1, 0), 0))]
        out_shape = [jax.ShapeDtypeStruct((BATCH, SEQ, D_MODEL), F32),
                     jax.ShapeDtypeStruct((N_SAMPLE, D_MODEL), F32)]
    else:
        out_specs, out_shape = row, jax.ShapeDtypeStruct((n, D_MODEL), F32)
    return pl.pallas_call(
        functools.partial(_combine_ln_body, split),
        grid=(n // TM_C,),
        in_specs=[pl.BlockSpec((1, 1, TM_C * TOP_K), lambda i: (i, 0, 0), memory_space=pltpu.SMEM),
                  pl.BlockSpec((TM_C, LANES), lambda i: (i, 0)), row,
                  pl.BlockSpec(memory_space=pl.ANY), full(g), full(b)],
        out_specs=out_specs,
        out_shape=out_shape,
        scratch_shapes=[pltpu.VMEM((TOP_K, TM_C, D_MODEL), F32), pltpu.SemaphoreType.DMA(())],
        compiler_params=_cparams(("arbitrary",)),
        name="moe_combine_ln",
    )(dest3, gates, x, out_rows, g, b)


def _positions(enc, counts):
    nb = (counts + TM_E - 1) // TM_E
    cum = jnp.cumsum(nb)
    first_blk = cum - nb
    experts = jnp.arange(N_EXP, dtype=I32)
    pair_hot = ((enc >> RANK_BITS)[:, :, None] == experts[None, None, :]).astype(I32)
    dest = jnp.sum(pair_hot * (first_blk * TM_E)[None, None, :], axis=2) + (enc & ((1 << RANK_BITS) - 1))
    n_used = cum[-1]
    blk = jnp.arange(N_EBLOCKS, dtype=I32)
    exp = jnp.minimum(jnp.sum((cum[None, :] <= blk[:, None]).astype(I32), axis=1), N_EXP - 1)
    later = lax.cummin(jnp.where(counts > 0, experts, N_EXP)[::-1])[::-1]
    nxt = jnp.concatenate([later[1:], jnp.full((1,), N_EXP, I32)])
    nxt = jnp.where(nxt >= N_EXP, -1, nxt)
    bnext = jnp.sum((exp[:, None] == experts[None, :]).astype(I32) * nxt[None, :], axis=1)
    pads = (first_blk * TM_E + counts, nb * TM_E - counts, n_used.reshape(1))
    return dest, (exp, (blk < n_used).astype(I32), bnext.astype(I32)), pads


def _moe_post_norm(layer, x, gates, enc, counts, w1, b1, w2, b2, g, b, split=False):
    dest4, tables, pads = _positions(enc[:, :TOP_K], counts[0, :N_EXP].astype(I32))
    x_rows = _dispatch(pads, dest4.reshape(N_TOK // TM_D, 1, TM_D * TOP_K), x)
    out_rows = _experts(layer, tables, x_rows, w1, b1.reshape(b1.shape[:2] + (1, 2 * D_EXP)),
                        w2, b2.reshape(b2.shape[:2] + (1, D_MODEL)))
    return _combine_ln(dest4.reshape(N_TOK // TM_C, 1, TM_C * TOP_K), gates, x, out_rows,
                       g.reshape(1, D_MODEL), b.reshape(1, D_MODEL), split=split)


def _router_weights(w_r, b_r):
    wh = w_r.astype(BF16)
    wl = (w_r - wh.astype(F32)).astype(BF16)
    pad = lambda a: jnp.pad(a, ((0, 0), (0, LANES - N_EXP)))
    return jnp.concatenate([pad(wh), pad(wl)], axis=1), jnp.pad(b_r, (0, LANES - N_EXP)).reshape(1, LANES)


def _tail8(state, keep):
    return jnp.pad(state, ((0, 0), (SUBLANES - keep, 0), (0, 0)))


def kernel(x_prompt, x_sample, state_conv_a, state_conv_qkv, state_delta, ab_w_in, ab_conv_a, ab_conv_qkv,
           ab_a_log, ab_dt_bias, ab_norm_g, ab_w_out, c_w_in, c_b_in, c_ln_g, c_ln_b, c_w_s, c_b_s, c_w_out,
           moe_w_router, moe_b_router, moe_w1, moe_b1, moe_w2, moe_b2, ln_g, ln_b):
    xp, xs = x_prompt.reshape(N_PROMPT, D_MODEL), x_sample.reshape(N_SAMPLE, D_MODEL)
    lnrow = lambda layer, j: (ln_g[layer, j].reshape(1, D_MODEL), ln_b[layer, j].reshape(1, D_MODEL))
    ri = jnp.arange(TM)
    tri = (ri[:, None] > ri[None, :]).astype(BF16)

    w_in = ab_w_in[0]
    w_main = w_in[:, :W_MAIN].astype(BF16)
    w_ab = jnp.pad(w_in[:, W_MAIN:], ((0, 0), (0, LANES - 2 * H_B))).astype(BF16)
    alog_row = jnp.pad(ab_a_log[0], (0, LANES - H_B)).reshape(1, LANES)
    dtb_row = jnp.pad(ab_dt_bias[0], (0, LANES - H_B)).reshape(1, LANES)
    bg, u, qkv, z, gb = _proj_ab(xp, xs, w_main, w_ab, alog_row, dtb_row)

    ng = ab_norm_g[0].reshape(1, DV)
    ycat = jnp.zeros((N_TOK, D_MODEL), BF16)
    gbr_p = _group_rows(gb[:N_PROMPT], BATCH, SEQ, DN_CHUNK, 1, GDN_CPS)
    ycat, pq8, pu8, p_delta = _gdn(
        qkv, u, bg, z, gbr_p,
        jnp.zeros((BATCH, SUBLANES, D_QKV), F32), jnp.zeros((BATCH, SUBLANES, D_A), F32),
        jnp.zeros((BATCH, H_B, DK, DV), F32), ab_conv_qkv[0], ab_conv_a[0], ng, ycat,
        chunk=DN_CHUNK, bb=1, cps=GDN_CPS, n_seq=BATCH, t_len=SEQ, row_block0=0, phased=True)
    bb_s = 16
    gbr_s = _group_rows(gb[N_PROMPT:], DEC_BATCH, DEC_SEQ, DEC_SEQ, bb_s, 1)
    ycat, sq8, su8, s_delta = _gdn(
        qkv, u, bg, z, gbr_s,
        _tail8(state_conv_qkv[0], CONV_B - 1), _tail8(state_conv_a[0], CONV_A - 1), state_delta[0],
        ab_conv_qkv[0], ab_conv_a[0], ng, ycat,
        chunk=DEC_SEQ, bb=bb_s, cps=1, n_seq=DEC_BATCH, t_len=DEC_SEQ, row_block0=N_PROMPT // (bb_s * DEC_SEQ))
    wr0, br0 = _router_weights(moe_w_router[0], moe_b_router[0])
    x, gates, dest, counts = _mm_res_ln(ycat, xp, xs, ab_w_out[0].astype(BF16), *lnrow(0, 0), wr0, br0, tri)
    x = _moe_post_norm(0, x, gates, dest, counts, moe_w1, moe_b1, moe_w2, moe_b2, ln_g[0, 1], ln_b[0, 1])

    uc, vc = _c_in(x, c_w_in[0].astype(BF16), c_b_in[0].reshape(1, 2 * D_C),
                   c_ln_g[0].reshape(1, D_C), c_ln_b[0].reshape(1, D_C))
    ws = c_w_s[0]
    reps = C_CHUNK // DEC_SEQ
    wmix2 = jnp.stack([ws, jnp.tile(ws[:, :DEC_SEQ, :DEC_SEQ], (1, reps, reps))])
    rc = jnp.arange(C_CHUNK)
    tril = rc[:, None] >= rc[None, :]
    mask2 = jnp.stack([tril, tril & ((rc[:, None] // DEC_SEQ) == (rc[None, :] // DEC_SEQ))]).astype(F32)
    bias_p = jnp.repeat(c_b_s[0].T, D_C // C_GROUPS, axis=1)
    bias2 = jnp.stack([bias_p, jnp.tile(bias_p[:DEC_SEQ], (reps, 1))])
    wr1, br1 = _router_weights(moe_w_router[1], moe_b_router[1])
    x, gates, dest, counts = _c_out(uc, vc, x, wmix2, mask2, bias2, c_w_out[0].astype(BF16), *lnrow(1, 0),
                                    wr1, br1, tri)
    y_prompt, y_sample = _moe_post_norm(1, x, gates, dest, counts, moe_w1, moe_b1, moe_w2, moe_b2,
                                        ln_g[1, 1], ln_b[1, 1], split=True)
    y_sample = y_sample.reshape(DEC_BATCH, DEC_SEQ, D_MODEL)
    ka, kq = CONV_A - 1, CONV_B - 1
    return (y_prompt, y_sample,
            pu8[None, :, SUBLANES - ka:], pq8[None, :, SUBLANES - kq:], p_delta[None],
            su8[None, :, SUBLANES - ka:], sq8[None, :, SUBLANES - kq:], s_delta[None],
            vc[N_PROMPT:].reshape(1, DEC_BATCH, DEC_SEQ, D_C))
```

```python
import functools
import math

import jax
import jax.numpy as jnp
from jax import lax
from jax.experimental import pallas as pl
from jax.experimental.pallas import tpu as pltpu

F32 = jnp.float32
BF16 = jnp.bfloat16
I32 = jnp.int32

D_MODEL = 1024
BATCH = 8
SEQ = 2048
DEC_BATCH = 128
DEC_SEQ = 8
N_PROMPT = BATCH * SEQ
N_SAMPLE = DEC_BATCH * DEC_SEQ
N_TOK = N_PROMPT + N_SAMPLE
D_A = 512
CONV_A = 3
H_B = 4
DK = 128
DV = 128
D_QKV = 1536
CONV_B = 4
DN_CHUNK = 64
W_MAIN = 3 * D_A + D_QKV + H_B * DV
D_C = 1024
C_GROUPS = 8
C_CHUNK = 128
N_EXP = 32
TOP_K = 4
D_EXP = 1024
SWIGLU_ALPHA = 1.702
SWIGLU_LIMIT = 7.0
DEEPNORM_ALPHA = 4.0 ** 0.25
LN_EPS = 1e-5
RMS_EPS = 1e-6

LANES = 128
SUBLANES = 8
VMEM_LIMIT = 56 * 1024 * 1024

TM = 512
TM_E = 512
TM_C = 256
TM_D = 256
GDN_CPS = 4
N_PAIRS = N_TOK * TOP_K
N_EBLOCKS = (N_PAIRS + N_EXP * (TM_E - 1)) // TM_E
ROWS_TOTAL = N_EBLOCKS * TM_E
RANK_BITS = 15
assert N_TOK <= 1 << RANK_BITS


def _cparams(sem):
    return pltpu.CompilerParams(dimension_semantics=sem, vmem_limit_bytes=VMEM_LIMIT)


def _layer_norm(t, g, b):
    mu = jnp.mean(t, axis=-1, keepdims=True)
    d = t - mu
    var = jnp.mean(d * d, axis=-1, keepdims=True)
    return d * lax.rsqrt(var + LN_EPS) * g + b


def _bdot(a, b):
    return jnp.dot(a.astype(BF16), b.astype(BF16), preferred_element_type=F32)


def _token_tile(xp_ref, xs_ref):
    return jnp.where(pl.program_id(0) < N_PROMPT // TM, xp_ref[...], xs_ref[...])


def _token_specs():
    n_p = N_PROMPT // TM
    return [pl.BlockSpec((TM, D_MODEL), lambda i: (jnp.minimum(i, n_p - 1), 0)),
            pl.BlockSpec((TM, D_MODEL), lambda i: (jnp.maximum(i - n_p, 0), 0))]


def _proj_ab_body(xp_ref, xs_ref, w_ref, wab_ref, alog_ref, dtb_ref, bg_ref, u_ref, qkv_ref, z_ref, gb_ref):
    xb = _token_tile(xp_ref, xs_ref).astype(BF16)

    def mm(lo, hi):
        return jnp.dot(xb, w_ref[:, lo:hi], preferred_element_type=F32)

    bg_ref[...] = mm(0, D_A)
    u_ref[...] = mm(D_A, 2 * D_A) * mm(2 * D_A, 3 * D_A)
    for c in range(D_QKV // 512):
        qkv_ref[:, c * 512:(c + 1) * 512] = mm(3 * D_A + c * 512, 3 * D_A + (c + 1) * 512)
    z_ref[...] = mm(3 * D_A + D_QKV, W_MAIN)
    ab = jnp.dot(xb, wab_ref[...], preferred_element_type=F32)
    g = -jnp.exp(alog_ref[...]) * jax.nn.softplus(ab + dtb_ref[...])
    beta = jax.nn.sigmoid(ab)
    lane = lax.broadcasted_iota(I32, ab.shape, 1)
    gb_ref[...] = jnp.where(lane < H_B, g, beta)


def _proj_ab(xp, xs, w_main, w_ab, alog_row, dtb_row):
    n = xp.shape[0] + xs.shape[0]
    row = lambda w: pl.BlockSpec((TM, w), lambda i: (i, 0))
    full = lambda a: pl.BlockSpec(a.shape, lambda i: (0,) * a.ndim)
    return pl.pallas_call(
        _proj_ab_body,
        grid=(n // TM,),
        in_specs=_token_specs() + [full(w_main), full(w_ab), full(alog_row), full(dtb_row)],
        out_specs=[row(D_A), row(D_A), row(D_QKV), row(D_A), row(LANES)],
        out_shape=[jax.ShapeDtypeStruct((n, D_A), F32), jax.ShapeDtypeStruct((n, D_A), F32),
                   jax.ShapeDtypeStruct((n, D_QKV), F32), jax.ShapeDtypeStruct((n, D_A), F32),
                   jax.ShapeDtypeStruct((n, LANES), F32)],
        compiler_params=_cparams(("arbitrary",)),
        name="proj_ab",
    )(xp, xs, w_main, w_ab, alog_row, dtb_row)


def _shift_rows(x, prev8, s):
    if s == 0:
        return x
    xr = pltpu.roll(x, s, axis=0)
    pr = pltpu.roll(prev8, s, axis=0)
    rid = lax.broadcasted_iota(I32, pr.shape, 0)
    head = jnp.where(rid < s, pr, xr[0:SUBLANES])
    if x.shape[0] == SUBLANES:
        return head
    return jnp.concatenate([head, xr[SUBLANES:]], axis=0)


def _causal_conv(x, prev8, w_ref, taps):
    y = x * w_ref[taps - 1:taps, :]
    for s in range(1, taps):
        y = y + _shift_rows(x, prev8, s) * w_ref[taps - 1 - s:taps - s, :]
    return y


def _lane_scan(x, pos, chunk, reverse):
    s = 1
    while s < chunk:
        if reverse:
            x = x + jnp.where(pos < chunk - s, pltpu.roll(x, LANES - s, axis=1), 0.0)
        else:
            x = x + jnp.where(pos >= s, pltpu.roll(x, s, axis=1), 0.0)
        s *= 2
    return x


def _per_row(row):
    return jnp.broadcast_to(row, (LANES, LANES)).T


def _gdn_body(chunk, bb, cps, n_steps,
              qkv_ref, u_ref, bg_ref, z_ref, gbr_ref, pq_ref, pu_ref, s0_ref, wq_ref, wa_ref, ng_ref, y_any,
              ycat_ref, nq_ref, nu_ref, sn_ref, cq_scr, cu_scr, s_scr):
    del y_any
    n = pl.program_id(1)

    @pl.when(n == 0)
    def _():
        cq_scr[...] = pq_ref[...]
        cu_scr[...] = pu_ref[...]
        s_scr[...] = s0_ref[...]

    gsz = LANES // chunk
    n_groups = bb * cps * H_B // gsz
    levels = int(math.log2(chunk))
    span = cps * chunk

    ii = lax.broadcasted_iota(I32, (LANES, LANES), 0)
    jj = lax.broadcasted_iota(I32, (LANES, LANES), 1)
    same = (ii // chunk) == (jj // chunk)
    m_incl = same & (ii >= jj)
    m_strict = same & (ii > jj)
    eye = (ii == jj).astype(F32)
    pos = lax.broadcasted_iota(I32, (SUBLANES, LANES), 1) % chunk

    qs, ks, vs = {}, {}, {}
    for b in range(bb):
        rows = slice(b * span, (b + 1) * span)
        x = qkv_ref[rows, :]
        qc = _causal_conv(x, cq_scr[b], wq_ref, CONV_B)
        qc = qc * jax.nn.sigmoid(qc)
        cq_scr[b] = x[span - SUBLANES:span]
        uu = u_ref[rows, :]
        ca = _causal_conv(uu, cu_scr[b], wa_ref, CONV_A)
        cu_scr[b] = uu[span - SUBLANES:span]
        ycat_ref[rows, 0:D_A] = (bg_ref[rows, :] * ca).astype(BF16)
        for j in range(cps):
            r = slice(j * chunk, (j + 1) * chunk)
            for h in range(H_B):
                qh = qc[r, h * DK:(h + 1) * DK]
                kh = qc[r, H_B * DK + h * DK:H_B * DK + (h + 1) * DK]
                vh = qc[r, 2 * H_B * DK + h * DV:2 * H_B * DK + (h + 1) * DV]
                key = (b * cps + j, h)
                qs[key] = qh * (lax.rsqrt(jnp.sum(qh * qh, axis=-1, keepdims=True) + RMS_EPS) * (DK ** -0.5))
                ks[key] = kh * lax.rsqrt(jnp.sum(kh * kh, axis=-1, keepdims=True) + RMS_EPS)
                vs[key] = vh

    for gi in range(n_groups):
        blocks = [divmod(gi * gsz + t, H_B) for t in range(gsz)]
        cat = lambda d: jnp.concatenate([d[uh] for uh in blocks], axis=0) if gsz > 1 else d[blocks[0]]
        qg, kg, vg = cat(qs), cat(ks), cat(vs)

        tile = gbr_ref[gi]
        gc = _lane_scan(tile, pos, chunk, False)
        rs = _lane_scan(tile, pos, chunk, True) - tile
        gc_row = gc[0:1]
        gc_m = _per_row(gc_row)
        rs_m = _per_row(rs[0:1])
        beta_m = _per_row(tile[1:2])
        diff = gc_m - jnp.broadcast_to(gc_row, (LANES, LANES))
        decay = jnp.where(m_incl, jnp.exp(jnp.where(m_incl, diff, 0.0)), 0.0)
        eg = jnp.exp(gc_m)
        etot = jnp.exp(gc_m + rs_m)

        kb = kg * beta_m
        kgb = kg.astype(BF16)
        a_mat = lax.dot_general(kb.astype(BF16), kgb, (((1,), (1,)), ((), ())), preferred_element_type=F32)
        lm = jnp.where(m_strict, a_mat * decay, 0.0)
        attn = lax.dot_general(qg.astype(BF16), kgb, (((1,), (1,)), ((), ())), preferred_element_type=F32) * decay

        p = eye - lm
        m = _bdot(lm, lm)
        for lvl in range(1, levels):
            p = p + _bdot(p, m)
            if lvl < levels - 1:
                m = _bdot(m, m)
        uw = _bdot(p, jnp.concatenate([vg * beta_m, kb * eg], axis=1))
        u_all, w_all = uw[:, :DV], uw[:, DV:]
        qe = qg * eg
        kdec = kg * jnp.exp(rs_m)

        vnew, qsv = [], []
        for t, (unit, h) in enumerate(blocks):
            b = unit // cps
            r = slice(t * chunk, (t + 1) * chunk)
            s_bf = s_scr[b, h].astype(BF16)
            lhs = jnp.concatenate([w_all[r], qe[r]], axis=0).astype(BF16)
            both = jnp.dot(lhs, s_bf, preferred_element_type=F32)
            vnew.append(u_all[r] - both[:chunk])
            qsv.append(both[chunk:])
        vnew_g = jnp.concatenate(vnew, axis=0) if gsz > 1 else vnew[0]
        qs_g = jnp.concatenate(qsv, axis=0) if gsz > 1 else qsv[0]
        o = qs_g + _bdot(attn, vnew_g)
        o = o * lax.rsqrt(jnp.mean(o * o, axis=-1, keepdims=True) + RMS_EPS) * ng_ref[...]

        for t, (unit, h) in enumerate(blocks):
            b = unit // cps
            r = slice(t * chunk, (t + 1) * chunk)
            rows = slice(unit * chunk, (unit + 1) * chunk)
            zz = z_ref[rows, h * DV:(h + 1) * DV]
            ycat_ref[rows, D_A + h * DV:D_A + (h + 1) * DV] = (o[r] * (zz * jax.nn.sigmoid(zz))).astype(BF16)
            upd = lax.dot_general(kdec[r].astype(BF16), vnew[t].astype(BF16), (((0,), (0,)), ((), ())),
                                  preferred_element_type=F32)
            scale = jnp.broadcast_to(etot[t * chunk:t * chunk + 1, :], (DK, DV))
            s_scr[b, h] = s_scr[b, h] * scale + upd

    @pl.when(n == n_steps - 1)
    def _():
        nq_ref[...] = cq_scr[...]
        nu_ref[...] = cu_scr[...]
        sn_ref[...] = s_scr[...]


def _gdn_phased_body(chunk, cps, n_steps,
                     qkv_ref, u_ref, bg_ref, z_ref, gbr_ref, pq_ref, pu_ref, s0_ref, wq_ref, wa_ref, ng_ref, y_any,
                     ycat_ref, nq_ref, nu_ref, sn_ref, cq_scr, cu_scr, s_scr,
                     kf, qf, vf, k16, kb16, q16, kdec16, att16, m16, c16, vk16, uw16,
                     dec, pm, qe, etot, dm, om, n16, bm):
    del y_any
    n = pl.program_id(1)

    @pl.when(n == 0)
    def _():
        cq_scr[...] = pq_ref[...]
        cu_scr[...] = pu_ref[...]
        s_scr[...] = s0_ref[...]

    gsz = LANES // chunk
    n_groups = cps * H_B // gsz
    levels = int(math.log2(chunk))
    span = cps * chunk
    groups = range(n_groups)
    blocks_of = lambda gi: [divmod(gi * gsz + t, H_B) for t in range(gsz)]
    rows_of = lambda t: slice(t * chunk, (t + 1) * chunk)

    ii = lax.broadcasted_iota(I32, (LANES, LANES), 0)
    jj = lax.broadcasted_iota(I32, (LANES, LANES), 1)
    same = (ii // chunk) == (jj // chunk)
    m_incl = same & (ii >= jj)
    m_strict = same & (ii > jj)
    eye = (ii == jj).astype(F32)
    pos = lax.broadcasted_iota(I32, (SUBLANES, LANES), 1) % chunk
    nt = (((1,), (1,)), ((), ()))
    tn = (((0,), (0,)), ((), ()))

    x = qkv_ref[...]
    qc = _causal_conv(x, cq_scr[0], wq_ref, CONV_B)
    qc = qc * jax.nn.sigmoid(qc)
    cq_scr[0] = x[span - SUBLANES:span]
    uu = u_ref[...]
    ca = _causal_conv(uu, cu_scr[0], wa_ref, CONV_A)
    cu_scr[0] = uu[span - SUBLANES:span]
    ycat_ref[:, 0:D_A] = (bg_ref[...] * ca).astype(BF16)
    for gi in groups:
        for t, (j, h) in enumerate(blocks_of(gi)):
            r = rows_of(j)
            qh = qc[r, h * DK:(h + 1) * DK]
            kh = qc[r, H_B * DK + h * DK:H_B * DK + (h + 1) * DK]
            qf[gi, rows_of(t), :] = qh * (lax.rsqrt(jnp.sum(qh * qh, axis=-1, keepdims=True) + RMS_EPS)
                                          * (DK ** -0.5))
            kf[gi, rows_of(t), :] = kh * lax.rsqrt(jnp.sum(kh * kh, axis=-1, keepdims=True) + RMS_EPS)
            vf[gi, rows_of(t), :] = qc[r, 2 * H_B * DK + h * DV:2 * H_B * DK + (h + 1) * DV]

    for gi in groups:
        tile = gbr_ref[gi]
        gc = _lane_scan(tile, pos, chunk, False)
        rs = _lane_scan(tile, pos, chunk, True) - tile
        gc_row = gc[0:1]
        gc_m = _per_row(gc_row)
        rs_m = _per_row(rs[0:1])
        beta_m = _per_row(tile[1:2])
        diff = gc_m - jnp.broadcast_to(gc_row, (LANES, LANES))
        dec[gi] = jnp.where(m_incl, jnp.exp(jnp.where(m_incl, diff, 0.0)), 0.0)
        eg = jnp.exp(gc_m)
        etot[gi] = jnp.exp(gc_m + rs_m)
        kg = kf[gi]
        kb = kg * beta_m
        k16[gi] = kg.astype(BF16)
        kb16[gi] = kb.astype(BF16)
        q16[gi] = qf[gi].astype(BF16)
        qe[gi] = qf[gi] * eg
        kdec16[gi] = (kg * jnp.exp(rs_m)).astype(BF16)
        vk16[gi, :, 0:DV] = (vf[gi] * beta_m).astype(BF16)
        vk16[gi, :, DV:] = (kb * eg).astype(BF16)

    for gi in groups:
        a_mat = lax.dot_general(kb16[gi], k16[gi], nt, preferred_element_type=F32)
        lm = jnp.where(m_strict, a_mat * dec[gi], 0.0)
        pm[gi] = eye - lm
        lm16 = lm.astype(BF16)
        m16[gi] = jnp.dot(lm16, lm16, preferred_element_type=F32).astype(BF16)
        att16[gi] = (lax.dot_general(q16[gi], k16[gi], nt, preferred_element_type=F32) * dec[gi]).astype(BF16)

    for lvl in range(1, levels):
        for gi in groups:
            pm[gi] = pm[gi] + jnp.dot(pm[gi].astype(BF16), m16[gi], preferred_element_type=F32)
        if lvl < levels - 1:
            for gi in groups:
                m16[gi] = jnp.dot(m16[gi], m16[gi], preferred_element_type=F32).astype(BF16)

    for gi in groups:
        uw16[gi] = jnp.dot(pm[gi].astype(BF16), vk16[gi], preferred_element_type=F32).astype(BF16)
    for gi in groups:
        au = jnp.dot(att16[gi], uw16[gi], preferred_element_type=F32)
        dm[gi] = au[:, :DV]
        c16[gi] = (qe[gi] - au[:, DV:]).astype(BF16)
        for t in range(gsz):
            r = rows_of(t)
            nb = lax.dot_general(kdec16[gi, r, :], uw16[gi, r, :], tn, preferred_element_type=F32)
            bm[gi * gsz + t] = nb[:, :DV]
            n16[gi * gsz + t] = nb[:, DV:].astype(BF16)

    for gi in groups:
        for t, (j, h) in enumerate(blocks_of(gi)):
            r = rows_of(t)
            s_old = s_scr[0, h]
            lhs = jnp.concatenate([c16[gi, r, :], n16[gi * gsz + t]], axis=0)
            both = jnp.dot(lhs, s_old.astype(BF16), preferred_element_type=F32)
            om[gi, r, :] = both[:chunk] + dm[gi, r, :]
            scale = jnp.broadcast_to(etot[gi, t * chunk:t * chunk + 1, :], (DK, DV))
            s_scr[0, h] = s_old * scale - both[chunk:] + bm[gi * gsz + t]

    for gi in groups:
        o = om[gi]
        o = o * lax.rsqrt(jnp.mean(o * o, axis=-1, keepdims=True) + RMS_EPS) * ng_ref[...]
        for t, (j, h) in enumerate(blocks_of(gi)):
            zz = z_ref[rows_of(j), h * DV:(h + 1) * DV]
            ycat_ref[rows_of(j), D_A + h * DV:D_A + (h + 1) * DV] = (
                o[rows_of(t)] * (zz * jax.nn.sigmoid(zz))).astype(BF16)

    @pl.when(n == n_steps - 1)
    def _():
        nq_ref[...] = cq_scr[...]
        nu_ref[...] = cu_scr[...]
        sn_ref[...] = s_scr[...]


def _gdn_phased_scratch(chunk, cps):
    g = cps * H_B * chunk // LANES
    nblk = cps * H_B
    mat = lambda dt, n=g, w=LANES: pltpu.VMEM((n, LANES, w), dt)
    return ([mat(F32)] * 3 + [mat(BF16)] * 7 + [mat(BF16, w=2 * LANES)] * 2 + [mat(F32)] * 6
            + [mat(BF16, n=nblk), mat(F32, n=nblk)])


def _gdn(qkv, u, bg, z, gbr, prev_q, prev_u, s0, wq, wa, ng, ycat, *, chunk, bb, cps, n_seq, t_len, row_block0,
         phased=False):
    rb = bb * cps * chunk
    gs = rb * H_B // LANES
    n_steps = t_len // (cps * chunk)
    rowmap = lambda i, n: (row_block0 + i * n_steps + n, 0)
    row = lambda w: pl.BlockSpec((rb, w), rowmap)
    seq3 = lambda w: pl.BlockSpec((bb, SUBLANES, w), lambda i, n: (i, 0, 0))
    full = lambda a: pl.BlockSpec(a.shape, lambda i, n: (0,) * a.ndim)
    st = pl.BlockSpec((bb, H_B, DK, DV), lambda i, n: (i, 0, 0, 0))
    if phased:
        assert bb == 1
        body = functools.partial(_gdn_phased_body, chunk, cps, n_steps)
        extra_scratch = _gdn_phased_scratch(chunk, cps)
    else:
        body = functools.partial(_gdn_body, chunk, bb, cps, n_steps)
        extra_scratch = []
    return pl.pallas_call(
        body,
        grid=(n_seq // bb, n_steps),
        in_specs=[row(D_QKV), row(D_A), row(D_A), row(D_A),
                  pl.BlockSpec((None, gs, SUBLANES, LANES), lambda i, n: (i * n_steps + n, 0, 0, 0)),
                  seq3(D_QKV), seq3(D_A), st, full(wq), full(wa), full(ng),
                  pl.BlockSpec(memory_space=pl.ANY)],
        out_specs=[pl.BlockSpec((rb, D_MODEL), rowmap), seq3(D_QKV), seq3(D_A), st],
        out_shape=[jax.ShapeDtypeStruct(ycat.shape, BF16),
                   jax.ShapeDtypeStruct((n_seq, SUBLANES, D_QKV), F32),
                   jax.ShapeDtypeStruct((n_seq, SUBLANES, D_A), F32),
                   jax.ShapeDtypeStruct((n_seq, H_B, DK, DV), F32)],
        scratch_shapes=[pltpu.VMEM((bb, SUBLANES, D_QKV), F32), pltpu.VMEM((bb, SUBLANES, D_A), F32),
                        pltpu.VMEM((bb, H_B, DK, DV), F32)] + extra_scratch,
        input_output_aliases={11: 0},
        compiler_params=_cparams(("arbitrary", "arbitrary")),
        name=f"gdn_c{chunk}",
    )(qkv, u, bg, z, gbr, prev_q, prev_u, s0, wq, wa, ng, ycat)


def _group_rows(gb, n_seq, t_len, chunk, bb, cps):
    n_steps = t_len // (cps * chunk)
    gs = bb * cps * chunk * H_B // LANES
    g = gb[:, :2 * H_B].reshape(n_seq // bb, bb, n_steps, cps, chunk, 2, H_B)
    g = jnp.transpose(g, (0, 2, 5, 1, 3, 6, 4))
    g = g.reshape(n_seq // bb * n_steps, 2, gs, LANES)
    g = jnp.transpose(g, (0, 2, 1, 3))
    return jnp.pad(g, ((0, 0), (0, 0), (0, SUBLANES - 2), (0, 0)))


def _route(x_new, wr_ref, br_ref, tri_ref, cnt_scr, gate_ref, dest_ref, counts_ref):
    @pl.when(pl.program_id(0) == 0)
    def _():
        cnt_scr[...] = jnp.zeros(cnt_scr.shape, F32)

    xh = x_new.astype(BF16)
    xl = (x_new - xh.astype(F32)).astype(BF16)
    p = jnp.dot(xh, wr_ref[...], preferred_element_type=F32)
    logits = (p[:, :LANES] + p[:, LANES:] + jnp.dot(xl, wr_ref[:, :LANES], preferred_element_type=F32)
              + br_ref[...])
    lane = lax.broadcasted_iota(I32, logits.shape, 1).astype(F32)
    neg = jnp.float32(-jnp.inf)
    l = jnp.where(lane < N_EXP, logits, neg)
    val_out = jnp.full(logits.shape, neg, F32)
    sels, hots = [], []
    for k in range(TOP_K):
        m = jnp.max(l, axis=-1, keepdims=True)
        sel = jnp.min(jnp.where(l == m, lane, float(LANES)), axis=-1, keepdims=True)
        hit = lane == sel
        val_out = jnp.where(lane == k, m, val_out)
        l = jnp.where(hit, neg, l)
        sels.append(sel)
        hots.append(hit.astype(F32))
    e = jnp.exp(val_out - jnp.max(val_out, axis=-1, keepdims=True))
    gate_ref[...] = e / jnp.sum(e, axis=-1, keepdims=True)

    hot = hots[0] + hots[1] + hots[2] + hots[3]
    before = jnp.dot(tri_ref[...], hot.astype(BF16), preferred_element_type=F32) + cnt_scr[...]
    dest = jnp.zeros(logits.shape, F32)
    for k in range(TOP_K):
        rank = jnp.sum(hots[k] * before, axis=-1, keepdims=True)
        dest = jnp.where(lane == k, sels[k] * float(1 << RANK_BITS) + rank, dest)
    dest_ref[...] = dest.astype(I32)
    cnt_scr[...] = cnt_scr[...] + jnp.sum(hot, axis=0, keepdims=True)
    counts_ref[...] = cnt_scr[...]


_ROUTE_OUT_SPECS = [pl.BlockSpec((TM, LANES), lambda i: (i, 0)), pl.BlockSpec((TM, LANES), lambda i: (i, 0)),
                    pl.BlockSpec((1, LANES), lambda i: (0, 0))]


def _route_out_shapes(n):
    return [jax.ShapeDtypeStruct((n, LANES), F32), jax.ShapeDtypeStruct((n, LANES), I32),
            jax.ShapeDtypeStruct((1, LANES), F32)]


def _mm_res_ln_body(y_ref, xp_ref, xs_ref, w_ref, g_ref, b_ref, wr_ref, br_ref, tri_ref,
                    o_ref, gate_ref, dest_ref, counts_ref, cnt_scr):
    h = jnp.dot(y_ref[...].astype(BF16), w_ref[...], preferred_element_type=F32)
    xn = _layer_norm(DEEPNORM_ALPHA * _token_tile(xp_ref, xs_ref) + h, g_ref[...], b_ref[...])
    o_ref[...] = xn
    _route(xn, wr_ref, br_ref, tri_ref, cnt_scr, gate_ref, dest_ref, counts_ref)


def _mm_res_ln(y, xp, xs, w, g, b, wr, br, tri):
    n = xp.shape[0] + xs.shape[0]
    row = pl.BlockSpec((TM, D_MODEL), lambda i: (i, 0))
    full = lambda a: pl.BlockSpec(a.shape, lambda i: (0,) * a.ndim)
    return pl.pallas_call(
        _mm_res_ln_body,
        grid=(n // TM,),
        in_specs=[row] + _token_specs() + [full(w), full(g), full(b), full(wr), full(br), full(tri)],
        out_specs=[row] + _ROUTE_OUT_SPECS,
        out_shape=[jax.ShapeDtypeStruct((n, D_MODEL), F32)] + _route_out_shapes(n),
        scratch_shapes=[pltpu.VMEM((1, LANES), F32)],
        compiler_params=_cparams(("arbitrary",)),
        name="mm_res_ln",
    )(y, xp, xs, w, g, b, wr, br, tri)


def _gelu(x):
    return 0.5 * x * (1.0 + lax.erf(x * (2.0 ** -0.5)))


def _c_out_body(u_ref, v_ref, x_ref, wmix_ref, mask_ref, bias_ref, wout_ref, g_ref, b_ref, wr_ref, br_ref, tri_ref,
                o_ref, gate_ref, dest_ref, counts_ref, us_scr, cnt_scr):
    mix = [(wmix_ref[gi] * mask_ref[...]).astype(BF16) for gi in range(C_GROUPS)]
    for t in range(TM // C_CHUNK):
        rows = slice(t * C_CHUNK, (t + 1) * C_CHUNK)
        for gi in range(C_GROUPS):
            cols = slice(gi * LANES, (gi + 1) * LANES)
            s = jnp.dot(mix[gi], v_ref[rows, cols].astype(BF16), preferred_element_type=F32) + bias_ref[:, cols]
            us_scr[rows, cols] = (u_ref[rows, cols] * s).astype(BF16)
    h = jnp.dot(us_scr[...], wout_ref[...], preferred_element_type=F32)
    xn = _layer_norm(DEEPNORM_ALPHA * x_ref[...] + h, g_ref[...], b_ref[...])
    o_ref[...] = xn
    _route(xn, wr_ref, br_ref, tri_ref, cnt_scr, gate_ref, dest_ref, counts_ref)


def _c_out(u, v, x, wmix2, mask2, bias2, wout, g, b, wr, br, tri):
    n = x.shape[0]
    first_sample_step = N_PROMPT // TM
    sel = lambda i: jnp.where(i >= first_sample_step, 1, 0)
    row = pl.BlockSpec((TM, D_MODEL), lambda i: (i, 0))
    full = lambda a: pl.BlockSpec(a.shape, lambda i: (0,) * a.ndim)
    return pl.pallas_call(
        _c_out_body,
        grid=(n // TM,),
        in_specs=[row, row, row,
                  pl.BlockSpec((None, C_GROUPS, C_CHUNK, C_CHUNK), lambda i: (sel(i), 0, 0, 0)),
                  pl.BlockSpec((None, C_CHUNK, C_CHUNK), lambda i: (sel(i), 0, 0)),
                  pl.BlockSpec((None, C_CHUNK, D_C), lambda i: (sel(i), 0, 0)),
                  full(wout), full(g), full(b), full(wr), full(br), full(tri)],
        out_specs=[row] + _ROUTE_OUT_SPECS,
        out_shape=[jax.ShapeDtypeStruct((n, D_MODEL), F32)] + _route_out_shapes(n),
        scratch_shapes=[pltpu.VMEM((TM, D_C), BF16), pltpu.VMEM((1, LANES), F32)],
        compiler_params=_cparams(("arbitrary",)),
        name="c_out",
    )(u, v, x, wmix2, mask2, bias2, wout, g, b, wr, br, tri)


def _for_each_row(n_rows, fn):
    def group(t8, c):
        base = pl.multiple_of(t8 * SUBLANES, SUBLANES)
        for s in range(SUBLANES):
            fn(t8, base, s)
        return c
    lax.fori_loop(0, n_rows // SUBLANES, group, 0)


def _dispatch_body(pad_start_ref, pad_n_ref, n_used_ref, dest_ref, x_ref, rows_hbm, zbuf, sem, zsem):
    i = pl.program_id(0)

    @pl.when(i == 0)
    def _():
        zbuf[...] = jnp.zeros(zbuf.shape, F32)

    @pl.when(i < N_EXP)
    def _():
        n = pad_n_ref[i]
        start = pad_start_ref[i]
        odd = n & (SUBLANES - 1)
        copies = [(pltpu.make_async_copy(zbuf.at[pl.ds(0, 1)], rows_hbm.at[pl.ds(start + s, 1)], zsem), s < odd)
                  for s in range(SUBLANES - 1)]
        off = start + odd
        for size in [1 << p for p in reversed(range(3, int(math.log2(TM_E))))]:
            dst = rows_hbm.at[pl.ds(pl.multiple_of(off, SUBLANES), size)]
            copies.append((pltpu.make_async_copy(zbuf.at[pl.ds(0, size)], dst, zsem), (n & size) != 0))
            off = off + (n & size)
        for cp, used in copies:
            pl.when(used)(cp.start)
        for cp, used in copies:
            pl.when(used)(cp.wait)

    blk = n_used_ref[0] + (i - N_EXP)

    @pl.when(jnp.logical_and(i >= N_EXP, blk < N_EBLOCKS))
    def _():
        cp = pltpu.make_async_copy(zbuf, rows_hbm.at[pl.ds(pl.multiple_of(blk * TM_E, TM_E), TM_E)], zsem)
        cp.start()
        cp.wait()

    def push(t8, base, s):
        src = x_ref.at[pl.ds(base, SUBLANES)].at[pl.ds(s, 1)]
        for k in range(TOP_K):
            d = dest_ref[0, 0, (t8 * SUBLANES + s) * TOP_K + k]
            pltpu.make_async_copy(src, rows_hbm.at[pl.ds(d, 1)], sem).start(priority=k % 2)
    _for_each_row(TM_D, push)

    def drain(t8, base, s):
        for k in range(TOP_K):
            pltpu.make_async_copy(x_ref.at[pl.ds(0, 1)], rows_hbm.at[pl.ds(0, 1)], sem).wait()
    _for_each_row(TM_D, drain)


def _dispatch(pads, dest3, x):
    n = x.shape[0]
    assert n // TM_D >= N_EXP + (N_EBLOCKS - N_PAIRS // TM_E)
    return pl.pallas_call(
        _dispatch_body,
        grid_spec=pltpu.PrefetchScalarGridSpec(
            num_scalar_prefetch=3,
            grid=(n // TM_D,),
            in_specs=[pl.BlockSpec((1, 1, TM_D * TOP_K), lambda i, *_: (i, 0, 0), memory_space=pltpu.SMEM),
                      pl.BlockSpec((TM_D, D_MODEL), lambda i, *_: (i, 0))],
            out_specs=pl.BlockSpec(memory_space=pl.ANY),
            scratch_shapes=[pltpu.VMEM((TM_E, D_MODEL), F32), pltpu.SemaphoreType.DMA(()),
                            pltpu.SemaphoreType.DMA(())]),
        out_shape=jax.ShapeDtypeStruct((ROWS_TOTAL, D_MODEL), F32),
        compiler_params=_cparams(("arbitrary",)),
        name="moe_dispatch",
    )(*pads, dest3, x)


def _experts_body(layer, bexp_ref, bval_ref, bnext_ref, x_ref, w1_hbm, b1_ref, w2_hbm, b2_ref, o_ref,
                  w1s, w2s, slot_ref, sem):
    i = pl.program_id(0)
    valid = bval_ref[i] != 0
    fresh = jnp.logical_or(i == 0, bexp_ref[i] != bexp_ref[jnp.maximum(i - 1, 0)])

    def weight_copies(e, slot):
        return (pltpu.make_async_copy(w1_hbm.at[layer, e], w1s.at[slot], sem.at[slot, 0]),
                pltpu.make_async_copy(w2_hbm.at[layer, e], w2s.at[slot], sem.at[slot, 1]))

    @pl.when(jnp.logical_and(valid, fresh))
    def _():
        @pl.when(i == 0)
        def _():
            slot_ref[0] = 1
            for cp in weight_copies(bexp_ref[i], 0):
                cp.start()
        slot = 1 - slot_ref[0]
        slot_ref[0] = slot
        for cp in weight_copies(bexp_ref[i], slot):
            cp.wait()

        @pl.when(bnext_ref[i] >= 0)
        def _():
            for cp in weight_copies(bnext_ref[i], 1 - slot):
                cp.start()

    @pl.when(jnp.logical_not(valid))
    def _():
        o_ref[...] = jnp.zeros(o_ref.shape, F32)

    @pl.when(valid)
    def _():
        slot = slot_ref[0]
        x = x_ref[...]
        glu = jnp.dot(x, w1s[slot, :, :D_EXP], preferred_element_type=F32) + b1_ref[:, :D_EXP]
        lin = jnp.dot(x, w1s[slot, :, D_EXP:], preferred_element_type=F32) + b1_ref[:, D_EXP:]
        glu = jnp.minimum(glu, SWIGLU_LIMIT)
        lin = jnp.clip(lin, -SWIGLU_LIMIT, SWIGLU_LIMIT)
        act = glu * jax.nn.sigmoid(SWIGLU_ALPHA * glu) * (lin + 1.0)
        o_ref[...] = jnp.dot(act, w2s[slot], preferred_element_type=F32) + b2_ref[...]


def _experts(layer, tables, x_rows, w1, b1, w2, b2):
    bspec = lambda w: pl.BlockSpec((None, None, 1, w), lambda i, be, bv, bn: (layer, be[i], 0, 0))
    rows = pl.BlockSpec((TM_E, D_MODEL), lambda i, be, bv, bn: (i, 0))
    hbm = pl.BlockSpec(memory_space=pl.ANY)
    return pl.pallas_call(
        functools.partial(_experts_body, layer),
        grid_spec=pltpu.PrefetchScalarGridSpec(
            num_scalar_prefetch=3,
            grid=(N_EBLOCKS,),
            in_specs=[rows, hbm, bspec(2 * D_EXP), hbm, bspec(D_MODEL)],
            out_specs=rows,
            scratch_shapes=[pltpu.VMEM((2, D_MODEL, 2 * D_EXP), F32), pltpu.VMEM((2, D_EXP, D_MODEL), F32),
                            pltpu.SMEM((1,), I32), pltpu.SemaphoreType.DMA((2, 2))]),
        out_shape=jax.ShapeDtypeStruct((ROWS_TOTAL, D_MODEL), F32),
        compiler_params=_cparams(("arbitrary",)),
        name="moe_experts",
    )(*tables, x_rows, w1, b1, w2, b2)


def _combine_ln_body(tail, n_tiles, dest_ref, dest_next_ref, gate_ref, x_ref, rows_hbm, g_ref, b_ref, *refs):
    if tail == "c_in":
        w_ref, bi_ref, lg_ref, lb_ref, o_ref, u_ref, v_ref, buf, sem = refs
    else:
        o_p_ref, o_s_ref, buf, sem = refs
    i = pl.program_id(0)

    def gather(d_ref, slot):
        def pull(t8, base, s):
            for k in range(TOP_K):
                d = d_ref[0, 0, (t8 * SUBLANES + s) * TOP_K + k]
                dst = buf.at[slot, k].at[pl.ds(base, SUBLANES)].at[pl.ds(s, 1)]
                pltpu.make_async_copy(rows_hbm.at[pl.ds(d, 1)], dst, sem.at[slot]).start(priority=k % 2)
        _for_each_row(TM_C, pull)

    @pl.when(i == 0)
    def _():
        gather(dest_ref, 0)

    @pl.when(i + 1 < n_tiles)
    def _():
        gather(dest_next_ref, (i + 1) % 2)

    slot = i % 2

    def drain(t8, base, s):
        for k in range(TOP_K):
            pltpu.make_async_copy(rows_hbm.at[pl.ds(0, 1)], buf.at[0, 0].at[pl.ds(0, 1)], sem.at[slot]).wait()
    _for_each_row(TM_C, drain)

    gates = gate_ref[...]
    y = buf[slot, 0] * gates[:, 0:1]
    for k in range(1, TOP_K):
        y = y + buf[slot, k] * gates[:, k:k + 1]
    res = _layer_norm(DEEPNORM_ALPHA * x_ref[...] + y, g_ref[...], b_ref[...])
    if tail == "c_in":
        o_ref[...] = res
        xb = res.astype(BF16)
        u_ref[...] = _gelu(jnp.dot(xb, w_ref[:, :D_C], preferred_element_type=F32) + bi_ref[:, :D_C])
        v = _gelu(jnp.dot(xb, w_ref[:, D_C:], preferred_element_type=F32) + bi_ref[:, D_C:])
        v_ref[...] = _layer_norm(v, lg_ref[...], lb_ref[...])
    else:
        is_prompt = i < N_PROMPT // TM_C

        @pl.when(is_prompt)
        def _():
            o_p_ref[...] = res

        @pl.when(jnp.logical_not(is_prompt))
        def _():
            o_s_ref[...] = res


def _combine_ln(dest3, gates, x, out_rows, g, b, c_in=None):
    n = x.shape[0]
    n_tiles = n // TM_C
    full = lambda a: pl.BlockSpec(a.shape, lambda i: (0,) * a.ndim)
    row = pl.BlockSpec((TM_C, D_MODEL), lambda i: (i, 0))
    dest_spec = lambda f: pl.BlockSpec((1, 1, TM_C * TOP_K), lambda i: (f(i), 0, 0), memory_space=pltpu.SMEM)
    in_specs = [dest_spec(lambda i: i), dest_spec(lambda i: jnp.minimum(i + 1, n_tiles - 1)),
                pl.BlockSpec((TM_C, LANES), lambda i: (i, 0)), row, pl.BlockSpec(memory_space=pl.ANY), full(g), full(b)]
    args = [dest3, dest3, gates, x, out_rows, g, b]
    if c_in is not None:
        in_specs += [full(a) for a in c_in]
        args += list(c_in)
        out_specs = [row, row, row]
        out_shape = [jax.ShapeDtypeStruct((n, D_MODEL), F32)] * 3
    else:
        per_seq = SEQ // TM_C
        last = N_PROMPT // TM_C - 1
        out_specs = [pl.BlockSpec((None, TM_C, D_MODEL),
                                  lambda i: (jnp.minimum(i, last) // per_seq, jnp.minimum(i, last) % per_seq, 0)),
                     pl.BlockSpec((TM_C, D_MODEL), lambda i: (jnp.maximum(i - last - 1, 0), 0))]
        out_shape = [jax.ShapeDtypeStruct((BATCH, SEQ, D_MODEL), F32),
                     jax.ShapeDtypeStruct((N_SAMPLE, D_MODEL), F32)]
    return pl.pallas_call(
        functools.partial(_combine_ln_body, "c_in" if c_in is not None else "final", n_tiles),
        grid=(n_tiles,),
        in_specs=in_specs,
        out_specs=out_specs,
        out_shape=out_shape,
        scratch_shapes=[pltpu.VMEM((2, TOP_K, TM_C, D_MODEL), F32), pltpu.SemaphoreType.DMA((2,))],
        compiler_params=_cparams(("arbitrary",)),
        name="moe_combine_ln",
    )(*args)


def _positions(enc, counts):
    nb = (counts + TM_E - 1) // TM_E
    cum = jnp.cumsum(nb)
    first_blk = cum - nb
    experts = jnp.arange(N_EXP, dtype=I32)
    pair_hot = ((enc >> RANK_BITS)[:, :, None] == experts[None, None, :]).astype(I32)
    dest = jnp.sum(pair_hot * (first_blk * TM_E)[None, None, :], axis=2) + (enc & ((1 << RANK_BITS) - 1))
    n_used = cum[-1]
    blk = jnp.arange(N_EBLOCKS, dtype=I32)
    exp = jnp.minimum(jnp.sum((cum[None, :] <= blk[:, None]).astype(I32), axis=1), N_EXP - 1)
    later = lax.cummin(jnp.where(counts > 0, experts, N_EXP)[::-1])[::-1]
    nxt = jnp.concatenate([later[1:], jnp.full((1,), N_EXP, I32)])
    nxt = jnp.where(nxt >= N_EXP, -1, nxt)
    bnext = jnp.sum((exp[:, None] == experts[None, :]).astype(I32) * nxt[None, :], axis=1)
    pads = (first_blk * TM_E + counts, nb * TM_E - counts, n_used.reshape(1))
    return dest, (exp, (blk < n_used).astype(I32), bnext.astype(I32)), pads


def _moe_post_norm(layer, x, gates, enc, counts, w1, b1, w2, b2, g, b, c_in=None):
    dest4, tables, pads = _positions(enc[:, :TOP_K], counts[0, :N_EXP].astype(I32))
    x_rows = _dispatch(pads, dest4.reshape(N_TOK // TM_D, 1, TM_D * TOP_K), x)
    out_rows = _experts(layer, tables, x_rows, w1, b1.reshape(b1.shape[:2] + (1, 2 * D_EXP)),
                        w2, b2.reshape(b2.shape[:2] + (1, D_MODEL)))
    return _combine_ln(dest4.reshape(N_TOK // TM_C, 1, TM_C * TOP_K), gates, x, out_rows,
                       g.reshape(1, D_MODEL), b.reshape(1, D_MODEL), c_in=c_in)


def _router_weights(w_r, b_r):
    wh = w_r.astype(BF16)
    wl = (w_r - wh.astype(F32)).astype(BF16)
    pad = lambda a: jnp.pad(a, ((0, 0), (0, LANES - N_EXP)))
    return jnp.concatenate([pad(wh), pad(wl)], axis=1), jnp.pad(b_r, (0, LANES - N_EXP)).reshape(1, LANES)


def _tail8(state, keep):
    return jnp.pad(state, ((0, 0), (SUBLANES - keep, 0), (0, 0)))


def kernel(x_prompt, x_sample, state_conv_a, state_conv_qkv, state_delta, ab_w_in, ab_conv_a, ab_conv_qkv,
           ab_a_log, ab_dt_bias, ab_norm_g, ab_w_out, c_w_in, c_b_in, c_ln_g, c_ln_b, c_w_s, c_b_s, c_w_out,
           moe_w_router, moe_b_router, moe_w1, moe_b1, moe_w2, moe_b2, ln_g, ln_b):
    xp, xs = x_prompt.reshape(N_PROMPT, D_MODEL), x_sample.reshape(N_SAMPLE, D_MODEL)
    lnrow = lambda layer, j: (ln_g[layer, j].reshape(1, D_MODEL), ln_b[layer, j].reshape(1, D_MODEL))
    ri = jnp.arange(TM)
    tri = (ri[:, None] > ri[None, :]).astype(BF16)

    w_in = ab_w_in[0]
    w_main = w_in[:, :W_MAIN].astype(BF16)
    w_ab = jnp.pad(w_in[:, W_MAIN:], ((0, 0), (0, LANES - 2 * H_B))).astype(BF16)
    alog_row = jnp.pad(ab_a_log[0], (0, LANES - H_B)).reshape(1, LANES)
    dtb_row = jnp.pad(ab_dt_bias[0], (0, LANES - H_B)).reshape(1, LANES)
    bg, u, qkv, z, gb = _proj_ab(xp, xs, w_main, w_ab, alog_row, dtb_row)

    ng = ab_norm_g[0].reshape(1, DV)
    ycat = jnp.zeros((N_TOK, D_MODEL), BF16)
    gbr_p = _group_rows(gb[:N_PROMPT], BATCH, SEQ, DN_CHUNK, 1, GDN_CPS)
    ycat, pq8, pu8, p_delta = _gdn(
        qkv, u, bg, z, gbr_p,
        jnp.zeros((BATCH, SUBLANES, D_QKV), F32), jnp.zeros((BATCH, SUBLANES, D_A), F32),
        jnp.zeros((BATCH, H_B, DK, DV), F32), ab_conv_qkv[0], ab_conv_a[0], ng, ycat,
        chunk=DN_CHUNK, bb=1, cps=GDN_CPS, n_seq=BATCH, t_len=SEQ, row_block0=0, phased=True)
    bb_s = 16
    gbr_s = _group_rows(gb[N_PROMPT:], DEC_BATCH, DEC_SEQ, DEC_SEQ, bb_s, 1)
    ycat, sq8, su8, s_delta = _gdn(
        qkv, u, bg, z, gbr_s,
        _tail8(state_conv_qkv[0], CONV_B - 1), _tail8(state_conv_a[0], CONV_A - 1), state_delta[0],
        ab_conv_qkv[0], ab_conv_a[0], ng, ycat,
        chunk=DEC_SEQ, bb=bb_s, cps=1, n_seq=DEC_BATCH, t_len=DEC_SEQ, row_block0=N_PROMPT // (bb_s * DEC_SEQ))
    wr0, br0 = _router_weights(moe_w_router[0], moe_b_router[0])
    x, gates, dest, counts = _mm_res_ln(ycat, xp, xs, ab_w_out[0].astype(BF16), *lnrow(0, 0), wr0, br0, tri)
    c_in = (c_w_in[0].astype(BF16), c_b_in[0].reshape(1, 2 * D_C), c_ln_g[0].reshape(1, D_C), c_ln_b[0].reshape(1, D_C))
    x, uc, vc = _moe_post_norm(0, x, gates, dest, counts, moe_w1, moe_b1, moe_w2, moe_b2, ln_g[0, 1], ln_b[0, 1],
                               c_in=c_in)
    ws = c_w_s[0]
    reps = C_CHUNK // DEC_SEQ
    wmix2 = jnp.stack([ws, jnp.tile(ws[:, :DEC_SEQ, :DEC_SEQ], (1, reps, reps))])
    rc = jnp.arange(C_CHUNK)
    tril = rc[:, None] >= rc[None, :]
    mask2 = jnp.stack([tril, tril & ((rc[:, None] // DEC_SEQ) == (rc[None, :] // DEC_SEQ))]).astype(F32)
    bias_p = jnp.repeat(c_b_s[0].T, D_C // C_GROUPS, axis=1)
    bias2 = jnp.stack([bias_p, jnp.tile(bias_p[:DEC_SEQ], (reps, 1))])
    wr1, br1 = _router_weights(moe_w_router[1], moe_b_router[1])
    x, gates, dest, counts = _c_out(uc, vc, x, wmix2, mask2, bias2, c_w_out[0].astype(BF16), *lnrow(1, 0),
                                    wr1, br1, tri)
    y_prompt, y_sample = _moe_post_norm(1, x, gates, dest, counts, moe_w1, moe_b1, moe_w2, moe_b2,
                                        ln_g[1, 1], ln_b[1, 1])
    y_sample = y_sample.reshape(DEC_BATCH, DEC_SEQ, D_MODEL)
    ka, kq = CONV_A - 1, CONV_B - 1
    return (y_prompt, y_sample,
            pu8[None, :, SUBLANES - ka:], pq8[None, :, SUBLANES - kq:], p_delta[None],
            su8[None, :, SUBLANES - ka:], sq8[None, :, SUBLANES - kq:], s_delta[None],
            vc[N_PROMPT:].reshape(1, DEC_BATCH, DEC_SEQ, D_C))
```

```python
import functools
import math

import jax
import jax.numpy as jnp
from jax import lax
from jax.experimental import pallas as pl
from jax.experimental.pallas import tpu as pltpu

F32 = jnp.float32
BF16 = jnp.bfloat16
I32 = jnp.int32

D_MODEL = 1024
BATCH = 8
SEQ = 2048
DEC_BATCH = 128
DEC_SEQ = 8
N_PROMPT = BATCH * SEQ
N_SAMPLE = DEC_BATCH * DEC_SEQ
N_TOK = N_PROMPT + N_SAMPLE
D_A = 512
CONV_A = 3
H_B = 4
DK = 128
DV = 128
D_QKV = 1536
CONV_B = 4
DN_CHUNK = 64
W_MAIN = 3 * D_A + D_QKV + H_B * DV
D_C = 1024
C_GROUPS = 8
C_CHUNK = 128
N_EXP = 32
TOP_K = 4
D_EXP = 1024
SWIGLU_ALPHA = 1.702
SWIGLU_LIMIT = 7.0
DEEPNORM_ALPHA = 4.0 ** 0.25
LN_EPS = 1e-5
RMS_EPS = 1e-6

LANES = 128
SUBLANES = 8
VMEM_LIMIT = 56 * 1024 * 1024

TM = 512
TM_E = 512
TM_C = 256
TM_D = 256
GDN_CPS = 8
N_PAIRS = N_TOK * TOP_K
N_EBLOCKS = (N_PAIRS + N_EXP * (TM_E - 1)) // TM_E
ROWS_TOTAL = N_EBLOCKS * TM_E
RANK_BITS = 15
assert N_TOK <= 1 << RANK_BITS


def _cparams(sem):
    return pltpu.CompilerParams(dimension_semantics=sem, vmem_limit_bytes=VMEM_LIMIT)


def _layer_norm(t, g, b):
    mu = jnp.mean(t, axis=-1, keepdims=True)
    d = t - mu
    var = jnp.mean(d * d, axis=-1, keepdims=True)
    return d * lax.rsqrt(var + LN_EPS) * g + b


def _bdot(a, b):
    return jnp.dot(a.astype(BF16), b.astype(BF16), preferred_element_type=F32)


def _token_tile(xp_ref, xs_ref):
    return jnp.where(pl.program_id(0) < N_PROMPT // TM, xp_ref[...], xs_ref[...])


def _token_specs():
    n_p = N_PROMPT // TM
    return [pl.BlockSpec((TM, D_MODEL), lambda i: (jnp.minimum(i, n_p - 1), 0)),
            pl.BlockSpec((TM, D_MODEL), lambda i: (jnp.maximum(i - n_p, 0), 0))]


def _proj_ab_body(xp_ref, xs_ref, w_ref, wab_ref, alog_ref, dtb_ref, bg_ref, u_ref, qkv_ref, z_ref, gb_ref):
    xb = _token_tile(xp_ref, xs_ref).astype(BF16)

    def mm(lo, hi):
        return jnp.dot(xb, w_ref[:, lo:hi], preferred_element_type=F32)

    bg_ref[...] = mm(0, D_A)
    u_ref[...] = mm(D_A, 2 * D_A) * mm(2 * D_A, 3 * D_A)
    for c in range(D_QKV // 512):
        qkv_ref[:, c * 512:(c + 1) * 512] = mm(3 * D_A + c * 512, 3 * D_A + (c + 1) * 512)
    z_ref[...] = mm(3 * D_A + D_QKV, W_MAIN)
    ab = jnp.dot(xb, wab_ref[...], preferred_element_type=F32)
    g = -jnp.exp(alog_ref[...]) * jax.nn.softplus(ab + dtb_ref[...])
    beta = jax.nn.sigmoid(ab)
    lane = lax.broadcasted_iota(I32, ab.shape, 1)
    gb_ref[...] = jnp.where(lane < H_B, g, beta)


def _proj_ab(xp, xs, w_main, w_ab, alog_row, dtb_row):
    n = xp.shape[0] + xs.shape[0]
    row = lambda w: pl.BlockSpec((TM, w), lambda i: (i, 0))
    full = lambda a: pl.BlockSpec(a.shape, lambda i: (0,) * a.ndim)
    return pl.pallas_call(
        _proj_ab_body,
        grid=(n // TM,),
        in_specs=_token_specs() + [full(w_main), full(w_ab), full(alog_row), full(dtb_row)],
        out_specs=[row(D_A), row(D_A), row(D_QKV), row(D_A), row(LANES)],
        out_shape=[jax.ShapeDtypeStruct((n, D_A), F32), jax.ShapeDtypeStruct((n, D_A), F32),
                   jax.ShapeDtypeStruct((n, D_QKV), F32), jax.ShapeDtypeStruct((n, D_A), F32),
                   jax.ShapeDtypeStruct((n, LANES), F32)],
        compiler_params=_cparams(("arbitrary",)),
        name="proj_ab",
    )(xp, xs, w_main, w_ab, alog_row, dtb_row)


def _shift_rows(x, prev8, s):
    if s == 0:
        return x
    xr = pltpu.roll(x, s, axis=0)
    pr = pltpu.roll(prev8, s, axis=0)
    rid = lax.broadcasted_iota(I32, pr.shape, 0)
    head = jnp.where(rid < s, pr, xr[0:SUBLANES])
    if x.shape[0] == SUBLANES:
        return head
    return jnp.concatenate([head, xr[SUBLANES:]], axis=0)


def _causal_conv(x, prev8, w_ref, taps):
    y = x * w_ref[taps - 1:taps, :]
    for s in range(1, taps):
        y = y + _shift_rows(x, prev8, s) * w_ref[taps - 1 - s:taps - s, :]
    return y


def _lane_scan(x, pos, chunk, reverse):
    s = 1
    while s < chunk:
        if reverse:
            x = x + jnp.where(pos < chunk - s, pltpu.roll(x, LANES - s, axis=1), 0.0)
        else:
            x = x + jnp.where(pos >= s, pltpu.roll(x, s, axis=1), 0.0)
        s *= 2
    return x


def _per_row(row):
    return jnp.broadcast_to(row, (LANES, LANES)).T


def _gdn_body(chunk, bb, cps, n_steps,
              qkv_ref, u_ref, bg_ref, z_ref, gbr_ref, pq_ref, pu_ref, s0_ref, wq_ref, wa_ref, ng_ref, y_any,
              ycat_ref, nq_ref, nu_ref, sn_ref, cq_scr, cu_scr, s_scr):
    del y_any
    n = pl.program_id(1)

    @pl.when(n == 0)
    def _():
        cq_scr[...] = pq_ref[...]
        cu_scr[...] = pu_ref[...]
        s_scr[...] = s0_ref[...]

    gsz = LANES // chunk
    n_groups = bb * cps * H_B // gsz
    levels = int(math.log2(chunk))
    span = cps * chunk

    ii = lax.broadcasted_iota(I32, (LANES, LANES), 0)
    jj = lax.broadcasted_iota(I32, (LANES, LANES), 1)
    same = (ii // chunk) == (jj // chunk)
    m_incl = same & (ii >= jj)
    m_strict = same & (ii > jj)
    eye = (ii == jj).astype(F32)
    pos = lax.broadcasted_iota(I32, (SUBLANES, LANES), 1) % chunk

    qs, ks, vs = {}, {}, {}
    for b in range(bb):
        rows = slice(b * span, (b + 1) * span)
        x = qkv_ref[rows, :]
        qc = _causal_conv(x, cq_scr[b], wq_ref, CONV_B)
        qc = qc * jax.nn.sigmoid(qc)
        cq_scr[b] = x[span - SUBLANES:span]
        uu = u_ref[rows, :]
        ca = _causal_conv(uu, cu_scr[b], wa_ref, CONV_A)
        cu_scr[b] = uu[span - SUBLANES:span]
        ycat_ref[rows, 0:D_A] = (bg_ref[rows, :] * ca).astype(BF16)
        for j in range(cps):
            r = slice(j * chunk, (j + 1) * chunk)
            for h in range(H_B):
                qh = qc[r, h * DK:(h + 1) * DK]
                kh = qc[r, H_B * DK + h * DK:H_B * DK + (h + 1) * DK]
                vh = qc[r, 2 * H_B * DK + h * DV:2 * H_B * DK + (h + 1) * DV]
                key = (b * cps + j, h)
                qs[key] = qh * (lax.rsqrt(jnp.sum(qh * qh, axis=-1, keepdims=True) + RMS_EPS) * (DK ** -0.5))
                ks[key] = kh * lax.rsqrt(jnp.sum(kh * kh, axis=-1, keepdims=True) + RMS_EPS)
                vs[key] = vh

    for gi in range(n_groups):
        blocks = [divmod(gi * gsz + t, H_B) for t in range(gsz)]
        cat = lambda d: jnp.concatenate([d[uh] for uh in blocks], axis=0) if gsz > 1 else d[blocks[0]]
        qg, kg, vg = cat(qs), cat(ks), cat(vs)

        tile = gbr_ref[gi]
        gc = _lane_scan(tile, pos, chunk, False)
        rs = _lane_scan(tile, pos, chunk, True) - tile
        gc_row = gc[0:1]
        gc_m = _per_row(gc_row)
        rs_m = _per_row(rs[0:1])
        beta_m = _per_row(tile[1:2])
        diff = gc_m - jnp.broadcast_to(gc_row, (LANES, LANES))
        decay = jnp.where(m_incl, jnp.exp(jnp.where(m_incl, diff, 0.0)), 0.0)
        eg = jnp.exp(gc_m)
        etot = jnp.exp(gc_m + rs_m)

        kb = kg * beta_m
        kgb = kg.astype(BF16)
        a_mat = lax.dot_general(kb.astype(BF16), kgb, (((1,), (1,)), ((), ())), preferred_element_type=F32)
        lm = jnp.where(m_strict, a_mat * decay, 0.0)
        attn = lax.dot_general(qg.astype(BF16), kgb, (((1,), (1,)), ((), ())), preferred_element_type=F32) * decay

        p = eye - lm
        m = _bdot(lm, lm)
        for lvl in range(1, levels):
            p = p + _bdot(p, m)
            if lvl < levels - 1:
                m = _bdot(m, m)
        uw = _bdot(p, jnp.concatenate([vg * beta_m, kb * eg], axis=1))
        u_all, w_all = uw[:, :DV], uw[:, DV:]
        qe = qg * eg
        kdec = kg * jnp.exp(rs_m)

        vnew, qsv = [], []
        for t, (unit, h) in enumerate(blocks):
            b = unit // cps
            r = slice(t * chunk, (t + 1) * chunk)
            s_bf = s_scr[b, h].astype(BF16)
            lhs = jnp.concatenate([w_all[r], qe[r]], axis=0).astype(BF16)
            both = jnp.dot(lhs, s_bf, preferred_element_type=F32)
            vnew.append(u_all[r] - both[:chunk])
            qsv.append(both[chunk:])
        vnew_g = jnp.concatenate(vnew, axis=0) if gsz > 1 else vnew[0]
        qs_g = jnp.concatenate(qsv, axis=0) if gsz > 1 else qsv[0]
        o = qs_g + _bdot(attn, vnew_g)
        o = o * lax.rsqrt(jnp.mean(o * o, axis=-1, keepdims=True) + RMS_EPS) * ng_ref[...]

        for t, (unit, h) in enumerate(blocks):
            b = unit // cps
            r = slice(t * chunk, (t + 1) * chunk)
            rows = slice(unit * chunk, (unit + 1) * chunk)
            zz = z_ref[rows, h * DV:(h + 1) * DV]
            ycat_ref[rows, D_A + h * DV:D_A + (h + 1) * DV] = (o[r] * (zz * jax.nn.sigmoid(zz))).astype(BF16)
            upd = lax.dot_general(kdec[r].astype(BF16), vnew[t].astype(BF16), (((0,), (0,)), ((), ())),
                                  preferred_element_type=F32)
            scale = jnp.broadcast_to(etot[t * chunk:t * chunk + 1, :], (DK, DV))
            s_scr[b, h] = s_scr[b, h] * scale + upd

    @pl.when(n == n_steps - 1)
    def _():
        nq_ref[...] = cq_scr[...]
        nu_ref[...] = cu_scr[...]
        sn_ref[...] = s_scr[...]


def _gdn_phased_body(chunk, cps, n_steps,
                     qkv_ref, u_ref, bg_ref, z_ref, gbr_ref, pq_ref, pu_ref, s0_ref, wq_ref, wa_ref, ng_ref, y_any,
                     ycat_ref, nq_ref, nu_ref, sn_ref, cq_scr, cu_scr, s_scr,
                     kf, qf, vf, k16, kb16, q16, kdec16, att16, m16, c16, vk16, uw16,
                     dec, pm, qe, etot, dm, om, n16, bm):
    del y_any
    n = pl.program_id(1)

    @pl.when(n == 0)
    def _():
        cq_scr[...] = pq_ref[...]
        cu_scr[...] = pu_ref[...]
        s_scr[...] = s0_ref[...]

    gsz = LANES // chunk
    n_groups = cps * H_B // gsz
    levels = int(math.log2(chunk))
    span = cps * chunk
    groups = range(n_groups)
    blocks_of = lambda gi: [divmod(gi * gsz + t, H_B) for t in range(gsz)]
    rows_of = lambda t: slice(t * chunk, (t + 1) * chunk)

    ii = lax.broadcasted_iota(I32, (LANES, LANES), 0)
    jj = lax.broadcasted_iota(I32, (LANES, LANES), 1)
    same = (ii // chunk) == (jj // chunk)
    m_incl = same & (ii >= jj)
    m_strict = same & (ii > jj)
    eye = (ii == jj).astype(F32)
    pos = lax.broadcasted_iota(I32, (SUBLANES, LANES), 1) % chunk
    nt = (((1,), (1,)), ((), ()))
    tn = (((0,), (0,)), ((), ()))

    x = qkv_ref[...]
    qc = _causal_conv(x, cq_scr[0], wq_ref, CONV_B)
    qc = qc * jax.nn.sigmoid(qc)
    cq_scr[0] = x[span - SUBLANES:span]
    uu = u_ref[...]
    ca = _causal_conv(uu, cu_scr[0], wa_ref, CONV_A)
    cu_scr[0] = uu[span - SUBLANES:span]
    ycat_ref[:, 0:D_A] = (bg_ref[...] * ca).astype(BF16)
    for gi in groups:
        for t, (j, h) in enumerate(blocks_of(gi)):
            r = rows_of(j)
            qh = qc[r, h * DK:(h + 1) * DK]
            kh = qc[r, H_B * DK + h * DK:H_B * DK + (h + 1) * DK]
            qf[gi, rows_of(t), :] = qh * (lax.rsqrt(jnp.sum(qh * qh, axis=-1, keepdims=True) + RMS_EPS)
                                          * (DK ** -0.5))
            kf[gi, rows_of(t), :] = kh * lax.rsqrt(jnp.sum(kh * kh, axis=-1, keepdims=True) + RMS_EPS)
            vf[gi, rows_of(t), :] = qc[r, 2 * H_B * DK + h * DV:2 * H_B * DK + (h + 1) * DV]

    for gi in groups:
        tile = gbr_ref[gi]
        gc = _lane_scan(tile, pos, chunk, False)
        rs = _lane_scan(tile, pos, chunk, True) - tile
        gc_row = gc[0:1]
        gc_m = _per_row(gc_row)
        rs_m = _per_row(rs[0:1])
        beta_m = _per_row(tile[1:2])
        diff = gc_m - jnp.broadcast_to(gc_row, (LANES, LANES))
        dec[gi] = jnp.where(m_incl, jnp.exp(jnp.where(m_incl, diff, 0.0)), 0.0)
        eg = jnp.exp(gc_m)
        etot[gi] = jnp.exp(gc_m + rs_m)
        kg = kf[gi]
        kb = kg * beta_m
        k16[gi] = kg.astype(BF16)
        kb16[gi] = kb.astype(BF16)
        q16[gi] = qf[gi].astype(BF16)
        qe[gi] = qf[gi] * eg
        kdec16[gi] = (kg * jnp.exp(rs_m)).astype(BF16)
        vk16[gi, :, 0:DV] = (vf[gi] * beta_m).astype(BF16)
        vk16[gi, :, DV:] = (kb * eg).astype(BF16)

    for gi in groups:
        a_mat = lax.dot_general(kb16[gi], k16[gi], nt, preferred_element_type=F32)
        lm = jnp.where(m_strict, a_mat * dec[gi], 0.0)
        pm[gi] = eye - lm
        lm16 = lm.astype(BF16)
        m16[gi] = jnp.dot(lm16, lm16, preferred_element_type=F32).astype(BF16)
        att16[gi] = (lax.dot_general(q16[gi], k16[gi], nt, preferred_element_type=F32) * dec[gi]).astype(BF16)

    for lvl in range(1, levels):
        for gi in groups:
            pm[gi] = pm[gi] + jnp.dot(pm[gi].astype(BF16), m16[gi], preferred_element_type=F32)
        if lvl < levels - 1:
            for gi in groups:
                m16[gi] = jnp.dot(m16[gi], m16[gi], preferred_element_type=F32).astype(BF16)

    for gi in groups:
        uw16[gi] = jnp.dot(pm[gi].astype(BF16), vk16[gi], preferred_element_type=F32).astype(BF16)
    for gi in groups:
        au = jnp.dot(att16[gi], uw16[gi], preferred_element_type=F32)
        dm[gi] = au[:, :DV]
        c16[gi] = (qe[gi] - au[:, DV:]).astype(BF16)
        for t in range(gsz):
            r = rows_of(t)
            nb = lax.dot_general(kdec16[gi, r, :], uw16[gi, r, :], tn, preferred_element_type=F32)
            bm[gi * gsz + t] = nb[:, :DV]
            n16[gi * gsz + t] = nb[:, DV:].astype(BF16)

    for gi in groups:
        for t, (j, h) in enumerate(blocks_of(gi)):
            r = rows_of(t)
            s_old = s_scr[0, h]
            lhs = jnp.concatenate([c16[gi, r, :], n16[gi * gsz + t]], axis=0)
            both = jnp.dot(lhs, s_old.astype(BF16), preferred_element_type=F32)
            om[gi, r, :] = both[:chunk] + dm[gi, r, :]
            scale = jnp.broadcast_to(etot[gi, t * chunk:t * chunk + 1, :], (DK, DV))
            s_scr[0, h] = s_old * scale - both[chunk:] + bm[gi * gsz + t]

    for gi in groups:
        o = om[gi]
        o = o * lax.rsqrt(jnp.mean(o * o, axis=-1, keepdims=True) + RMS_EPS) * ng_ref[...]
        for t, (j, h) in enumerate(blocks_of(gi)):
            zz = z_ref[rows_of(j), h * DV:(h + 1) * DV]
            ycat_ref[rows_of(j), D_A + h * DV:D_A + (h + 1) * DV] = (
                o[rows_of(t)] * (zz * jax.nn.sigmoid(zz))).astype(BF16)

    @pl.when(n == n_steps - 1)
    def _():
        nq_ref[...] = cq_scr[...]
        nu_ref[...] = cu_scr[...]
        sn_ref[...] = s_scr[...]


def _gdn_phased_scratch(chunk, cps):
    g = cps * H_B * chunk // LANES
    nblk = cps * H_B
    mat = lambda dt, n=g, w=LANES: pltpu.VMEM((n, LANES, w), dt)
    return ([mat(F32)] * 3 + [mat(BF16)] * 7 + [mat(BF16, w=2 * LANES)] * 2 + [mat(F32)] * 6
            + [mat(BF16, n=nblk), mat(F32, n=nblk)])


def _gdn(qkv, u, bg, z, gbr, prev_q, prev_u, s0, wq, wa, ng, ycat, *, chunk, bb, cps, n_seq, t_len, row_block0,
         phased=False):
    rb = bb * cps * chunk
    gs = rb * H_B // LANES
    n_steps = t_len // (cps * chunk)
    rowmap = lambda i, n: (row_block0 + i * n_steps + n, 0)
    row = lambda w: pl.BlockSpec((rb, w), rowmap)
    seq3 = lambda w: pl.BlockSpec((bb, SUBLANES, w), lambda i, n: (i, 0, 0))
    full = lambda a: pl.BlockSpec(a.shape, lambda i, n: (0,) * a.ndim)
    st = pl.BlockSpec((bb, H_B, DK, DV), lambda i, n: (i, 0, 0, 0))
    if phased:
        assert bb == 1
        body = functools.partial(_gdn_phased_body, chunk, cps, n_steps)
        extra_scratch = _gdn_phased_scratch(chunk, cps)
    else:
        body = functools.partial(_gdn_body, chunk, bb, cps, n_steps)
        extra_scratch = []
    return pl.pallas_call(
        body,
        grid=(n_seq // bb, n_steps),
        in_specs=[row(D_QKV), row(D_A), row(D_A), row(D_A),
                  pl.BlockSpec((None, gs, SUBLANES, LANES), lambda i, n: (i * n_steps + n, 0, 0, 0)),
                  seq3(D_QKV), seq3(D_A), st, full(wq), full(wa), full(ng),
                  pl.BlockSpec(memory_space=pl.ANY)],
        out_specs=[pl.BlockSpec((rb, D_MODEL), rowmap), seq3(D_QKV), seq3(D_A), st],
        out_shape=[jax.ShapeDtypeStruct(ycat.shape, BF16),
                   jax.ShapeDtypeStruct((n_seq, SUBLANES, D_QKV), F32),
                   jax.ShapeDtypeStruct((n_seq, SUBLANES, D_A), F32),
                   jax.ShapeDtypeStruct((n_seq, H_B, DK, DV), F32)],
        scratch_shapes=[pltpu.VMEM((bb, SUBLANES, D_QKV), F32), pltpu.VMEM((bb, SUBLANES, D_A), F32),
                        pltpu.VMEM((bb, H_B, DK, DV), F32)] + extra_scratch,
        input_output_aliases={11: 0},
        compiler_params=_cparams(("arbitrary", "arbitrary")),
        name=f"gdn_c{chunk}",
    )(qkv, u, bg, z, gbr, prev_q, prev_u, s0, wq, wa, ng, ycat)


def _group_rows(gb, n_seq, t_len, chunk, bb, cps):
    n_steps = t_len // (cps * chunk)
    gs = bb * cps * chunk * H_B // LANES
    g = gb[:, :2 * H_B].reshape(n_seq // bb, bb, n_steps, cps, chunk, 2, H_B)
    g = jnp.transpose(g, (0, 2, 5, 1, 3, 6, 4))
    g = g.reshape(n_seq // bb * n_steps, 2, gs, LANES)
    g = jnp.transpose(g, (0, 2, 1, 3))
    return jnp.pad(g, ((0, 0), (0, 0), (0, SUBLANES - 2), (0, 0)))


def _route(x_new, wr_ref, br_ref, tri_ref, cnt_scr, gate_ref, dest_ref, counts_ref):
    @pl.when(pl.program_id(0) == 0)
    def _():
        cnt_scr[...] = jnp.zeros(cnt_scr.shape, F32)

    xh = x_new.astype(BF16)
    xl = (x_new - xh.astype(F32)).astype(BF16)
    p = jnp.dot(xh, wr_ref[...], preferred_element_type=F32)
    logits = (p[:, :LANES] + p[:, LANES:] + jnp.dot(xl, wr_ref[:, :LANES], preferred_element_type=F32)
              + br_ref[...])
    lane = lax.broadcasted_iota(I32, logits.shape, 1).astype(F32)
    neg = jnp.float32(-jnp.inf)
    l = jnp.where(lane < N_EXP, logits, neg)
    val_out = jnp.full(logits.shape, neg, F32)
    sels, hots = [], []
    for k in range(TOP_K):
        m = jnp.max(l, axis=-1, keepdims=True)
        sel = jnp.min(jnp.where(l == m, lane, float(LANES)), axis=-1, keepdims=True)
        hit = lane == sel
        val_out = jnp.where(lane == k, m, val_out)
        l = jnp.where(hit, neg, l)
        sels.append(sel)
        hots.append(hit.astype(F32))
    e = jnp.exp(val_out - jnp.max(val_out, axis=-1, keepdims=True))
    gate_ref[...] = e / jnp.sum(e, axis=-1, keepdims=True)

    hot = hots[0] + hots[1] + hots[2] + hots[3]
    before = jnp.dot(tri_ref[...], hot.astype(BF16), preferred_element_type=F32) + cnt_scr[...]
    dest = jnp.zeros(logits.shape, F32)
    for k in range(TOP_K):
        rank = jnp.sum(hots[k] * before, axis=-1, keepdims=True)
        dest = jnp.where(lane == k, sels[k] * float(1 << RANK_BITS) + rank, dest)
    dest_ref[...] = dest.astype(I32)
    cnt_scr[...] = cnt_scr[...] + jnp.sum(hot, axis=0, keepdims=True)
    counts_ref[...] = cnt_scr[...]


_ROUTE_OUT_SPECS = [pl.BlockSpec((TM, LANES), lambda i: (i, 0)), pl.BlockSpec((TM, LANES), lambda i: (i, 0)),
                    pl.BlockSpec((1, LANES), lambda i: (0, 0))]


def _route_out_shapes(n):
    return [jax.ShapeDtypeStruct((n, LANES), F32), jax.ShapeDtypeStruct((n, LANES), I32),
            jax.ShapeDtypeStruct((1, LANES), F32)]


def _mm_res_ln_body(y_ref, xp_ref, xs_ref, w_ref, g_ref, b_ref, wr_ref, br_ref, tri_ref,
                    o_ref, gate_ref, dest_ref, counts_ref, cnt_scr):
    h = jnp.dot(y_ref[...].astype(BF16), w_ref[...], preferred_element_type=F32)
    xn = _layer_norm(DEEPNORM_ALPHA * _token_tile(xp_ref, xs_ref) + h, g_ref[...], b_ref[...])
    o_ref[...] = xn
    _route(xn, wr_ref, br_ref, tri_ref, cnt_scr, gate_ref, dest_ref, counts_ref)


def _mm_res_ln(y, xp, xs, w, g, b, wr, br, tri):
    n = xp.shape[0] + xs.shape[0]
    row = pl.BlockSpec((TM, D_MODEL), lambda i: (i, 0))
    full = lambda a: pl.BlockSpec(a.shape, lambda i: (0,) * a.ndim)
    return pl.pallas_call(
        _mm_res_ln_body,
        grid=(n // TM,),
        in_specs=[row] + _token_specs() + [full(w), full(g), full(b), full(wr), full(br), full(tri)],
        out_specs=[row] + _ROUTE_OUT_SPECS,
        out_shape=[jax.ShapeDtypeStruct((n, D_MODEL), F32)] + _route_out_shapes(n),
        scratch_shapes=[pltpu.VMEM((1, LANES), F32)],
        compiler_params=_cparams(("arbitrary",)),
        name="mm_res_ln",
    )(y, xp, xs, w, g, b, wr, br, tri)


def _gelu(x):
    return 0.5 * x * (1.0 + lax.erf(x * (2.0 ** -0.5)))


def _c_out_body(u_ref, v_ref, x_ref, wmix_ref, mask_ref, bias_ref, wout_ref, g_ref, b_ref, wr_ref, br_ref, tri_ref,
                o_ref, gate_ref, dest_ref, counts_ref, us_scr, cnt_scr):
    mix = [(wmix_ref[gi] * mask_ref[...]).astype(BF16) for gi in range(C_GROUPS)]
    for t in range(TM // C_CHUNK):
        rows = slice(t * C_CHUNK, (t + 1) * C_CHUNK)
        for gi in range(C_GROUPS):
            cols = slice(gi * LANES, (gi + 1) * LANES)
            s = jnp.dot(mix[gi], v_ref[rows, cols].astype(BF16), preferred_element_type=F32) + bias_ref[:, cols]
            us_scr[rows, cols] = (u_ref[rows, cols] * s).astype(BF16)
    h = jnp.dot(us_scr[...], wout_ref[...], preferred_element_type=F32)
    xn = _layer_norm(DEEPNORM_ALPHA * x_ref[...] + h, g_ref[...], b_ref[...])
    o_ref[...] = xn
    _route(xn, wr_ref, br_ref, tri_ref, cnt_scr, gate_ref, dest_ref, counts_ref)


def _c_out(u, v, x, wmix2, mask2, bias2, wout, g, b, wr, br, tri):
    n = x.shape[0]
    first_sample_step = N_PROMPT // TM
    sel = lambda i: jnp.where(i >= first_sample_step, 1, 0)
    row = pl.BlockSpec((TM, D_MODEL), lambda i: (i, 0))
    full = lambda a: pl.BlockSpec(a.shape, lambda i: (0,) * a.ndim)
    return pl.pallas_call(
        _c_out_body,
        grid=(n // TM,),
        in_specs=[row, row, row,
                  pl.BlockSpec((None, C_GROUPS, C_CHUNK, C_CHUNK), lambda i: (sel(i), 0, 0, 0)),
                  pl.BlockSpec((None, C_CHUNK, C_CHUNK), lambda i: (sel(i), 0, 0)),
                  pl.BlockSpec((None, C_CHUNK, D_C), lambda i: (sel(i), 0, 0)),
                  full(wout), full(g), full(b), full(wr), full(br), full(tri)],
        out_specs=[row] + _ROUTE_OUT_SPECS,
        out_shape=[jax.ShapeDtypeStruct((n, D_MODEL), F32)] + _route_out_shapes(n),
        scratch_shapes=[pltpu.VMEM((TM, D_C), BF16), pltpu.VMEM((1, LANES), F32)],
        compiler_params=_cparams(("arbitrary",)),
        name="c_out",
    )(u, v, x, wmix2, mask2, bias2, wout, g, b, wr, br, tri)


def _for_each_row(n_rows, fn):
    def group(t8, c):
        base = pl.multiple_of(t8 * SUBLANES, SUBLANES)
        for s in range(SUBLANES):
            fn(t8, base, s)
        return c
    lax.fori_loop(0, n_rows // SUBLANES, group, 0)


def _dispatch_body(pad_start_ref, pad_n_ref, n_used_ref, dest_ref, x_ref, rows_hbm, zbuf, sem, zsem):
    i = pl.program_id(0)

    @pl.when(i == 0)
    def _():
        zbuf[...] = jnp.zeros(zbuf.shape, F32)

    @pl.when(i < N_EXP)
    def _():
        n = pad_n_ref[i]
        start = pad_start_ref[i]
        odd = n & (SUBLANES - 1)
        copies = [(pltpu.make_async_copy(zbuf.at[pl.ds(0, 1)], rows_hbm.at[pl.ds(start + s, 1)], zsem), s < odd)
                  for s in range(SUBLANES - 1)]
        off = start + odd
        for size in [1 << p for p in reversed(range(3, int(math.log2(TM_E))))]:
            dst = rows_hbm.at[pl.ds(pl.multiple_of(off, SUBLANES), size)]
            copies.append((pltpu.make_async_copy(zbuf.at[pl.ds(0, size)], dst, zsem), (n & size) != 0))
            off = off + (n & size)
        for cp, used in copies:
            pl.when(used)(cp.start)
        for cp, used in copies:
            pl.when(used)(cp.wait)

    blk = n_used_ref[0] + (i - N_EXP)

    @pl.when(jnp.logical_and(i >= N_EXP, blk < N_EBLOCKS))
    def _():
        cp = pltpu.make_async_copy(zbuf, rows_hbm.at[pl.ds(pl.multiple_of(blk * TM_E, TM_E), TM_E)], zsem)
        cp.start()
        cp.wait()

    def push(t8, base, s):
        src = x_ref.at[pl.ds(base, SUBLANES)].at[pl.ds(s, 1)]
        for k in range(TOP_K):
            d = dest_ref[0, 0, (t8 * SUBLANES + s) * TOP_K + k]
            pltpu.make_async_copy(src, rows_hbm.at[pl.ds(d, 1)], sem).start(priority=k % 2)
    _for_each_row(TM_D, push)

    def drain(t8, base, s):
        for k in range(TOP_K):
            pltpu.make_async_copy(x_ref.at[pl.ds(0, 1)], rows_hbm.at[pl.ds(0, 1)], sem).wait()
    _for_each_row(TM_D, drain)


def _dispatch(pads, dest3, x):
    n = x.shape[0]
    assert n // TM_D >= N_EXP + (N_EBLOCKS - N_PAIRS // TM_E)
    return pl.pallas_call(
        _dispatch_body,
        grid_spec=pltpu.PrefetchScalarGridSpec(
            num_scalar_prefetch=3,
            grid=(n // TM_D,),
            in_specs=[pl.BlockSpec((1, 1, TM_D * TOP_K), lambda i, *_: (i, 0, 0), memory_space=pltpu.SMEM),
                      pl.BlockSpec((TM_D, D_MODEL), lambda i, *_: (i, 0))],
            out_specs=pl.BlockSpec(memory_space=pl.ANY),
            scratch_shapes=[pltpu.VMEM((TM_E, D_MODEL), F32), pltpu.SemaphoreType.DMA(()),
                            pltpu.SemaphoreType.DMA(())]),
        out_shape=jax.ShapeDtypeStruct((ROWS_TOTAL, D_MODEL), F32),
        compiler_params=_cparams(("arbitrary",)),
        name="moe_dispatch",
    )(*pads, dest3, x)


def _experts_body(layer, bexp_ref, bval_ref, bnext_ref, x_ref, w1_hbm, b1_ref, w2_hbm, b2_ref, o_ref,
                  w1s, w2s, slot_ref, sem):
    i = pl.program_id(0)
    valid = bval_ref[i] != 0
    fresh = jnp.logical_or(i == 0, bexp_ref[i] != bexp_ref[jnp.maximum(i - 1, 0)])

    def weight_copies(e, slot):
        return (pltpu.make_async_copy(w1_hbm.at[layer, e], w1s.at[slot], sem.at[slot, 0]),
                pltpu.make_async_copy(w2_hbm.at[layer, e], w2s.at[slot], sem.at[slot, 1]))

    @pl.when(jnp.logical_and(valid, fresh))
    def _():
        @pl.when(i == 0)
        def _():
            slot_ref[0] = 1
            for cp in weight_copies(bexp_ref[i], 0):
                cp.start()
        slot = 1 - slot_ref[0]
        slot_ref[0] = slot
        for cp in weight_copies(bexp_ref[i], slot):
            cp.wait()

        @pl.when(bnext_ref[i] >= 0)
        def _():
            for cp in weight_copies(bnext_ref[i], 1 - slot):
                cp.start()

    @pl.when(jnp.logical_not(valid))
    def _():
        o_ref[...] = jnp.zeros(o_ref.shape, F32)

    @pl.when(valid)
    def _():
        slot = slot_ref[0]
        x = x_ref[...]
        glu = jnp.dot(x, w1s[slot, :, :D_EXP], preferred_element_type=F32) + b1_ref[:, :D_EXP]
        lin = jnp.dot(x, w1s[slot, :, D_EXP:], preferred_element_type=F32) + b1_ref[:, D_EXP:]
        glu = jnp.minimum(glu, SWIGLU_LIMIT)
        lin = jnp.clip(lin, -SWIGLU_LIMIT, SWIGLU_LIMIT)
        act = glu * jax.nn.sigmoid(SWIGLU_ALPHA * glu) * (lin + 1.0)
        o_ref[...] = jnp.dot(act, w2s[slot], preferred_element_type=F32) + b2_ref[...]


def _experts(layer, tables, x_rows, w1, b1, w2, b2):
    bspec = lambda w: pl.BlockSpec((None, None, 1, w), lambda i, be, bv, bn: (layer, be[i], 0, 0))
    rows = pl.BlockSpec((TM_E, D_MODEL), lambda i, be, bv, bn: (i, 0))
    hbm = pl.BlockSpec(memory_space=pl.ANY)
    return pl.pallas_call(
        functools.partial(_experts_body, layer),
        grid_spec=pltpu.PrefetchScalarGridSpec(
            num_scalar_prefetch=3,
            grid=(N_EBLOCKS,),
            in_specs=[rows, hbm, bspec(2 * D_EXP), hbm, bspec(D_MODEL)],
            out_specs=rows,
            scratch_shapes=[pltpu.VMEM((2, D_MODEL, 2 * D_EXP), F32), pltpu.VMEM((2, D_EXP, D_MODEL), F32),
                            pltpu.SMEM((1,), I32), pltpu.SemaphoreType.DMA((2, 2))]),
        out_shape=jax.ShapeDtypeStruct((ROWS_TOTAL, D_MODEL), F32),
        compiler_params=_cparams(("arbitrary",)),
        name="moe_experts",
    )(*tables, x_rows, w1, b1, w2, b2)


def _combine_ln_body(tail, n_tiles, dest_ref, dest1_ref, dest2_ref, gate_ref, x_ref, rows_hbm, g_ref, b_ref, *refs):
    if tail == "c_in":
        w_ref, bi_ref, lg_ref, lb_ref, o_ref, u_ref, v_ref, buf, sem = refs
    else:
        o_p_ref, o_s_ref, buf, sem = refs
    i = pl.program_id(0)

    def gather(d_ref, slot):
        def pull(t8, base, s):
            for k in range(TOP_K):
                d = d_ref[0, 0, (t8 * SUBLANES + s) * TOP_K + k]
                dst = buf.at[slot, k].at[pl.ds(base, SUBLANES)].at[pl.ds(s, 1)]
                pltpu.make_async_copy(rows_hbm.at[pl.ds(d, 1)], dst, sem.at[slot]).start(priority=k % 2)
        _for_each_row(TM_C, pull)

    def drain(slot):
        def one(t8, base, s):
            for k in range(TOP_K):
                pltpu.make_async_copy(rows_hbm.at[pl.ds(0, 1)], buf.at[0, 0].at[pl.ds(0, 1)], sem.at[slot]).wait()
        _for_each_row(TM_C, one)

    @pl.when(i == 0)
    def _():
        gather(dest_ref, 0)
        gather(dest1_ref, 1)

    slot = i % 3
    drain(slot)

    gates = gate_ref[...]
    y = buf[slot, 0] * gates[:, 0:1]
    for k in range(1, TOP_K):
        y = y + buf[slot, k] * gates[:, k:k + 1]
    res = _layer_norm(DEEPNORM_ALPHA * x_ref[...] + y, g_ref[...], b_ref[...])
    if tail == "c_in":
        o_ref[...] = res
        xb = res.astype(BF16)
        u_ref[...] = _gelu(jnp.dot(xb, w_ref[:, :D_C], preferred_element_type=F32) + bi_ref[:, :D_C])
        v = _gelu(jnp.dot(xb, w_ref[:, D_C:], preferred_element_type=F32) + bi_ref[:, D_C:])
        v_ref[...] = _layer_norm(v, lg_ref[...], lb_ref[...])

    nxt = (i + 2) % 3
    for t in range(TM_C):
        for k in range(TOP_K):
            src = rows_hbm.at[pl.ds(dest2_ref[0, 0, t * TOP_K + k], 1)]
            pltpu.make_async_copy(src, buf.at[nxt, k].at[pl.ds(t, 1)], sem.at[nxt]).start(priority=k % 2)

    if tail != "c_in":
        is_prompt = i < N_PROMPT // TM_C

        @pl.when(is_prompt)
        def _():
            o_p_ref[...] = res

        @pl.when(jnp.logical_not(is_prompt))
        def _():
            o_s_ref[...] = res

    @pl.when(i == n_tiles - 1)
    def _():
        drain((i + 1) % 3)
        drain(nxt)


def _combine_ln(dest3, gates, x, out_rows, g, b, c_in=None):
    n = x.shape[0]
    n_tiles = n // TM_C
    full = lambda a: pl.BlockSpec(a.shape, lambda i: (0,) * a.ndim)
    row = pl.BlockSpec((TM_C, D_MODEL), lambda i: (i, 0))
    dest_spec = lambda f: pl.BlockSpec((1, 1, TM_C * TOP_K), lambda i: (f(i), 0, 0), memory_space=pltpu.SMEM)
    assert n_tiles >= 2
    in_specs = [dest_spec(lambda i: i), dest_spec(lambda i: jnp.minimum(i + 1, n_tiles - 1)),
                dest_spec(lambda i: jnp.minimum(i + 2, n_tiles - 1)),
                pl.BlockSpec((TM_C, LANES), lambda i: (i, 0)), row, pl.BlockSpec(memory_space=pl.ANY), full(g), full(b)]
    args = [dest3, dest3, dest3, gates, x, out_rows, g, b]
    if c_in is not None:
        in_specs += [full(a) for a in c_in]
        args += list(c_in)
        out_specs = [row, row, row]
        out_shape = [jax.ShapeDtypeStruct((n, D_MODEL), F32)] * 3
    else:
        per_seq = SEQ // TM_C
        last = N_PROMPT // TM_C - 1
        out_specs = [pl.BlockSpec((None, TM_C, D_MODEL),
                                  lambda i: (jnp.minimum(i, last) // per_seq, jnp.minimum(i, last) % per_seq, 0)),
                     pl.BlockSpec((TM_C, D_MODEL), lambda i: (jnp.maximum(i - last - 1, 0), 0))]
        out_shape = [jax.ShapeDtypeStruct((BATCH, SEQ, D_MODEL), F32),
                     jax.ShapeDtypeStruct((N_SAMPLE, D_MODEL), F32)]
    return pl.pallas_call(
        functools.partial(_combine_ln_body, "c_in" if c_in is not None else "final", n_tiles),
        grid=(n_tiles,),
        in_specs=in_specs,
        out_specs=out_specs,
        out_shape=out_shape,
        scratch_shapes=[pltpu.VMEM((3, TOP_K, TM_C, D_MODEL), F32), pltpu.SemaphoreType.DMA((3,))],
        compiler_params=_cparams(("arbitrary",)),
        name="moe_combine_ln",
    )(*args)


def _positions(enc, counts):
    nb = (counts + TM_E - 1) // TM_E
    cum = jnp.cumsum(nb)
    first_blk = cum - nb
    experts = jnp.arange(N_EXP, dtype=I32)
    pair_hot = ((enc >> RANK_BITS)[:, :, None] == experts[None, None, :]).astype(I32)
    dest = jnp.sum(pair_hot * (first_blk * TM_E)[None, None, :], axis=2) + (enc & ((1 << RANK_BITS) - 1))
    n_used = cum[-1]
    blk = jnp.arange(N_EBLOCKS, dtype=I32)
    exp = jnp.minimum(jnp.sum((cum[None, :] <= blk[:, None]).astype(I32), axis=1), N_EXP - 1)
    later = lax.cummin(jnp.where(counts > 0, experts, N_EXP)[::-1])[::-1]
    nxt = jnp.concatenate([later[1:], jnp.full((1,), N_EXP, I32)])
    nxt = jnp.where(nxt >= N_EXP, -1, nxt)
    bnext = jnp.sum((exp[:, None] == experts[None, :]).astype(I32) * nxt[None, :], axis=1)
    pads = (first_blk * TM_E + counts, nb * TM_E - counts, n_used.reshape(1))
    return dest, (exp, (blk < n_used).astype(I32), bnext.astype(I32)), pads


def _moe_post_norm(layer, x, gates, enc, counts, w1, b1, w2, b2, g, b, c_in=None):
    dest4, tables, pads = _positions(enc[:, :TOP_K], counts[0, :N_EXP].astype(I32))
    x_rows = _dispatch(pads, dest4.reshape(N_TOK // TM_D, 1, TM_D * TOP_K), x)
    out_rows = _experts(layer, tables, x_rows, w1, b1.reshape(b1.shape[:2] + (1, 2 * D_EXP)),
                        w2, b2.reshape(b2.shape[:2] + (1, D_MODEL)))
    return _combine_ln(dest4.reshape(N_TOK // TM_C, 1, TM_C * TOP_K), gates, x, out_rows,
                       g.reshape(1, D_MODEL), b.reshape(1, D_MODEL), c_in=c_in)


def _router_weights(w_r, b_r):
    wh = w_r.astype(BF16)
    wl = (w_r - wh.astype(F32)).astype(BF16)
    pad = lambda a: jnp.pad(a, ((0, 0), (0, LANES - N_EXP)))
    return jnp.concatenate([pad(wh), pad(wl)], axis=1), jnp.pad(b_r, (0, LANES - N_EXP)).reshape(1, LANES)


def _tail8(state, keep):
    return jnp.pad(state, ((0, 0), (SUBLANES - keep, 0), (0, 0)))


def kernel(x_prompt, x_sample, state_conv_a, state_conv_qkv, state_delta, ab_w_in, ab_conv_a, ab_conv_qkv,
           ab_a_log, ab_dt_bias, ab_norm_g, ab_w_out, c_w_in, c_b_in, c_ln_g, c_ln_b, c_w_s, c_b_s, c_w_out,
           moe_w_router, moe_b_router, moe_w1, moe_b1, moe_w2, moe_b2, ln_g, ln_b):
    xp, xs = x_prompt.reshape(N_PROMPT, D_MODEL), x_sample.reshape(N_SAMPLE, D_MODEL)
    lnrow = lambda layer, j: (ln_g[layer, j].reshape(1, D_MODEL), ln_b[layer, j].reshape(1, D_MODEL))
    ri = jnp.arange(TM)
    tri = (ri[:, None] > ri[None, :]).astype(BF16)

    w_in = ab_w_in[0]
    w_main = w_in[:, :W_MAIN].astype(BF16)
    w_ab = jnp.pad(w_in[:, W_MAIN:], ((0, 0), (0, LANES - 2 * H_B))).astype(BF16)
    alog_row = jnp.pad(ab_a_log[0], (0, LANES - H_B)).reshape(1, LANES)
    dtb_row = jnp.pad(ab_dt_bias[0], (0, LANES - H_B)).reshape(1, LANES)
    bg, u, qkv, z, gb = _proj_ab(xp, xs, w_main, w_ab, alog_row, dtb_row)

    ng = ab_norm_g[0].reshape(1, DV)
    ycat = jnp.zeros((N_TOK, D_MODEL), BF16)
    gbr_p = _group_rows(gb[:N_PROMPT], BATCH, SEQ, DN_CHUNK, 1, GDN_CPS)
    ycat, pq8, pu8, p_delta = _gdn(
        qkv, u, bg, z, gbr_p,
        jnp.zeros((BATCH, SUBLANES, D_QKV), F32), jnp.zeros((BATCH, SUBLANES, D_A), F32),
        jnp.zeros((BATCH, H_B, DK, DV), F32), ab_conv_qkv[0], ab_conv_a[0], ng, ycat,
        chunk=DN_CHUNK, bb=1, cps=GDN_CPS, n_seq=BATCH, t_len=SEQ, row_block0=0, phased=True)
    bb_s = 16
    gbr_s = _group_rows(gb[N_PROMPT:], DEC_BATCH, DEC_SEQ, DEC_SEQ, bb_s, 1)
    ycat, sq8, su8, s_delta = _gdn(
        qkv, u, bg, z, gbr_s,
        _tail8(state_conv_qkv[0], CONV_B - 1), _tail8(state_conv_a[0], CONV_A - 1), state_delta[0],
        ab_conv_qkv[0], ab_conv_a[0], ng, ycat,
        chunk=DEC_SEQ, bb=bb_s, cps=1, n_seq=DEC_BATCH, t_len=DEC_SEQ, row_block0=N_PROMPT // (bb_s * DEC_SEQ))
    wr0, br0 = _router_weights(moe_w_router[0], moe_b_router[0])
    x, gates, dest, counts = _mm_res_ln(ycat, xp, xs, ab_w_out[0].astype(BF16), *lnrow(0, 0), wr0, br0, tri)
    c_in = (c_w_in[0].astype(BF16), c_b_in[0].reshape(1, 2 * D_C), c_ln_g[0].reshape(1, D_C), c_ln_b[0].reshape(1, D_C))
    x, uc, vc = _moe_post_norm(0, x, gates, dest, counts, moe_w1, moe_b1, moe_w2, moe_b2, ln_g[0, 1], ln_b[0, 1],
                               c_in=c_in)
    ws = c_w_s[0]
    reps = C_CHUNK // DEC_SEQ
    wmix2 = jnp.stack([ws, jnp.tile(ws[:, :DEC_SEQ, :DEC_SEQ], (1, reps, reps))])
    rc = jnp.arange(C_CHUNK)
    tril = rc[:, None] >= rc[None, :]
    mask2 = jnp.stack([tril, tril & ((rc[:, None] // DEC_SEQ) == (rc[None, :] // DEC_SEQ))]).astype(F32)
    bias_p = jnp.repeat(c_b_s[0].T, D_C // C_GROUPS, axis=1)
    bias2 = jnp.stack([bias_p, jnp.tile(bias_p[:DEC_SEQ], (reps, 1))])
    wr1, br1 = _router_weights(moe_w_router[1], moe_b_router[1])
    x, gates, dest, counts = _c_out(uc, vc, x, wmix2, mask2, bias2, c_w_out[0].astype(BF16), *lnrow(1, 0),
                                    wr1, br1, tri)
    y_prompt, y_sample = _moe_post_norm(1, x, gates, dest, counts, moe_w1, moe_b1, moe_w2, moe_b2,
                                        ln_g[1, 1], ln_b[1, 1])
    y_sample = y_sample.reshape(DEC_BATCH, DEC_SEQ, D_MODEL)
    ka, kq = CONV_A - 1, CONV_B - 1
    return (y_prompt, y_sample,
            pu8[None, :, SUBLANES - ka:], pq8[None, :, SUBLANES - kq:], p_delta[None],
            su8[None, :, SUBLANES - ka:], sq8[None, :, SUBLANES - kq:], s_delta[None],
            vc[N_PROMPT:].reshape(1, DEC_BATCH, DEC_SEQ, D_C))
```

```python
import functools
import math

import jax
import jax.numpy as jnp
from jax import lax
from jax.experimental import pallas as pl
from jax.experimental.pallas import tpu as pltpu

F32 = jnp.float32
BF16 = jnp.bfloat16
I32 = jnp.int32

D_MODEL = 1024
BATCH = 8
SEQ = 2048
DEC_BATCH = 128
DEC_SEQ = 8
N_PROMPT = BATCH * SEQ
N_SAMPLE = DEC_BATCH * DEC_SEQ
N_TOK = N_PROMPT + N_SAMPLE
D_A = 512
CONV_A = 3
H_B = 4
DK = 128
DV = 128
D_QKV = 1536
CONV_B = 4
DN_CHUNK = 64
W_MAIN = 3 * D_A + D_QKV + H_B * DV
D_C = 1024
C_GROUPS = 8
C_CHUNK = 128
N_EXP = 32
TOP_K = 4
D_EXP = 1024
SWIGLU_ALPHA = 1.702
SWIGLU_LIMIT = 7.0
DEEPNORM_ALPHA = 4.0 ** 0.25
LN_EPS = 1e-5
RMS_EPS = 1e-6

LANES = 128
SUBLANES = 8
VMEM_LIMIT = 56 * 1024 * 1024

TM = 512
TM_E = 512
TM_C = 256
TM_D = 256
GDN_CPS = 8
N_PAIRS = N_TOK * TOP_K
N_EBLOCKS = (N_PAIRS + N_EXP * (TM_E - 1)) // TM_E
ROWS_TOTAL = N_EBLOCKS * TM_E
RANK_BITS = 15
assert N_TOK <= 1 << RANK_BITS


def _cparams(sem):
    return pltpu.CompilerParams(dimension_semantics=sem, vmem_limit_bytes=VMEM_LIMIT)


def _layer_norm(t, g, b):
    mu = jnp.mean(t, axis=-1, keepdims=True)
    d = t - mu
    var = jnp.mean(d * d, axis=-1, keepdims=True)
    return d * lax.rsqrt(var + LN_EPS) * g + b


def _bdot(a, b):
    return jnp.dot(a.astype(BF16), b.astype(BF16), preferred_element_type=F32)


def _token_tile(xp_ref, xs_ref):
    return jnp.where(pl.program_id(0) < N_PROMPT // TM, xp_ref[...], xs_ref[...])


def _token_specs():
    n_p = N_PROMPT // TM
    return [pl.BlockSpec((TM, D_MODEL), lambda i: (jnp.minimum(i, n_p - 1), 0)),
            pl.BlockSpec((TM, D_MODEL), lambda i: (jnp.maximum(i - n_p, 0), 0))]


def _proj_ab_body(xp_ref, xs_ref, w_ref, wab_ref, alog_ref, dtb_ref, bg_ref, u_ref, qkv_ref, z_ref, gb_ref):
    xb = _token_tile(xp_ref, xs_ref).astype(BF16)

    def mm(lo, hi):
        return jnp.dot(xb, w_ref[:, lo:hi], preferred_element_type=F32)

    bg_ref[...] = mm(0, D_A)
    u_ref[...] = mm(D_A, 2 * D_A) * mm(2 * D_A, 3 * D_A)
    for c in range(D_QKV // 512):
        qkv_ref[:, c * 512:(c + 1) * 512] = mm(3 * D_A + c * 512, 3 * D_A + (c + 1) * 512)
    z_ref[...] = mm(3 * D_A + D_QKV, W_MAIN)
    ab = jnp.dot(xb, wab_ref[...], preferred_element_type=F32)
    g = -jnp.exp(alog_ref[...]) * jax.nn.softplus(ab + dtb_ref[...])
    beta = jax.nn.sigmoid(ab)
    lane = lax.broadcasted_iota(I32, ab.shape, 1)
    gb_ref[...] = jnp.where(lane < H_B, g, beta)


def _proj_ab(xp, xs, w_main, w_ab, alog_row, dtb_row):
    n = xp.shape[0] + xs.shape[0]
    row = lambda w: pl.BlockSpec((TM, w), lambda i: (i, 0))
    full = lambda a: pl.BlockSpec(a.shape, lambda i: (0,) * a.ndim)
    return pl.pallas_call(
        _proj_ab_body,
        grid=(n // TM,),
        in_specs=_token_specs() + [full(w_main), full(w_ab), full(alog_row), full(dtb_row)],
        out_specs=[row(D_A), row(D_A), row(D_QKV), row(D_A), row(LANES)],
        out_shape=[jax.ShapeDtypeStruct((n, D_A), F32), jax.ShapeDtypeStruct((n, D_A), F32),
                   jax.ShapeDtypeStruct((n, D_QKV), F32), jax.ShapeDtypeStruct((n, D_A), F32),
                   jax.ShapeDtypeStruct((n, LANES), F32)],
        compiler_params=_cparams(("arbitrary",)),
        name="proj_ab",
    )(xp, xs, w_main, w_ab, alog_row, dtb_row)


def _shift_rows(x, prev8, s):
    if s == 0:
        return x
    xr = pltpu.roll(x, s, axis=0)
    pr = pltpu.roll(prev8, s, axis=0)
    rid = lax.broadcasted_iota(I32, pr.shape, 0)
    head = jnp.where(rid < s, pr, xr[0:SUBLANES])
    if x.shape[0] == SUBLANES:
        return head
    return jnp.concatenate([head, xr[SUBLANES:]], axis=0)


def _causal_conv(x, prev8, w_ref, taps):
    y = x * w_ref[taps - 1:taps, :]
    for s in range(1, taps):
        y = y + _shift_rows(x, prev8, s) * w_ref[taps - 1 - s:taps - s, :]
    return y


def _lane_scan(x, pos, chunk, reverse):
    s = 1
    while s < chunk:
        if reverse:
            x = x + jnp.where(pos < chunk - s, pltpu.roll(x, LANES - s, axis=1), 0.0)
        else:
            x = x + jnp.where(pos >= s, pltpu.roll(x, s, axis=1), 0.0)
        s *= 2
    return x


def _per_row(row):
    return jnp.broadcast_to(row, (LANES, LANES)).T


def _gdn_body(chunk, bb, cps, n_steps,
              qkv_ref, u_ref, bg_ref, z_ref, gbr_ref, pq_ref, pu_ref, s0_ref, wq_ref, wa_ref, ng_ref, y_any,
              ycat_ref, nq_ref, nu_ref, sn_ref, cq_scr, cu_scr, s_scr):
    del y_any
    n = pl.program_id(1)

    @pl.when(n == 0)
    def _():
        cq_scr[...] = pq_ref[...]
        cu_scr[...] = pu_ref[...]
        s_scr[...] = s0_ref[...]

    gsz = LANES // chunk
    n_groups = bb * cps * H_B // gsz
    levels = int(math.log2(chunk))
    span = cps * chunk

    ii = lax.broadcasted_iota(I32, (LANES, LANES), 0)
    jj = lax.broadcasted_iota(I32, (LANES, LANES), 1)
    same = (ii // chunk) == (jj // chunk)
    m_incl = same & (ii >= jj)
    m_strict = same & (ii > jj)
    eye = (ii == jj).astype(F32)
    pos = lax.broadcasted_iota(I32, (SUBLANES, LANES), 1) % chunk

    qs, ks, vs = {}, {}, {}
    for b in range(bb):
        rows = slice(b * span, (b + 1) * span)
        x = qkv_ref[rows, :]
        qc = _causal_conv(x, cq_scr[b], wq_ref, CONV_B)
        qc = qc * jax.nn.sigmoid(qc)
        cq_scr[b] = x[span - SUBLANES:span]
        uu = u_ref[rows, :]
        ca = _causal_conv(uu, cu_scr[b], wa_ref, CONV_A)
        cu_scr[b] = uu[span - SUBLANES:span]
        ycat_ref[rows, 0:D_A] = (bg_ref[rows, :] * ca).astype(BF16)
        for j in range(cps):
            r = slice(j * chunk, (j + 1) * chunk)
            for h in range(H_B):
                qh = qc[r, h * DK:(h + 1) * DK]
                kh = qc[r, H_B * DK + h * DK:H_B * DK + (h + 1) * DK]
                vh = qc[r, 2 * H_B * DK + h * DV:2 * H_B * DK + (h + 1) * DV]
                key = (b * cps + j, h)
                qs[key] = qh * (lax.rsqrt(jnp.sum(qh * qh, axis=-1, keepdims=True) + RMS_EPS) * (DK ** -0.5))
                ks[key] = kh * lax.rsqrt(jnp.sum(kh * kh, axis=-1, keepdims=True) + RMS_EPS)
                vs[key] = vh

    for gi in range(n_groups):
        blocks = [divmod(gi * gsz + t, H_B) for t in range(gsz)]
        cat = lambda d: jnp.concatenate([d[uh] for uh in blocks], axis=0) if gsz > 1 else d[blocks[0]]
        qg, kg, vg = cat(qs), cat(ks), cat(vs)

        tile = gbr_ref[gi]
        gc = _lane_scan(tile, pos, chunk, False)
        rs = _lane_scan(tile, pos, chunk, True) - tile
        gc_row = gc[0:1]
        gc_m = _per_row(gc_row)
        rs_m = _per_row(rs[0:1])
        beta_m = _per_row(tile[1:2])
        diff = gc_m - jnp.broadcast_to(gc_row, (LANES, LANES))
        decay = jnp.where(m_incl, jnp.exp(jnp.where(m_incl, diff, 0.0)), 0.0)
        eg = jnp.exp(gc_m)
        etot = jnp.exp(gc_m + rs_m)

        kb = kg * beta_m
        kgb = kg.astype(BF16)
        a_mat = lax.dot_general(kb.astype(BF16), kgb, (((1,), (1,)), ((), ())), preferred_element_type=F32)
        lm = jnp.where(m_strict, a_mat * decay, 0.0)
        attn = lax.dot_general(qg.astype(BF16), kgb, (((1,), (1,)), ((), ())), preferred_element_type=F32) * decay

        p = eye - lm
        m = _bdot(lm, lm)
        for lvl in range(1, levels):
            p = p + _bdot(p, m)
            if lvl < levels - 1:
                m = _bdot(m, m)
        uw = _bdot(p, jnp.concatenate([vg * beta_m, kb * eg], axis=1))
        u_all, w_all = uw[:, :DV], uw[:, DV:]
        qe = qg * eg
        kdec = kg * jnp.exp(rs_m)

        vnew, qsv = [], []
        for t, (unit, h) in enumerate(blocks):
            b = unit // cps
            r = slice(t * chunk, (t + 1) * chunk)
            s_bf = s_scr[b, h].astype(BF16)
            lhs = jnp.concatenate([w_all[r], qe[r]], axis=0).astype(BF16)
            both = jnp.dot(lhs, s_bf, preferred_element_type=F32)
            vnew.append(u_all[r] - both[:chunk])
            qsv.append(both[chunk:])
        vnew_g = jnp.concatenate(vnew, axis=0) if gsz > 1 else vnew[0]
        qs_g = jnp.concatenate(qsv, axis=0) if gsz > 1 else qsv[0]
        o = qs_g + _bdot(attn, vnew_g)
        o = o * lax.rsqrt(jnp.mean(o * o, axis=-1, keepdims=True) + RMS_EPS) * ng_ref[...]

        for t, (unit, h) in enumerate(blocks):
            b = unit // cps
            r = slice(t * chunk, (t + 1) * chunk)
            rows = slice(unit * chunk, (unit + 1) * chunk)
            zz = z_ref[rows, h * DV:(h + 1) * DV]
            ycat_ref[rows, D_A + h * DV:D_A + (h + 1) * DV] = (o[r] * (zz * jax.nn.sigmoid(zz))).astype(BF16)
            upd = lax.dot_general(kdec[r].astype(BF16), vnew[t].astype(BF16), (((0,), (0,)), ((), ())),
                                  preferred_element_type=F32)
            scale = jnp.broadcast_to(etot[t * chunk:t * chunk + 1, :], (DK, DV))
            s_scr[b, h] = s_scr[b, h] * scale + upd

    @pl.when(n == n_steps - 1)
    def _():
        nq_ref[...] = cq_scr[...]
        nu_ref[...] = cu_scr[...]
        sn_ref[...] = s_scr[...]


def _gdn_phased_body(chunk, cps, n_steps,
                     qkv_ref, u_ref, bg_ref, z_ref, gbr_ref, pq_ref, pu_ref, s0_ref, wq_ref, wa_ref, ng_ref, y_any,
                     ycat_ref, nq_ref, nu_ref, sn_ref, cq_scr, cu_scr, s_scr,
                     kf, qf, vf, k16, kb16, q16, kdec16, att16, m16, c16, vk16, uw16,
                     dec, pm, qe, etot, dm, om, n16, bm):
    del y_any
    n = pl.program_id(1)

    @pl.when(n == 0)
    def _():
        cq_scr[...] = pq_ref[...]
        cu_scr[...] = pu_ref[...]
        s_scr[...] = s0_ref[...]

    gsz = LANES // chunk
    n_groups = cps * H_B // gsz
    levels = int(math.log2(chunk))
    span = cps * chunk
    groups = range(n_groups)
    blocks_of = lambda gi: [divmod(gi * gsz + t, H_B) for t in range(gsz)]
    rows_of = lambda t: slice(t * chunk, (t + 1) * chunk)

    ii = lax.broadcasted_iota(I32, (LANES, LANES), 0)
    jj = lax.broadcasted_iota(I32, (LANES, LANES), 1)
    same = (ii // chunk) == (jj // chunk)
    m_incl = same & (ii >= jj)
    m_strict = same & (ii > jj)
    eye = (ii == jj).astype(F32)
    pos = lax.broadcasted_iota(I32, (SUBLANES, LANES), 1) % chunk
    nt = (((1,), (1,)), ((), ()))
    tn = (((0,), (0,)), ((), ()))

    x = qkv_ref[...]
    qc = _causal_conv(x, cq_scr[0], wq_ref, CONV_B)
    qc = qc * jax.nn.sigmoid(qc)
    cq_scr[0] = x[span - SUBLANES:span]
    uu = u_ref[...]
    ca = _causal_conv(uu, cu_scr[0], wa_ref, CONV_A)
    cu_scr[0] = uu[span - SUBLANES:span]
    ycat_ref[:, 0:D_A] = (bg_ref[...] * ca).astype(BF16)
    for gi in groups:
        for t, (j, h) in enumerate(blocks_of(gi)):
            r = rows_of(j)
            qh = qc[r, h * DK:(h + 1) * DK]
            kh = qc[r, H_B * DK + h * DK:H_B * DK + (h + 1) * DK]
            qf[gi, rows_of(t), :] = qh * (lax.rsqrt(jnp.sum(qh * qh, axis=-1, keepdims=True) + RMS_EPS)
                                          * (DK ** -0.5))
            kf[gi, rows_of(t), :] = kh * lax.rsqrt(jnp.sum(kh * kh, axis=-1, keepdims=True) + RMS_EPS)
            vf[gi, rows_of(t), :] = qc[r, 2 * H_B * DK + h * DV:2 * H_B * DK + (h + 1) * DV]

    for gi in groups:
        tile = gbr_ref[gi]
        gc = _lane_scan(tile, pos, chunk, False)
        rs = _lane_scan(tile, pos, chunk, True) - tile
        gc_row = gc[0:1]
        gc_m = _per_row(gc_row)
        rs_m = _per_row(rs[0:1])
        beta_m = _per_row(tile[1:2])
        diff = gc_m - jnp.broadcast_to(gc_row, (LANES, LANES))
        dec[gi] = jnp.where(m_incl, jnp.exp(jnp.where(m_incl, diff, 0.0)), 0.0)
        eg = jnp.exp(gc_m)
        etot[gi] = jnp.exp(gc_m + rs_m)
        kg = kf[gi]
        kb = kg * beta_m
        k16[gi] = kg.astype(BF16)
        kb16[gi] = kb.astype(BF16)
        q16[gi] = qf[gi].astype(BF16)
        qe[gi] = qf[gi] * eg
        kdec16[gi] = (kg * jnp.exp(rs_m)).astype(BF16)
        vk16[gi, :, 0:DV] = (vf[gi] * beta_m).astype(BF16)
        vk16[gi, :, DV:] = (kb * eg).astype(BF16)

    for gi in groups:
        a_mat = lax.dot_general(kb16[gi], k16[gi], nt, preferred_element_type=F32)
        lm = jnp.where(m_strict, a_mat * dec[gi], 0.0)
        pm[gi] = eye - lm
        lm16 = lm.astype(BF16)
        m16[gi] = jnp.dot(lm16, lm16, preferred_element_type=F32).astype(BF16)
        att16[gi] = (lax.dot_general(q16[gi], k16[gi], nt, preferred_element_type=F32) * dec[gi]).astype(BF16)

    for lvl in range(1, levels):
        for gi in groups:
            pm[gi] = pm[gi] + jnp.dot(pm[gi].astype(BF16), m16[gi], preferred_element_type=F32)
        if lvl < levels - 1:
            for gi in groups:
                m16[gi] = jnp.dot(m16[gi], m16[gi], preferred_element_type=F32).astype(BF16)

    for gi in groups:
        uw16[gi] = jnp.dot(pm[gi].astype(BF16), vk16[gi], preferred_element_type=F32).astype(BF16)
    for gi in groups:
        au = jnp.dot(att16[gi], uw16[gi], preferred_element_type=F32)
        dm[gi] = au[:, :DV]
        c16[gi] = (qe[gi] - au[:, DV:]).astype(BF16)
        for t in range(gsz):
            r = rows_of(t)
            nb = lax.dot_general(kdec16[gi, r, :], uw16[gi, r, :], tn, preferred_element_type=F32)
            bm[gi * gsz + t] = nb[:, :DV]
            n16[gi * gsz + t] = nb[:, DV:].astype(BF16)

    for gi in groups:
        for t, (j, h) in enumerate(blocks_of(gi)):
            r = rows_of(t)
            s_old = s_scr[0, h]
            lhs = jnp.concatenate([c16[gi, r, :], n16[gi * gsz + t]], axis=0)
            both = jnp.dot(lhs, s_old.astype(BF16), preferred_element_type=F32)
            om[gi, r, :] = both[:chunk] + dm[gi, r, :]
            scale = jnp.broadcast_to(etot[gi, t * chunk:t * chunk + 1, :], (DK, DV))
            s_scr[0, h] = s_old * scale - both[chunk:] + bm[gi * gsz + t]

    for gi in groups:
        o = om[gi]
        o = o * lax.rsqrt(jnp.mean(o * o, axis=-1, keepdims=True) + RMS_EPS) * ng_ref[...]
        for t, (j, h) in enumerate(blocks_of(gi)):
            zz = z_ref[rows_of(j), h * DV:(h + 1) * DV]
            ycat_ref[rows_of(j), D_A + h * DV:D_A + (h + 1) * DV] = (
                o[rows_of(t)] * (zz * jax.nn.sigmoid(zz))).astype(BF16)

    @pl.when(n == n_steps - 1)
    def _():
        nq_ref[...] = cq_scr[...]
        nu_ref[...] = cu_scr[...]
        sn_ref[...] = s_scr[...]


def _gdn_phased_scratch(chunk, cps):
    g = cps * H_B * chunk // LANES
    nblk = cps * H_B
    mat = lambda dt, n=g, w=LANES: pltpu.VMEM((n, LANES, w), dt)
    return ([mat(F32)] * 3 + [mat(BF16)] * 7 + [mat(BF16, w=2 * LANES)] * 2 + [mat(F32)] * 6
            + [mat(BF16, n=nblk), mat(F32, n=nblk)])


def _gdn(qkv, u, bg, z, gbr, prev_q, prev_u, s0, wq, wa, ng, ycat, *, chunk, bb, cps, n_seq, t_len, row_block0,
         phased=False):
    rb = bb * cps * chunk
    gs = rb * H_B // LANES
    n_steps = t_len // (cps * chunk)
    rowmap = lambda i, n: (row_block0 + i * n_steps + n, 0)
    row = lambda w: pl.BlockSpec((rb, w), rowmap)
    seq3 = lambda w: pl.BlockSpec((bb, SUBLANES, w), lambda i, n: (i, 0, 0))
    full = lambda a: pl.BlockSpec(a.shape, lambda i, n: (0,) * a.ndim)
    st = pl.BlockSpec((bb, H_B, DK, DV), lambda i, n: (i, 0, 0, 0))
    if phased:
        assert bb == 1
        body = functools.partial(_gdn_phased_body, chunk, cps, n_steps)
        extra_scratch = _gdn_phased_scratch(chunk, cps)
    else:
        body = functools.partial(_gdn_body, chunk, bb, cps, n_steps)
        extra_scratch = []
    return pl.pallas_call(
        body,
        grid=(n_seq // bb, n_steps),
        in_specs=[row(D_QKV), row(D_A), row(D_A), row(D_A),
                  pl.BlockSpec((None, gs, SUBLANES, LANES), lambda i, n: (i * n_steps + n, 0, 0, 0)),
                  seq3(D_QKV), seq3(D_A), st, full(wq), full(wa), full(ng),
                  pl.BlockSpec(memory_space=pl.ANY)],
        out_specs=[pl.BlockSpec((rb, D_MODEL), rowmap), seq3(D_QKV), seq3(D_A), st],
        out_shape=[jax.ShapeDtypeStruct(ycat.shape, BF16),
                   jax.ShapeDtypeStruct((n_seq, SUBLANES, D_QKV), F32),
                   jax.ShapeDtypeStruct((n_seq, SUBLANES, D_A), F32),
                   jax.ShapeDtypeStruct((n_seq, H_B, DK, DV), F32)],
        scratch_shapes=[pltpu.VMEM((bb, SUBLANES, D_QKV), F32), pltpu.VMEM((bb, SUBLANES, D_A), F32),
                        pltpu.VMEM((bb, H_B, DK, DV), F32)] + extra_scratch,
        input_output_aliases={11: 0},
        compiler_params=_cparams(("arbitrary", "arbitrary")),
        name=f"gdn_c{chunk}",
    )(qkv, u, bg, z, gbr, prev_q, prev_u, s0, wq, wa, ng, ycat)


def _group_rows(gb, n_seq, t_len, chunk, bb, cps):
    n_steps = t_len // (cps * chunk)
    gs = bb * cps * chunk * H_B // LANES
    g = gb[:, :2 * H_B].reshape(n_seq // bb, bb, n_steps, cps, chunk, 2, H_B)
    g = jnp.transpose(g, (0, 2, 5, 1, 3, 6, 4))
    g = g.reshape(n_seq // bb * n_steps, 2, gs, LANES)
    g = jnp.transpose(g, (0, 2, 1, 3))
    return jnp.pad(g, ((0, 0), (0, 0), (0, SUBLANES - 2), (0, 0)))


def _route(x_new, wr_ref, br_ref, tri_ref, cnt_scr, gate_ref, dest_ref, counts_ref):
    @pl.when(pl.program_id(0) == 0)
    def _():
        cnt_scr[...] = jnp.zeros(cnt_scr.shape, F32)

    xh = x_new.astype(BF16)
    xl = (x_new - xh.astype(F32)).astype(BF16)
    p = jnp.dot(xh, wr_ref[...], preferred_element_type=F32)
    logits = (p[:, :LANES] + p[:, LANES:] + jnp.dot(xl, wr_ref[:, :LANES], preferred_element_type=F32)
              + br_ref[...])
    lane = lax.broadcasted_iota(I32, logits.shape, 1).astype(F32)
    neg = jnp.float32(-jnp.inf)
    l = jnp.where(lane < N_EXP, logits, neg)
    val_out = jnp.full(logits.shape, neg, F32)
    sels, hots = [], []
    for k in range(TOP_K):
        m = jnp.max(l, axis=-1, keepdims=True)
        sel = jnp.min(jnp.where(l == m, lane, float(LANES)), axis=-1, keepdims=True)
        hit = lane == sel
        val_out = jnp.where(lane == k, m, val_out)
        l = jnp.where(hit, neg, l)
        sels.append(sel)
        hots.append(hit.astype(F32))
    e = jnp.exp(val_out - jnp.max(val_out, axis=-1, keepdims=True))
    gate_ref[...] = e / jnp.sum(e, axis=-1, keepdims=True)

    hot = hots[0] + hots[1] + hots[2] + hots[3]
    before = jnp.dot(tri_ref[...], hot.astype(BF16), preferred_element_type=F32) + cnt_scr[...]
    dest = jnp.zeros(logits.shape, F32)
    for k in range(TOP_K):
        rank = jnp.sum(hots[k] * before, axis=-1, keepdims=True)
        dest = jnp.where(lane == k, sels[k] * float(1 << RANK_BITS) + rank, dest)
    dest_ref[...] = dest.astype(I32)
    cnt_scr[...] = cnt_scr[...] + jnp.sum(hot, axis=0, keepdims=True)
    counts_ref[...] = cnt_scr[...]


_ROUTE_OUT_SPECS = [pl.BlockSpec((TM, LANES), lambda i: (i, 0)), pl.BlockSpec((TM, LANES), lambda i: (i, 0)),
                    pl.BlockSpec((1, LANES), lambda i: (0, 0))]


def _route_out_shapes(n):
    return [jax.ShapeDtypeStruct((n, LANES), F32), jax.ShapeDtypeStruct((n, LANES), I32),
            jax.ShapeDtypeStruct((1, LANES), F32)]


def _mm_res_ln_body(y_ref, xp_ref, xs_ref, w_ref, g_ref, b_ref, wr_ref, br_ref, tri_ref,
                    o_ref, gate_ref, dest_ref, counts_ref, cnt_scr):
    h = jnp.dot(y_ref[...].astype(BF16), w_ref[...], preferred_element_type=F32)
    xn = _layer_norm(DEEPNORM_ALPHA * _token_tile(xp_ref, xs_ref) + h, g_ref[...], b_ref[...])
    o_ref[...] = xn
    _route(xn, wr_ref, br_ref, tri_ref, cnt_scr, gate_ref, dest_ref, counts_ref)


def _mm_res_ln(y, xp, xs, w, g, b, wr, br, tri):
    n = xp.shape[0] + xs.shape[0]
    row = pl.BlockSpec((TM, D_MODEL), lambda i: (i, 0))
    full = lambda a: pl.BlockSpec(a.shape, lambda i: (0,) * a.ndim)
    return pl.pallas_call(
        _mm_res_ln_body,
        grid=(n // TM,),
        in_specs=[row] + _token_specs() + [full(w), full(g), full(b), full(wr), full(br), full(tri)],
        out_specs=[row] + _ROUTE_OUT_SPECS,
        out_shape=[jax.ShapeDtypeStruct((n, D_MODEL), F32)] + _route_out_shapes(n),
        scratch_shapes=[pltpu.VMEM((1, LANES), F32)],
        compiler_params=_cparams(("arbitrary",)),
        name="mm_res_ln",
    )(y, xp, xs, w, g, b, wr, br, tri)


def _gelu(x):
    return 0.5 * x * (1.0 + lax.erf(x * (2.0 ** -0.5)))


def _c_out_body(u_ref, v_ref, x_ref, wmix_ref, mask_ref, bias_ref, wout_ref, g_ref, b_ref, wr_ref, br_ref, tri_ref,
                o_ref, gate_ref, dest_ref, counts_ref, us_scr, cnt_scr):
    mix = [(wmix_ref[gi] * mask_ref[...]).astype(BF16) for gi in range(C_GROUPS)]
    for t in range(TM // C_CHUNK):
        rows = slice(t * C_CHUNK, (t + 1) * C_CHUNK)
        for gi in range(C_GROUPS):
            cols = slice(gi * LANES, (gi + 1) * LANES)
            s = jnp.dot(mix[gi], v_ref[rows, cols].astype(BF16), preferred_element_type=F32) + bias_ref[:, cols]
            us_scr[rows, cols] = (u_ref[rows, cols] * s).astype(BF16)
    h = jnp.dot(us_scr[...], wout_ref[...], preferred_element_type=F32)
    xn = _layer_norm(DEEPNORM_ALPHA * x_ref[...] + h, g_ref[...], b_ref[...])
    o_ref[...] = xn
    _route(xn, wr_ref, br_ref, tri_ref, cnt_scr, gate_ref, dest_ref, counts_ref)


def _c_out(u, v, x, wmix2, mask2, bias2, wout, g, b, wr, br, tri):
    n = x.shape[0]
    first_sample_step = N_PROMPT // TM
    sel = lambda i: jnp.where(i >= first_sample_step, 1, 0)
    row = pl.BlockSpec((TM, D_MODEL), lambda i: (i, 0))
    full = lambda a: pl.BlockSpec(a.shape, lambda i: (0,) * a.ndim)
    return pl.pallas_call(
        _c_out_body,
        grid=(n // TM,),
        in_specs=[row, row, row,
                  pl.BlockSpec((None, C_GROUPS, C_CHUNK, C_CHUNK), lambda i: (sel(i), 0, 0, 0)),
                  pl.BlockSpec((None, C_CHUNK, C_CHUNK), lambda i: (sel(i), 0, 0)),
                  pl.BlockSpec((None, C_CHUNK, D_C), lambda i: (sel(i), 0, 0)),
                  full(wout), full(g), full(b), full(wr), full(br), full(tri)],
        out_specs=[row] + _ROUTE_OUT_SPECS,
        out_shape=[jax.ShapeDtypeStruct((n, D_MODEL), F32)] + _route_out_shapes(n),
        scratch_shapes=[pltpu.VMEM((TM, D_C), BF16), pltpu.VMEM((1, LANES), F32)],
        compiler_params=_cparams(("arbitrary",)),
        name="c_out",
    )(u, v, x, wmix2, mask2, bias2, wout, g, b, wr, br, tri)


def _for_each_row(n_rows, fn):
    def group(t8, c):
        base = pl.multiple_of(t8 * SUBLANES, SUBLANES)
        for s in range(SUBLANES):
            fn(t8, base, s)
        return c
    lax.fori_loop(0, n_rows // SUBLANES, group, 0)


def _dispatch_body(pad_start_ref, pad_n_ref, n_used_ref, dest_ref, x_ref, rows_hbm, ring, zbuf, sem, zsem):
    i = pl.program_id(0)

    @pl.when(i == 0)
    def _():
        zbuf[...] = jnp.zeros(zbuf.shape, F32)

    @pl.when(i < N_EXP)
    def _():
        n = pad_n_ref[i]
        start = pad_start_ref[i]
        odd = n & (SUBLANES - 1)
        copies = [(pltpu.make_async_copy(zbuf.at[pl.ds(0, 1)], rows_hbm.at[pl.ds(start + s, 1)], zsem), s < odd)
                  for s in range(SUBLANES - 1)]
        off = start + odd
        for size in [1 << p for p in reversed(range(3, int(math.log2(TM_E))))]:
            dst = rows_hbm.at[pl.ds(pl.multiple_of(off, SUBLANES), size)]
            copies.append((pltpu.make_async_copy(zbuf.at[pl.ds(0, size)], dst, zsem), (n & size) != 0))
            off = off + (n & size)
        for cp, used in copies:
            pl.when(used)(cp.start)
        for cp, used in copies:
            pl.when(used)(cp.wait)

    blk = n_used_ref[0] + (i - N_EXP)

    @pl.when(jnp.logical_and(i >= N_EXP, blk < N_EBLOCKS))
    def _():
        cp = pltpu.make_async_copy(zbuf, rows_hbm.at[pl.ds(pl.multiple_of(blk * TM_E, TM_E), TM_E)], zsem)
        cp.start()
        cp.wait()

    slot = i % 2
    ring[slot] = x_ref[...]

    def push(t8, base, s):
        src = ring.at[slot].at[pl.ds(base, SUBLANES)].at[pl.ds(s, 1)]
        for k in range(TOP_K):
            d = dest_ref[0, 0, (t8 * SUBLANES + s) * TOP_K + k]
            pltpu.make_async_copy(src, rows_hbm.at[pl.ds(d, 1)], sem.at[slot]).start(priority=k % 2)
    _for_each_row(TM_D, push)

    def drain(which):
        def one(t8, base, s):
            for k in range(TOP_K):
                pltpu.make_async_copy(ring.at[0].at[pl.ds(0, 1)], rows_hbm.at[pl.ds(0, 1)], sem.at[which]).wait()
        _for_each_row(TM_D, one)

    @pl.when(i > 0)
    def _():
        drain(1 - slot)

    @pl.when(i == pl.num_programs(0) - 1)
    def _():
        drain(slot)


def _dispatch(pads, dest3, x):
    n = x.shape[0]
    assert n // TM_D >= N_EXP + (N_EBLOCKS - N_PAIRS // TM_E)
    return pl.pallas_call(
        _dispatch_body,
        grid_spec=pltpu.PrefetchScalarGridSpec(
            num_scalar_prefetch=3,
            grid=(n // TM_D,),
            in_specs=[pl.BlockSpec((1, 1, TM_D * TOP_K), lambda i, *_: (i, 0, 0), memory_space=pltpu.SMEM),
                      pl.BlockSpec((TM_D, D_MODEL), lambda i, *_: (i, 0))],
            out_specs=pl.BlockSpec(memory_space=pl.ANY),
            scratch_shapes=[pltpu.VMEM((2, TM_D, D_MODEL), F32), pltpu.VMEM((TM_E, D_MODEL), F32),
                            pltpu.SemaphoreType.DMA((2,)), pltpu.SemaphoreType.DMA(())]),
        out_shape=jax.ShapeDtypeStruct((ROWS_TOTAL, D_MODEL), F32),
        compiler_params=_cparams(("arbitrary",)),
        name="moe_dispatch",
    )(*pads, dest3, x)


def _experts_body(layer, bexp_ref, bval_ref, bnext_ref, x_ref, w1_hbm, b1_ref, w2_hbm, b2_ref, o_ref,
                  w1s, w2s, slot_ref, sem):
    i = pl.program_id(0)
    valid = bval_ref[i] != 0
    fresh = jnp.logical_or(i == 0, bexp_ref[i] != bexp_ref[jnp.maximum(i - 1, 0)])

    def weight_copies(e, slot):
        return (pltpu.make_async_copy(w1_hbm.at[layer, e], w1s.at[slot], sem.at[slot, 0]),
                pltpu.make_async_copy(w2_hbm.at[layer, e], w2s.at[slot], sem.at[slot, 1]))

    @pl.when(jnp.logical_and(valid, fresh))
    def _():
        @pl.when(i == 0)
        def _():
            slot_ref[0] = 1
            for cp in weight_copies(bexp_ref[i], 0):
                cp.start()
        slot = 1 - slot_ref[0]
        slot_ref[0] = slot
        for cp in weight_copies(bexp_ref[i], slot):
            cp.wait()

        @pl.when(bnext_ref[i] >= 0)
        def _():
            for cp in weight_copies(bnext_ref[i], 1 - slot):
                cp.start()

    @pl.when(jnp.logical_not(valid))
    def _():
        o_ref[...] = jnp.zeros(o_ref.shape, F32)

    @pl.when(valid)
    def _():
        slot = slot_ref[0]
        x = x_ref[...]
        glu = jnp.dot(x, w1s[slot, :, :D_EXP], preferred_element_type=F32) + b1_ref[:, :D_EXP]
        lin = jnp.dot(x, w1s[slot, :, D_EXP:], preferred_element_type=F32) + b1_ref[:, D_EXP:]
        glu = jnp.minimum(glu, SWIGLU_LIMIT)
        lin = jnp.clip(lin, -SWIGLU_LIMIT, SWIGLU_LIMIT)
        act = glu * jax.nn.sigmoid(SWIGLU_ALPHA * glu) * (lin + 1.0)
        o_ref[...] = jnp.dot(act, w2s[slot], preferred_element_type=F32) + b2_ref[...]


def _experts(layer, tables, x_rows, w1, b1, w2, b2):
    bspec = lambda w: pl.BlockSpec((None, None, 1, w), lambda i, be, bv, bn: (layer, be[i], 0, 0))
    rows = pl.BlockSpec((TM_E, D_MODEL), lambda i, be, bv, bn: (i, 0))
    hbm = pl.BlockSpec(memory_space=pl.ANY)
    return pl.pallas_call(
        functools.partial(_experts_body, layer),
        grid_spec=pltpu.PrefetchScalarGridSpec(
            num_scalar_prefetch=3,
            grid=(N_EBLOCKS,),
            in_specs=[rows, hbm, bspec(2 * D_EXP), hbm, bspec(D_MODEL)],
            out_specs=rows,
            scratch_shapes=[pltpu.VMEM((2, D_MODEL, 2 * D_EXP), F32), pltpu.VMEM((2, D_EXP, D_MODEL), F32),
                            pltpu.SMEM((1,), I32), pltpu.SemaphoreType.DMA((2, 2))]),
        out_shape=jax.ShapeDtypeStruct((ROWS_TOTAL, D_MODEL), F32),
        compiler_params=_cparams(("arbitrary",)),
        name="moe_experts",
    )(*tables, x_rows, w1, b1, w2, b2)


def _combine_ln_body(tail, n_tiles, dest_ref, dest1_ref, dest2_ref, gate_ref, x_ref, rows_hbm, g_ref, b_ref, *refs):
    if tail == "c_in":
        w_ref, bi_ref, lg_ref, lb_ref, o_ref, u_ref, v_ref, buf, sem = refs
    else:
        o_p_ref, o_s_ref, buf, sem = refs
    i = pl.program_id(0)

    def gather(d_ref, slot):
        def pull(t8, base, s):
            for k in range(TOP_K):
                d = d_ref[0, 0, (t8 * SUBLANES + s) * TOP_K + k]
                dst = buf.at[slot, k].at[pl.ds(base, SUBLANES)].at[pl.ds(s, 1)]
                pltpu.make_async_copy(rows_hbm.at[pl.ds(d, 1)], dst, sem.at[slot]).start(priority=k % 2)
        _for_each_row(TM_C, pull)

    def drain(slot):
        def one(t8, base, s):
            for k in range(TOP_K):
                pltpu.make_async_copy(rows_hbm.at[pl.ds(0, 1)], buf.at[0, 0].at[pl.ds(0, 1)], sem.at[slot]).wait()
        _for_each_row(TM_C, one)

    @pl.when(i == 0)
    def _():
        gather(dest_ref, 0)
        gather(dest1_ref, 1)

    slot = i % 3
    drain(slot)

    gates = gate_ref[...]
    y = buf[slot, 0] * gates[:, 0:1]
    for k in range(1, TOP_K):
        y = y + buf[slot, k] * gates[:, k:k + 1]
    res = _layer_norm(DEEPNORM_ALPHA * x_ref[...] + y, g_ref[...], b_ref[...])
    if tail == "c_in":
        o_ref[...] = res
        xb = res.astype(BF16)
        u_ref[...] = _gelu(jnp.dot(xb, w_ref[:, :D_C], preferred_element_type=F32) + bi_ref[:, :D_C])
        v = _gelu(jnp.dot(xb, w_ref[:, D_C:], preferred_element_type=F32) + bi_ref[:, D_C:])
        v_ref[...] = _layer_norm(v, lg_ref[...], lb_ref[...])

    nxt = (i + 2) % 3
    for t in range(TM_C):
        for k in range(TOP_K):
            src = rows_hbm.at[pl.ds(dest2_ref[0, 0, t * TOP_K + k], 1)]
            pltpu.make_async_copy(src, buf.at[nxt, k].at[pl.ds(t, 1)], sem.at[nxt]).start(priority=k % 2)

    if tail != "c_in":
        is_prompt = i < N_PROMPT // TM_C

        @pl.when(is_prompt)
        def _():
            o_p_ref[...] = res

        @pl.when(jnp.logical_not(is_prompt))
        def _():
            o_s_ref[...] = res

    @pl.when(i == n_tiles - 1)
    def _():
        drain((i + 1) % 3)
        drain(nxt)


def _combine_ln(dest3, gates, x, out_rows, g, b, c_in=None):
    n = x.shape[0]
    n_tiles = n // TM_C
    full = lambda a: pl.BlockSpec(a.shape, lambda i: (0,) * a.ndim)
    row = pl.BlockSpec((TM_C, D_MODEL), lambda i: (i, 0))
    dest_spec = lambda f: pl.BlockSpec((1, 1, TM_C * TOP_K), lambda i: (f(i), 0, 0), memory_space=pltpu.SMEM)
    assert n_tiles >= 2
    in_specs = [dest_spec(lambda i: i), dest_spec(lambda i: jnp.minimum(i + 1, n_tiles - 1)),
                dest_spec(lambda i: jnp.minimum(i + 2, n_tiles - 1)),
                pl.BlockSpec((TM_C, LANES), lambda i: (i, 0)), row, pl.BlockSpec(memory_space=pl.ANY), full(g), full(b)]
    args = [dest3, dest3, dest3, gates, x, out_rows, g, b]
    if c_in is not None:
        in_specs += [full(a) for a in c_in]
        args += list(c_in)
        out_specs = [row, row, row]
        out_shape = [jax.ShapeDtypeStruct((n, D_MODEL), F32)] * 3
    else:
        per_seq = SEQ // TM_C
        last = N_PROMPT // TM_C - 1
        out_specs = [pl.BlockSpec((None, TM_C, D_MODEL),
                                  lambda i: (jnp.minimum(i, last) // per_seq, jnp.minimum(i, last) % per_seq, 0)),
                     pl.BlockSpec((TM_C, D_MODEL), lambda i: (jnp.maximum(i - last - 1, 0), 0))]
        out_shape = [jax.ShapeDtypeStruct((BATCH, SEQ, D_MODEL), F32),
                     jax.ShapeDtypeStruct((N_SAMPLE, D_MODEL), F32)]
    return pl.pallas_call(
        functools.partial(_combine_ln_body, "c_in" if c_in is not None else "final", n_tiles),
        grid=(n_tiles,),
        in_specs=in_specs,
        out_specs=out_specs,
        out_shape=out_shape,
        scratch_shapes=[pltpu.VMEM((3, TOP_K, TM_C, D_MODEL), F32), pltpu.SemaphoreType.DMA((3,))],
        compiler_params=_cparams(("arbitrary",)),
        name="moe_combine_ln",
    )(*args)


def _positions(enc, counts):
    nb = (counts + TM_E - 1) // TM_E
    cum = jnp.cumsum(nb)
    first_blk = cum - nb
    experts = jnp.arange(N_EXP, dtype=I32)
    pair_hot = ((enc >> RANK_BITS)[:, :, None] == experts[None, None, :]).astype(I32)
    dest = jnp.sum(pair_hot * (first_blk * TM_E)[None, None, :], axis=2) + (enc & ((1 << RANK_BITS) - 1))
    n_used = cum[-1]
    blk = jnp.arange(N_EBLOCKS, dtype=I32)
    exp = jnp.minimum(jnp.sum((cum[None, :] <= blk[:, None]).astype(I32), axis=1), N_EXP - 1)
    later = lax.cummin(jnp.where(counts > 0, experts, N_EXP)[::-1])[::-1]
    nxt = jnp.concatenate([later[1:], jnp.full((1,), N_EXP, I32)])
    nxt = jnp.where(nxt >= N_EXP, -1, nxt)
    bnext = jnp.sum((exp[:, None] == experts[None, :]).astype(I32) * nxt[None, :], axis=1)
    pads = (first_blk * TM_E + counts, nb * TM_E - counts, n_used.reshape(1))
    return dest, (exp, (blk < n_used).astype(I32), bnext.astype(I32)), pads


def _moe_post_norm(layer, x, gates, enc, counts, w1, b1, w2, b2, g, b, c_in=None):
    dest4, tables, pads = _positions(enc[:, :TOP_K], counts[0, :N_EXP].astype(I32))
    x_rows = _dispatch(pads, dest4.reshape(N_TOK // TM_D, 1, TM_D * TOP_K), x)
    out_rows = _experts(layer, tables, x_rows, w1, b1.reshape(b1.shape[:2] + (1, 2 * D_EXP)),
                        w2, b2.reshape(b2.shape[:2] + (1, D_MODEL)))
    return _combine_ln(dest4.reshape(N_TOK // TM_C, 1, TM_C * TOP_K), gates, x, out_rows,
                       g.reshape(1, D_MODEL), b.reshape(1, D_MODEL), c_in=c_in)


def _router_weights(w_r, b_r):
    wh = w_r.astype(BF16)
    wl = (w_r - wh.astype(F32)).astype(BF16)
    pad = lambda a: jnp.pad(a, ((0, 0), (0, LANES - N_EXP)))
    return jnp.concatenate([pad(wh), pad(wl)], axis=1), jnp.pad(b_r, (0, LANES - N_EXP)).reshape(1, LANES)


def _tail8(state, keep):
    return jnp.pad(state, ((0, 0), (SUBLANES - keep, 0), (0, 0)))


def kernel(x_prompt, x_sample, state_conv_a, state_conv_qkv, state_delta, ab_w_in, ab_conv_a, ab_conv_qkv,
           ab_a_log, ab_dt_bias, ab_norm_g, ab_w_out, c_w_in, c_b_in, c_ln_g, c_ln_b, c_w_s, c_b_s, c_w_out,
           moe_w_router, moe_b_router, moe_w1, moe_b1, moe_w2, moe_b2, ln_g, ln_b):
    xp, xs = x_prompt.reshape(N_PROMPT, D_MODEL), x_sample.reshape(N_SAMPLE, D_MODEL)
    lnrow = lambda layer, j: (ln_g[layer, j].reshape(1, D_MODEL), ln_b[layer, j].reshape(1, D_MODEL))
    ri = jnp.arange(TM)
    tri = (ri[:, None] > ri[None, :]).astype(BF16)

    w_in = ab_w_in[0]
    w_main = w_in[:, :W_MAIN].astype(BF16)
    w_ab = jnp.pad(w_in[:, W_MAIN:], ((0, 0), (0, LANES - 2 * H_B))).astype(BF16)
    alog_row = jnp.pad(ab_a_log[0], (0, LANES - H_B)).reshape(1, LANES)
    dtb_row = jnp.pad(ab_dt_bias[0], (0, LANES - H_B)).reshape(1, LANES)
    bg, u, qkv, z, gb = _proj_ab(xp, xs, w_main, w_ab, alog_row, dtb_row)

    ng = ab_norm_g[0].reshape(1, DV)
    ycat = jnp.zeros((N_TOK, D_MODEL), BF16)
    gbr_p = _group_rows(gb[:N_PROMPT], BATCH, SEQ, DN_CHUNK, 1, GDN_CPS)
    ycat, pq8, pu8, p_delta = _gdn(
        qkv, u, bg, z, gbr_p,
        jnp.zeros((BATCH, SUBLANES, D_QKV), F32), jnp.zeros((BATCH, SUBLANES, D_A), F32),
        jnp.zeros((BATCH, H_B, DK, DV), F32), ab_conv_qkv[0], ab_conv_a[0], ng, ycat,
        chunk=DN_CHUNK, bb=1, cps=GDN_CPS, n_seq=BATCH, t_len=SEQ, row_block0=0, phased=True)
    bb_s = 16
    gbr_s = _group_rows(gb[N_PROMPT:], DEC_BATCH, DEC_SEQ, DEC_SEQ, bb_s, 1)
    ycat, sq8, su8, s_delta = _gdn(
        qkv, u, bg, z, gbr_s,
        _tail8(state_conv_qkv[0], CONV_B - 1), _tail8(state_conv_a[0], CONV_A - 1), state_delta[0],
        ab_conv_qkv[0], ab_conv_a[0], ng, ycat,
        chunk=DEC_SEQ, bb=bb_s, cps=1, n_seq=DEC_BATCH, t_len=DEC_SEQ, row_block0=N_PROMPT // (bb_s * DEC_SEQ))
    wr0, br0 = _router_weights(moe_w_router[0], moe_b_router[0])
    x, gates, dest, counts = _mm_res_ln(ycat, xp, xs, ab_w_out[0].astype(BF16), *lnrow(0, 0), wr0, br0, tri)
    c_in = (c_w_in[0].astype(BF16), c_b_in[0].reshape(1, 2 * D_C), c_ln_g[0].reshape(1, D_C), c_ln_b[0].reshape(1, D_C))
    x, uc, vc = _moe_post_norm(0, x, gates, dest, counts, moe_w1, moe_b1, moe_w2, moe_b2, ln_g[0, 1], ln_b[0, 1],
                               c_in=c_in)
    ws = c_w_s[0]
    reps = C_CHUNK // DEC_SEQ
    wmix2 = jnp.stack([ws, jnp.tile(ws[:, :DEC_SEQ, :DEC_SEQ], (1, reps, reps))])
    rc = jnp.arange(C_CHUNK)
    tril = rc[:, None] >= rc[None, :]
    mask2 = jnp.stack([tril, tril & ((rc[:, None] // DEC_SEQ) == (rc[None, :] // DEC_SEQ))]).astype(F32)
    bias_p = jnp.repeat(c_b_s[0].T, D_C // C_GROUPS, axis=1)
    bias2 = jnp.stack([bias_p, jnp.tile(bias_p[:DEC_SEQ], (reps, 1))])
    wr1, br1 = _router_weights(moe_w_router[1], moe_b_router[1])
    x, gates, dest, counts = _c_out(uc, vc, x, wmix2, mask2, bias2, c_w_out[0].astype(BF16), *lnrow(1, 0),
                                    wr1, br1, tri)
    y_prompt, y_sample = _moe_post_norm(1, x, gates, dest, counts, moe_w1, moe_b1, moe_w2, moe_b2,
                                        ln_g[1, 1], ln_b[1, 1])
    y_sample = y_sample.reshape(DEC_BATCH, DEC_SEQ, D_MODEL)
    ka, kq = CONV_A - 1, CONV_B - 1
    return (y_prompt, y_sample,
            pu8[None, :, SUBLANES - ka:], pq8[None, :, SUBLANES - kq:], p_delta[None],
            su8[None, :, SUBLANES - ka:], sq8[None, :, SUBLANES - kq:], s_delta[None],
            vc[N_PROMPT:].reshape(1, DEC_BATCH, DEC_SEQ, D_C))
```

```python
import functools
import math

import jax
import jax.numpy as jnp
from jax import lax
from jax.experimental import pallas as pl
from jax.experimental.pallas import tpu as pltpu

F32 = jnp.float32
BF16 = jnp.bfloat16
I32 = jnp.int32

D_MODEL = 1024
BATCH = 8
SEQ = 2048
DEC_BATCH = 128
DEC_SEQ = 8
N_PROMPT = BATCH * SEQ
N_SAMPLE = DEC_BATCH * DEC_SEQ
N_TOK = N_PROMPT + N_SAMPLE
D_A = 512
CONV_A = 3
H_B = 4
DK = 128
DV = 128
D_QKV = 1536
CONV_B = 4
DN_CHUNK = 64
W_MAIN = 3 * D_A + D_QKV + H_B * DV
D_C = 1024
C_GROUPS = 8
C_CHUNK = 128
N_EXP = 32
TOP_K = 4
D_EXP = 1024
SWIGLU_ALPHA = 1.702
SWIGLU_LIMIT = 7.0
DEEPNORM_ALPHA = 4.0 ** 0.25
LN_EPS = 1e-5
RMS_EPS = 1e-6

LANES = 128
SUBLANES = 8
VMEM_LIMIT = 56 * 1024 * 1024

TM = 512
TM_E = 512
TM_C = 256
TM_D = 256
GDN_CPS = 8
N_PAIRS = N_TOK * TOP_K
N_EBLOCKS = (N_PAIRS + N_EXP * (TM_E - 1)) // TM_E
ROWS_TOTAL = N_EBLOCKS * TM_E
RANK_BITS = 15
assert N_TOK <= 1 << RANK_BITS


def _cparams(sem, n_inputs=0, fuse=()):
    fusion = [i in fuse for i in range(n_inputs)] if fuse else None
    return pltpu.CompilerParams(dimension_semantics=sem, vmem_limit_bytes=VMEM_LIMIT, allow_input_fusion=fusion)


def _layer_norm(t, g, b):
    mu = jnp.mean(t, axis=-1, keepdims=True)
    d = t - mu
    var = jnp.mean(d * d, axis=-1, keepdims=True)
    return d * lax.rsqrt(var + LN_EPS) * g + b


def _bdot(a, b):
    return jnp.dot(a.astype(BF16), b.astype(BF16), preferred_element_type=F32)


def _token_tile(xp_ref, xs_ref):
    return jnp.where(pl.program_id(0) < N_PROMPT // TM, xp_ref[...], xs_ref[...])


def _token_specs():
    n_p = N_PROMPT // TM
    return [pl.BlockSpec((TM, D_MODEL), lambda i: (jnp.minimum(i, n_p - 1), 0)),
            pl.BlockSpec((TM, D_MODEL), lambda i: (jnp.maximum(i - n_p, 0), 0))]


def _proj_ab_body(xp_ref, xs_ref, w_ref, wab_ref, alog_ref, dtb_ref, bg_ref, u_ref, qkv_ref, z_ref, gb_ref):
    xb = _token_tile(xp_ref, xs_ref).astype(BF16)

    def mm(lo, hi):
        return jnp.dot(xb, w_ref[:, lo:hi], preferred_element_type=F32)

    bg_ref[...] = mm(0, D_A)
    u_ref[...] = mm(D_A, 2 * D_A) * mm(2 * D_A, 3 * D_A)
    for c in range(D_QKV // 512):
        qkv_ref[:, c * 512:(c + 1) * 512] = mm(3 * D_A + c * 512, 3 * D_A + (c + 1) * 512)
    z_ref[...] = mm(3 * D_A + D_QKV, W_MAIN)
    ab = jnp.dot(xb, wab_ref[...], preferred_element_type=F32)
    g = -jnp.exp(alog_ref[...]) * jax.nn.softplus(ab + dtb_ref[...])
    beta = jax.nn.sigmoid(ab)
    lane = lax.broadcasted_iota(I32, ab.shape, 1)
    gb_ref[...] = jnp.where(lane < H_B, g, beta)


def _proj_ab(xp, xs, w_main, w_ab, alog_row, dtb_row):
    n = xp.shape[0] + xs.shape[0]
    row = lambda w: pl.BlockSpec((TM, w), lambda i: (i, 0))
    full = lambda a: pl.BlockSpec(a.shape, lambda i: (0,) * a.ndim)
    return pl.pallas_call(
        _proj_ab_body,
        grid=(n // TM,),
        in_specs=_token_specs() + [full(w_main), full(w_ab), full(alog_row), full(dtb_row)],
        out_specs=[row(D_A), row(D_A), row(D_QKV), row(D_A), row(LANES)],
        out_shape=[jax.ShapeDtypeStruct((n, D_A), F32), jax.ShapeDtypeStruct((n, D_A), F32),
                   jax.ShapeDtypeStruct((n, D_QKV), F32), jax.ShapeDtypeStruct((n, D_A), F32),
                   jax.ShapeDtypeStruct((n, LANES), F32)],
        compiler_params=_cparams(("arbitrary",), 6, fuse=(2, 3)),
        name="proj_ab",
    )(xp, xs, w_main, w_ab, alog_row, dtb_row)


def _shift_rows(x, prev8, s):
    if s == 0:
        return x
    xr = pltpu.roll(x, s, axis=0)
    pr = pltpu.roll(prev8, s, axis=0)
    rid = lax.broadcasted_iota(I32, pr.shape, 0)
    head = jnp.where(rid < s, pr, xr[0:SUBLANES])
    if x.shape[0] == SUBLANES:
        return head
    return jnp.concatenate([head, xr[SUBLANES:]], axis=0)


def _causal_conv(x, prev8, w_ref, taps):
    y = x * w_ref[taps - 1:taps, :]
    for s in range(1, taps):
        y = y + _shift_rows(x, prev8, s) * w_ref[taps - 1 - s:taps - s, :]
    return y


def _lane_scan(x, pos, chunk, reverse):
    s = 1
    while s < chunk:
        if reverse:
            x = x + jnp.where(pos < chunk - s, pltpu.roll(x, LANES - s, axis=1), 0.0)
        else:
            x = x + jnp.where(pos >= s, pltpu.roll(x, s, axis=1), 0.0)
        s *= 2
    return x


def _per_row(row):
    return jnp.broadcast_to(row, (LANES, LANES)).T


def _gdn_body(chunk, bb, cps, n_steps,
              qkv_ref, u_ref, bg_ref, z_ref, gbr_ref, pq_ref, pu_ref, s0_ref, wq_ref, wa_ref, ng_ref, y_any,
              ycat_ref, nq_ref, nu_ref, sn_ref, cq_scr, cu_scr, s_scr):
    del y_any
    n = pl.program_id(1)

    @pl.when(n == 0)
    def _():
        cq_scr[...] = pq_ref[...]
        cu_scr[...] = pu_ref[...]
        s_scr[...] = s0_ref[...]

    gsz = LANES // chunk
    n_groups = bb * cps * H_B // gsz
    levels = int(math.log2(chunk))
    span = cps * chunk

    ii = lax.broadcasted_iota(I32, (LANES, LANES), 0)
    jj = lax.broadcasted_iota(I32, (LANES, LANES), 1)
    same = (ii // chunk) == (jj // chunk)
    m_incl = same & (ii >= jj)
    m_strict = same & (ii > jj)
    eye = (ii == jj).astype(F32)
    pos = lax.broadcasted_iota(I32, (SUBLANES, LANES), 1) % chunk

    qs, ks, vs = {}, {}, {}
    for b in range(bb):
        rows = slice(b * span, (b + 1) * span)
        x = qkv_ref[rows, :]
        qc = _causal_conv(x, cq_scr[b], wq_ref, CONV_B)
        qc = qc * jax.nn.sigmoid(qc)
        cq_scr[b] = x[span - SUBLANES:span]
        uu = u_ref[rows, :]
        ca = _causal_conv(uu, cu_scr[b], wa_ref, CONV_A)
        cu_scr[b] = uu[span - SUBLANES:span]
        ycat_ref[rows, 0:D_A] = (bg_ref[rows, :] * ca).astype(BF16)
        for j in range(cps):
            r = slice(j * chunk, (j + 1) * chunk)
            for h in range(H_B):
                qh = qc[r, h * DK:(h + 1) * DK]
                kh = qc[r, H_B * DK + h * DK:H_B * DK + (h + 1) * DK]
                vh = qc[r, 2 * H_B * DK + h * DV:2 * H_B * DK + (h + 1) * DV]
                key = (b * cps + j, h)
                qs[key] = qh * (lax.rsqrt(jnp.sum(qh * qh, axis=-1, keepdims=True) + RMS_EPS) * (DK ** -0.5))
                ks[key] = kh * lax.rsqrt(jnp.sum(kh * kh, axis=-1, keepdims=True) + RMS_EPS)
                vs[key] = vh

    for gi in range(n_groups):
        blocks = [divmod(gi * gsz + t, H_B) for t in range(gsz)]
        cat = lambda d: jnp.concatenate([d[uh] for uh in blocks], axis=0) if gsz > 1 else d[blocks[0]]
        qg, kg, vg = cat(qs), cat(ks), cat(vs)

        tile = gbr_ref[gi]
        gc = _lane_scan(tile, pos, chunk, False)
        rs = _lane_scan(tile, pos, chunk, True) - tile
        gc_row = gc[0:1]
        gc_m = _per_row(gc_row)
        rs_m = _per_row(rs[0:1])
        beta_m = _per_row(tile[1:2])
        diff = gc_m - jnp.broadcast_to(gc_row, (LANES, LANES))
        decay = jnp.where(m_incl, jnp.exp(jnp.where(m_incl, diff, 0.0)), 0.0)
        eg = jnp.exp(gc_m)
        etot = jnp.exp(gc_m + rs_m)

        kb = kg * beta_m
        kgb = kg.astype(BF16)
        a_mat = lax.dot_general(kb.astype(BF16), kgb, (((1,), (1,)), ((), ())), preferred_element_type=F32)
        lm = jnp.where(m_strict, a_mat * decay, 0.0)
        attn = lax.dot_general(qg.astype(BF16), kgb, (((1,), (1,)), ((), ())), preferred_element_type=F32) * decay

        p = eye - lm
        m = _bdot(lm, lm)
        for lvl in range(1, levels):
            p = p + _bdot(p, m)
            if lvl < levels - 1:
                m = _bdot(m, m)
        uw = _bdot(p, jnp.concatenate([vg * beta_m, kb * eg], axis=1))
        u_all, w_all = uw[:, :DV], uw[:, DV:]
        qe = qg * eg
        kdec = kg * jnp.exp(rs_m)

        vnew, qsv = [], []
        for t, (unit, h) in enumerate(blocks):
            b = unit // cps
            r = slice(t * chunk, (t + 1) * chunk)
            s_bf = s_scr[b, h].astype(BF16)
            lhs = jnp.concatenate([w_all[r], qe[r]], axis=0).astype(BF16)
            both = jnp.dot(lhs, s_bf, preferred_element_type=F32)
            vnew.append(u_all[r] - both[:chunk])
            qsv.append(both[chunk:])
        vnew_g = jnp.concatenate(vnew, axis=0) if gsz > 1 else vnew[0]
        qs_g = jnp.concatenate(qsv, axis=0) if gsz > 1 else qsv[0]
        o = qs_g + _bdot(attn, vnew_g)
        o = o * lax.rsqrt(jnp.mean(o * o, axis=-1, keepdims=True) + RMS_EPS) * ng_ref[...]

        for t, (unit, h) in enumerate(blocks):
            b = unit // cps
            r = slice(t * chunk, (t + 1) * chunk)
            rows = slice(unit * chunk, (unit + 1) * chunk)
            zz = z_ref[rows, h * DV:(h + 1) * DV]
            ycat_ref[rows, D_A + h * DV:D_A + (h + 1) * DV] = (o[r] * (zz * jax.nn.sigmoid(zz))).astype(BF16)
            upd = lax.dot_general(kdec[r].astype(BF16), vnew[t].astype(BF16), (((0,), (0,)), ((), ())),
                                  preferred_element_type=F32)
            scale = jnp.broadcast_to(etot[t * chunk:t * chunk + 1, :], (DK, DV))
            s_scr[b, h] = s_scr[b, h] * scale + upd

    @pl.when(n == n_steps - 1)
    def _():
        nq_ref[...] = cq_scr[...]
        nu_ref[...] = cu_scr[...]
        sn_ref[...] = s_scr[...]


def _gdn_phased_body(chunk, cps, n_steps,
                     qkv_ref, u_ref, bg_ref, z_ref, gbr_ref, pq_ref, pu_ref, s0_ref, wq_ref, wa_ref, ng_ref, y_any,
                     ycat_ref, nq_ref, nu_ref, sn_ref, cq_scr, cu_scr, s_scr,
                     kf, qf, vf, k16, kb16, q16, kdec16, att16, m16, c16, vk16, uw16,
                     dec, pm, qe, etot, dm, om, n16, bm):
    del y_any
    n = pl.program_id(1)

    @pl.when(n == 0)
    def _():
        cq_scr[...] = pq_ref[...]
        cu_scr[...] = pu_ref[...]
        s_scr[...] = s0_ref[...]

    gsz = LANES // chunk
    n_groups = cps * H_B // gsz
    levels = int(math.log2(chunk))
    span = cps * chunk
    groups = range(n_groups)
    blocks_of = lambda gi: [divmod(gi * gsz + t, H_B) for t in range(gsz)]
    rows_of = lambda t: slice(t * chunk, (t + 1) * chunk)

    ii = lax.broadcasted_iota(I32, (LANES, LANES), 0)
    jj = lax.broadcasted_iota(I32, (LANES, LANES), 1)
    same = (ii // chunk) == (jj // chunk)
    m_incl = same & (ii >= jj)
    m_strict = same & (ii > jj)
    eye = (ii == jj).astype(F32)
    pos = lax.broadcasted_iota(I32, (SUBLANES, LANES), 1) % chunk
    nt = (((1,), (1,)), ((), ()))
    tn = (((0,), (0,)), ((), ()))

    x = qkv_ref[...]
    qc = _causal_conv(x, cq_scr[0], wq_ref, CONV_B)
    qc = qc * jax.nn.sigmoid(qc)
    cq_scr[0] = x[span - SUBLANES:span]
    uu = u_ref[...]
    ca = _causal_conv(uu, cu_scr[0], wa_ref, CONV_A)
    cu_scr[0] = uu[span - SUBLANES:span]
    ycat_ref[:, 0:D_A] = (bg_ref[...] * ca).astype(BF16)
    for gi in groups:
        for t, (j, h) in enumerate(blocks_of(gi)):
            r = rows_of(j)
            qh = qc[r, h * DK:(h + 1) * DK]
            kh = qc[r, H_B * DK + h * DK:H_B * DK + (h + 1) * DK]
            qf[gi, rows_of(t), :] = qh * (lax.rsqrt(jnp.sum(qh * qh, axis=-1, keepdims=True) + RMS_EPS)
                                          * (DK ** -0.5))
            kf[gi, rows_of(t), :] = kh * lax.rsqrt(jnp.sum(kh * kh, axis=-1, keepdims=True) + RMS_EPS)
            vf[gi, rows_of(t), :] = qc[r, 2 * H_B * DK + h * DV:2 * H_B * DK + (h + 1) * DV]

    for gi in groups:
        tile = gbr_ref[gi]
        gc = _lane_scan(tile, pos, chunk, False)
        rs = _lane_scan(tile, pos, chunk, True) - tile
        gc_row = gc[0:1]
        gc_m = _per_row(gc_row)
        rs_m = _per_row(rs[0:1])
        beta_m = _per_row(tile[1:2])
        diff = gc_m - jnp.broadcast_to(gc_row, (LANES, LANES))
        dec[gi] = jnp.where(m_incl, jnp.exp(jnp.where(m_incl, diff, 0.0)), 0.0)
        eg = jnp.exp(gc_m)
        etot[gi] = jnp.exp(gc_m + rs_m)
        kg = kf[gi]
        kb = kg * beta_m
        k16[gi] = kg.astype(BF16)
        kb16[gi] = kb.astype(BF16)
        q16[gi] = qf[gi].astype(BF16)
        qe[gi] = qf[gi] * eg
        kdec16[gi] = (kg * jnp.exp(rs_m)).astype(BF16)
        vk16[gi, :, 0:DV] = (vf[gi] * beta_m).astype(BF16)
        vk16[gi, :, DV:] = (kb * eg).astype(BF16)

    for gi in groups:
        a_mat = lax.dot_general(kb16[gi], k16[gi], nt, preferred_element_type=F32)
        lm = jnp.where(m_strict, a_mat * dec[gi], 0.0)
        pm[gi] = eye - lm
        lm16 = lm.astype(BF16)
        m16[gi] = jnp.dot(lm16, lm16, preferred_element_type=F32).astype(BF16)
        att16[gi] = (lax.dot_general(q16[gi], k16[gi], nt, preferred_element_type=F32) * dec[gi]).astype(BF16)

    for lvl in range(1, levels):
        for gi in groups:
            pm[gi] = pm[gi] + jnp.dot(pm[gi].astype(BF16), m16[gi], preferred_element_type=F32)
        if lvl < levels - 1:
            for gi in groups:
                m16[gi] = jnp.dot(m16[gi], m16[gi], preferred_element_type=F32).astype(BF16)

    for gi in groups:
        uw16[gi] = jnp.dot(pm[gi].astype(BF16), vk16[gi], preferred_element_type=F32).astype(BF16)
    for gi in groups:
        au = jnp.dot(att16[gi], uw16[gi], preferred_element_type=F32)
        dm[gi] = au[:, :DV]
        c16[gi] = (qe[gi] - au[:, DV:]).astype(BF16)
        for t in range(gsz):
            r = rows_of(t)
            nb = lax.dot_general(kdec16[gi, r, :], uw16[gi, r, :], tn, preferred_element_type=F32)
            bm[gi * gsz + t] = nb[:, :DV]
            n16[gi * gsz + t] = nb[:, DV:].astype(BF16)

    for gi in groups:
        for t, (j, h) in enumerate(blocks_of(gi)):
            r = rows_of(t)
            s_old = s_scr[0, h]
            lhs = jnp.concatenate([c16[gi, r, :], n16[gi * gsz + t]], axis=0)
            both = jnp.dot(lhs, s_old.astype(BF16), preferred_element_type=F32)
            om[gi, r, :] = both[:chunk] + dm[gi, r, :]
            scale = jnp.broadcast_to(etot[gi, t * chunk:t * chunk + 1, :], (DK, DV))
            s_scr[0, h] = s_old * scale - both[chunk:] + bm[gi * gsz + t]

    for gi in groups:
        o = om[gi]
        o = o * lax.rsqrt(jnp.mean(o * o, axis=-1, keepdims=True) + RMS_EPS) * ng_ref[...]
        for t, (j, h) in enumerate(blocks_of(gi)):
            zz = z_ref[rows_of(j), h * DV:(h + 1) * DV]
            ycat_ref[rows_of(j), D_A + h * DV:D_A + (h + 1) * DV] = (
                o[rows_of(t)] * (zz * jax.nn.sigmoid(zz))).astype(BF16)

    @pl.when(n == n_steps - 1)
    def _():
        nq_ref[...] = cq_scr[...]
        nu_ref[...] = cu_scr[...]
        sn_ref[...] = s_scr[...]


def _gdn_phased_scratch(chunk, cps):
    g = cps * H_B * chunk // LANES
    nblk = cps * H_B
    mat = lambda dt, n=g, w=LANES: pltpu.VMEM((n, LANES, w), dt)
    return ([mat(F32)] * 3 + [mat(BF16)] * 7 + [mat(BF16, w=2 * LANES)] * 2 + [mat(F32)] * 6
            + [mat(BF16, n=nblk), mat(F32, n=nblk)])


def _gdn(qkv, u, bg, z, gbr, prev_q, prev_u, s0, wq, wa, ng, ycat, *, chunk, bb, cps, n_seq, t_len, row_block0,
         phased=False):
    rb = bb * cps * chunk
    gs = rb * H_B // LANES
    n_steps = t_len // (cps * chunk)
    rowmap = lambda i, n: (row_block0 + i * n_steps + n, 0)
    row = lambda w: pl.BlockSpec((rb, w), rowmap)
    seq3 = lambda w: pl.BlockSpec((bb, SUBLANES, w), lambda i, n: (i, 0, 0))
    full = lambda a: pl.BlockSpec(a.shape, lambda i, n: (0,) * a.ndim)
    st = pl.BlockSpec((bb, H_B, DK, DV), lambda i, n: (i, 0, 0, 0))
    if phased:
        assert bb == 1
        body = functools.partial(_gdn_phased_body, chunk, cps, n_steps)
        extra_scratch = _gdn_phased_scratch(chunk, cps)
    else:
        body = functools.partial(_gdn_body, chunk, bb, cps, n_steps)
        extra_scratch = []
    return pl.pallas_call(
        body,
        grid=(n_seq // bb, n_steps),
        in_specs=[row(D_QKV), row(D_A), row(D_A), row(D_A),
                  pl.BlockSpec((None, gs, SUBLANES, LANES), lambda i, n: (i * n_steps + n, 0, 0, 0)),
                  seq3(D_QKV), seq3(D_A), st, full(wq), full(wa), full(ng),
                  pl.BlockSpec(memory_space=pl.ANY)],
        out_specs=[pl.BlockSpec((rb, D_MODEL), rowmap), seq3(D_QKV), seq3(D_A), st],
        out_shape=[jax.ShapeDtypeStruct(ycat.shape, BF16),
                   jax.ShapeDtypeStruct((n_seq, SUBLANES, D_QKV), F32),
                   jax.ShapeDtypeStruct((n_seq, SUBLANES, D_A), F32),
                   jax.ShapeDtypeStruct((n_seq, H_B, DK, DV), F32)],
        scratch_shapes=[pltpu.VMEM((bb, SUBLANES, D_QKV), F32), pltpu.VMEM((bb, SUBLANES, D_A), F32),
                        pltpu.VMEM((bb, H_B, DK, DV), F32)] + extra_scratch,
        input_output_aliases={11: 0},
        compiler_params=_cparams(("arbitrary", "arbitrary")),
        name=f"gdn_c{chunk}",
    )(qkv, u, bg, z, gbr, prev_q, prev_u, s0, wq, wa, ng, ycat)


def _group_rows(gb, n_seq, t_len, chunk, bb, cps):
    n_steps = t_len // (cps * chunk)
    gs = bb * cps * chunk * H_B // LANES
    g = gb[:, :2 * H_B].reshape(n_seq // bb, bb, n_steps, cps, chunk, 2, H_B)
    g = jnp.transpose(g, (0, 2, 5, 1, 3, 6, 4))
    g = g.reshape(n_seq // bb * n_steps, 2, gs, LANES)
    g = jnp.transpose(g, (0, 2, 1, 3))
    return jnp.pad(g, ((0, 0), (0, 0), (0, SUBLANES - 2), (0, 0)))


def _route(x_new, wr_ref, br_ref, tri_ref, cnt_scr, gate_ref, dest_ref, counts_ref):
    @pl.when(pl.program_id(0) == 0)
    def _():
        cnt_scr[...] = jnp.zeros(cnt_scr.shape, F32)

    xh = x_new.astype(BF16)
    xl = (x_new - xh.astype(F32)).astype(BF16)
    p = jnp.dot(xh, wr_ref[...], preferred_element_type=F32)
    logits = (p[:, :LANES] + p[:, LANES:] + jnp.dot(xl, wr_ref[:, :LANES], preferred_element_type=F32)
              + br_ref[...])
    lane = lax.broadcasted_iota(I32, logits.shape, 1).astype(F32)
    neg = jnp.float32(-jnp.inf)
    l = jnp.where(lane < N_EXP, logits, neg)
    val_out = jnp.full(logits.shape, neg, F32)
    sels, hots = [], []
    for k in range(TOP_K):
        m = jnp.max(l, axis=-1, keepdims=True)
        sel = jnp.min(jnp.where(l == m, lane, float(LANES)), axis=-1, keepdims=True)
        hit = lane == sel
        val_out = jnp.where(lane == k, m, val_out)
        l = jnp.where(hit, neg, l)
        sels.append(sel)
        hots.append(hit.astype(F32))
    e = jnp.exp(val_out - jnp.max(val_out, axis=-1, keepdims=True))
    gate_ref[...] = e / jnp.sum(e, axis=-1, keepdims=True)

    hot = hots[0] + hots[1] + hots[2] + hots[3]
    before = jnp.dot(tri_ref[...], hot.astype(BF16), preferred_element_type=F32) + cnt_scr[...]
    dest = jnp.zeros(logits.shape, F32)
    for k in range(TOP_K):
        rank = jnp.sum(hots[k] * before, axis=-1, keepdims=True)
        dest = jnp.where(lane == k, sels[k] * float(1 << RANK_BITS) + rank, dest)
    dest_ref[...] = dest.astype(I32)
    cnt_scr[...] = cnt_scr[...] + jnp.sum(hot, axis=0, keepdims=True)
    counts_ref[...] = cnt_scr[...]


_ROUTE_OUT_SPECS = [pl.BlockSpec((TM, LANES), lambda i: (i, 0)), pl.BlockSpec((TM, LANES), lambda i: (i, 0)),
                    pl.BlockSpec((1, LANES), lambda i: (0, 0))]


def _route_out_shapes(n):
    return [jax.ShapeDtypeStruct((n, LANES), F32), jax.ShapeDtypeStruct((n, LANES), I32),
            jax.ShapeDtypeStruct((1, LANES), F32)]


def _mm_res_ln_body(y_ref, xp_ref, xs_ref, w_ref, g_ref, b_ref, wr_ref, br_ref, tri_ref,
                    o_ref, gate_ref, dest_ref, counts_ref, cnt_scr):
    h = jnp.dot(y_ref[...].astype(BF16), w_ref[...], preferred_element_type=F32)
    xn = _layer_norm(DEEPNORM_ALPHA * _token_tile(xp_ref, xs_ref) + h, g_ref[...], b_ref[...])
    o_ref[...] = xn
    _route(xn, wr_ref, br_ref, tri_ref, cnt_scr, gate_ref, dest_ref, counts_ref)


def _mm_res_ln(y, xp, xs, w, g, b, wr, br, tri):
    n = xp.shape[0] + xs.shape[0]
    row = pl.BlockSpec((TM, D_MODEL), lambda i: (i, 0))
    full = lambda a: pl.BlockSpec(a.shape, lambda i: (0,) * a.ndim)
    return pl.pallas_call(
        _mm_res_ln_body,
        grid=(n // TM,),
        in_specs=[row] + _token_specs() + [full(w), full(g), full(b), full(wr), full(br), full(tri)],
        out_specs=[row] + _ROUTE_OUT_SPECS,
        out_shape=[jax.ShapeDtypeStruct((n, D_MODEL), F32)] + _route_out_shapes(n),
        scratch_shapes=[pltpu.VMEM((1, LANES), F32)],
        compiler_params=_cparams(("arbitrary",), 9, fuse=(3, 6, 8)),
        name="mm_res_ln",
    )(y, xp, xs, w, g, b, wr, br, tri)


def _gelu(x):
    return 0.5 * x * (1.0 + lax.erf(x * (2.0 ** -0.5)))


def _c_out_body(u_ref, v_ref, x_ref, wmix_ref, mask_ref, bias_ref, wout_ref, g_ref, b_ref, wr_ref, br_ref, tri_ref,
                o_ref, gate_ref, dest_ref, counts_ref, us_scr, cnt_scr):
    mix = [(wmix_ref[gi] * mask_ref[...]).astype(BF16) for gi in range(C_GROUPS)]
    for t in range(TM // C_CHUNK):
        rows = slice(t * C_CHUNK, (t + 1) * C_CHUNK)
        for gi in range(C_GROUPS):
            cols = slice(gi * LANES, (gi + 1) * LANES)
            s = jnp.dot(mix[gi], v_ref[rows, cols].astype(BF16), preferred_element_type=F32) + bias_ref[:, cols]
            us_scr[rows, cols] = (u_ref[rows, cols] * s).astype(BF16)
    h = jnp.dot(us_scr[...], wout_ref[...], preferred_element_type=F32)
    xn = _layer_norm(DEEPNORM_ALPHA * x_ref[...] + h, g_ref[...], b_ref[...])
    o_ref[...] = xn
    _route(xn, wr_ref, br_ref, tri_ref, cnt_scr, gate_ref, dest_ref, counts_ref)


def _c_out(u, v, x, wmix2, mask2, bias2, wout, g, b, wr, br, tri):
    n = x.shape[0]
    first_sample_step = N_PROMPT // TM
    sel = lambda i: jnp.where(i >= first_sample_step, 1, 0)
    row = pl.BlockSpec((TM, D_MODEL), lambda i: (i, 0))
    full = lambda a: pl.BlockSpec(a.shape, lambda i: (0,) * a.ndim)
    return pl.pallas_call(
        _c_out_body,
        grid=(n // TM,),
        in_specs=[row, row, row,
                  pl.BlockSpec((None, C_GROUPS, C_CHUNK, C_CHUNK), lambda i: (sel(i), 0, 0, 0)),
                  pl.BlockSpec((None, C_CHUNK, C_CHUNK), lambda i: (sel(i), 0, 0)),
                  pl.BlockSpec((None, C_CHUNK, D_C), lambda i: (sel(i), 0, 0)),
                  full(wout), full(g), full(b), full(wr), full(br), full(tri)],
        out_specs=[row] + _ROUTE_OUT_SPECS,
        out_shape=[jax.ShapeDtypeStruct((n, D_MODEL), F32)] + _route_out_shapes(n),
        scratch_shapes=[pltpu.VMEM((TM, D_C), BF16), pltpu.VMEM((1, LANES), F32)],
        compiler_params=_cparams(("arbitrary",), 12, fuse=(3, 4, 5, 6, 9, 11)),
        name="c_out",
    )(u, v, x, wmix2, mask2, bias2, wout, g, b, wr, br, tri)


def _for_each_row(n_rows, fn):
    def group(t8, c):
        base = pl.multiple_of(t8 * SUBLANES, SUBLANES)
        for s in range(SUBLANES):
            fn(t8, base, s)
        return c
    lax.fori_loop(0, n_rows // SUBLANES, group, 0)


def _dispatch_body(pad_start_ref, pad_n_ref, n_used_ref, dest_ref, x_ref, rows_hbm, ring, zbuf, sem, zsem):
    i = pl.program_id(0)

    @pl.when(i == 0)
    def _():
        zbuf[...] = jnp.zeros(zbuf.shape, F32)

    @pl.when(i < N_EXP)
    def _():
        n = pad_n_ref[i]
        start = pad_start_ref[i]
        odd = n & (SUBLANES - 1)
        copies = [(pltpu.make_async_copy(zbuf.at[pl.ds(0, 1)], rows_hbm.at[pl.ds(start + s, 1)], zsem), s < odd)
                  for s in range(SUBLANES - 1)]
        off = start + odd
        for size in [1 << p for p in reversed(range(3, int(math.log2(TM_E))))]:
            dst = rows_hbm.at[pl.ds(pl.multiple_of(off, SUBLANES), size)]
            copies.append((pltpu.make_async_copy(zbuf.at[pl.ds(0, size)], dst, zsem), (n & size) != 0))
            off = off + (n & size)
        for cp, used in copies:
            pl.when(used)(cp.start)
        for cp, used in copies:
            pl.when(used)(cp.wait)

    blk = n_used_ref[0] + (i - N_EXP)

    @pl.when(jnp.logical_and(i >= N_EXP, blk < N_EBLOCKS))
    def _():
        cp = pltpu.make_async_copy(zbuf, rows_hbm.at[pl.ds(pl.multiple_of(blk * TM_E, TM_E), TM_E)], zsem)
        cp.start()
        cp.wait()

    slot = i % 2
    ring[slot] = x_ref[...]

    def push(t8, base, s):
        src = ring.at[slot].at[pl.ds(base, SUBLANES)].at[pl.ds(s, 1)]
        for k in range(TOP_K):
            d = dest_ref[0, 0, (t8 * SUBLANES + s) * TOP_K + k]
            pltpu.make_async_copy(src, rows_hbm.at[pl.ds(d, 1)], sem.at[slot]).start(priority=k % 2)
    _for_each_row(TM_D, push)

    def drain(which):
        def one(t8, base, s):
            for k in range(TOP_K):
                pltpu.make_async_copy(ring.at[0].at[pl.ds(0, 1)], rows_hbm.at[pl.ds(0, 1)], sem.at[which]).wait()
        _for_each_row(TM_D, one)

    @pl.when(i > 0)
    def _():
        drain(1 - slot)

    @pl.when(i == pl.num_programs(0) - 1)
    def _():
        drain(slot)


def _dispatch(pads, dest3, x):
    n = x.shape[0]
    assert n // TM_D >= N_EXP + (N_EBLOCKS - N_PAIRS // TM_E)
    return pl.pallas_call(
        _dispatch_body,
        grid_spec=pltpu.PrefetchScalarGridSpec(
            num_scalar_prefetch=3,
            grid=(n // TM_D,),
            in_specs=[pl.BlockSpec((1, 1, TM_D * TOP_K), lambda i, *_: (i, 0, 0), memory_space=pltpu.SMEM),
                      pl.BlockSpec((TM_D, D_MODEL), lambda i, *_: (i, 0))],
            out_specs=pl.BlockSpec(memory_space=pl.ANY),
            scratch_shapes=[pltpu.VMEM((2, TM_D, D_MODEL), F32), pltpu.VMEM((TM_E, D_MODEL), F32),
                            pltpu.SemaphoreType.DMA((2,)), pltpu.SemaphoreType.DMA(())]),
        out_shape=jax.ShapeDtypeStruct((ROWS_TOTAL, D_MODEL), F32),
        compiler_params=_cparams(("arbitrary",)),
        name="moe_dispatch",
    )(*pads, dest3, x)


def _experts_body(layer, bexp_ref, bval_ref, bnext_ref, x_ref, w1_hbm, b1_ref, w2_hbm, b2_ref, o_ref,
                  w1s, w2s, slot_ref, sem):
    i = pl.program_id(0)
    valid = bval_ref[i] != 0
    fresh = jnp.logical_or(i == 0, bexp_ref[i] != bexp_ref[jnp.maximum(i - 1, 0)])

    def weight_copies(e, slot):
        return (pltpu.make_async_copy(w1_hbm.at[layer, e], w1s.at[slot], sem.at[slot, 0]),
                pltpu.make_async_copy(w2_hbm.at[layer, e], w2s.at[slot], sem.at[slot, 1]))

    @pl.when(jnp.logical_and(valid, fresh))
    def _():
        @pl.when(i == 0)
        def _():
            slot_ref[0] = 1
            for cp in weight_copies(bexp_ref[i], 0):
                cp.start()
        slot = 1 - slot_ref[0]
        slot_ref[0] = slot
        for cp in weight_copies(bexp_ref[i], slot):
            cp.wait()

        @pl.when(bnext_ref[i] >= 0)
        def _():
            for cp in weight_copies(bnext_ref[i], 1 - slot):
                cp.start()

    @pl.when(jnp.logical_not(valid))
    def _():
        o_ref[...] = jnp.zeros(o_ref.shape, F32)

    @pl.when(valid)
    def _():
        slot = slot_ref[0]
        x = x_ref[...]
        glu = jnp.dot(x, w1s[slot, :, :D_EXP], preferred_element_type=F32) + b1_ref[:, :D_EXP]
        lin = jnp.dot(x, w1s[slot, :, D_EXP:], preferred_element_type=F32) + b1_ref[:, D_EXP:]
        glu = jnp.minimum(glu, SWIGLU_LIMIT)
        lin = jnp.clip(lin, -SWIGLU_LIMIT, SWIGLU_LIMIT)
        act = glu * jax.nn.sigmoid(SWIGLU_ALPHA * glu) * (lin + 1.0)
        o_ref[...] = jnp.dot(act, w2s[slot], preferred_element_type=F32) + b2_ref[...]


def _experts(layer, tables, x_rows, w1, b1, w2, b2):
    bspec = lambda w: pl.BlockSpec((None, None, 1, w), lambda i, be, bv, bn: (layer, be[i], 0, 0))
    rows = pl.BlockSpec((TM_E, D_MODEL), lambda i, be, bv, bn: (i, 0))
    hbm = pl.BlockSpec(memory_space=pl.ANY)
    return pl.pallas_call(
        functools.partial(_experts_body, layer),
        grid_spec=pltpu.PrefetchScalarGridSpec(
            num_scalar_prefetch=3,
            grid=(N_EBLOCKS,),
            in_specs=[rows, hbm, bspec(2 * D_EXP), hbm, bspec(D_MODEL)],
            out_specs=rows,
            scratch_shapes=[pltpu.VMEM((2, D_MODEL, 2 * D_EXP), F32), pltpu.VMEM((2, D_EXP, D_MODEL), F32),
                            pltpu.SMEM((1,), I32), pltpu.SemaphoreType.DMA((2, 2))]),
        out_shape=jax.ShapeDtypeStruct((ROWS_TOTAL, D_MODEL), F32),
        compiler_params=_cparams(("arbitrary",)),
        name="moe_experts",
    )(*tables, x_rows, w1, b1, w2, b2)


def _combine_ln_body(tail, n_tiles, dest_ref, dest1_ref, dest2_ref, gate_ref, x_ref, rows_hbm, g_ref, b_ref, *refs):
    if tail == "c_in":
        w_ref, bi_ref, lg_ref, lb_ref, o_ref, u_ref, v_ref, buf, sem = refs
    else:
        o_p_ref, o_s_ref, buf, sem = refs
    i = pl.program_id(0)

    def gather(d_ref, slot):
        def pull(t8, base, s):
            for k in range(TOP_K):
                d = d_ref[0, 0, (t8 * SUBLANES + s) * TOP_K + k]
                dst = buf.at[slot, k].at[pl.ds(base, SUBLANES)].at[pl.ds(s, 1)]
                pltpu.make_async_copy(rows_hbm.at[pl.ds(d, 1)], dst, sem.at[slot]).start(priority=k % 2)
        _for_each_row(TM_C, pull)

    def drain(slot):
        def one(t8, base, s):
            for k in range(TOP_K):
                pltpu.make_async_copy(rows_hbm.at[pl.ds(0, 1)], buf.at[0, 0].at[pl.ds(0, 1)], sem.at[slot]).wait()
        _for_each_row(TM_C, one)

    @pl.when(i == 0)
    def _():
        gather(dest_ref, 0)
        gather(dest1_ref, 1)

    slot = i % 3
    drain(slot)

    gates = gate_ref[...]
    y = buf[slot, 0] * gates[:, 0:1]
    for k in range(1, TOP_K):
        y = y + buf[slot, k] * gates[:, k:k + 1]
    res = _layer_norm(DEEPNORM_ALPHA * x_ref[...] + y, g_ref[...], b_ref[...])
    if tail == "c_in":
        o_ref[...] = res
        xb = res.astype(BF16)
        u_ref[...] = _gelu(jnp.dot(xb, w_ref[:, :D_C], preferred_element_type=F32) + bi_ref[:, :D_C])
        v = _gelu(jnp.dot(xb, w_ref[:, D_C:], preferred_element_type=F32) + bi_ref[:, D_C:])
        v_ref[...] = _layer_norm(v, lg_ref[...], lb_ref[...])

    nxt = (i + 2) % 3
    for t in range(TM_C):
        for k in range(TOP_K):
            src = rows_hbm.at[pl.ds(dest2_ref[0, 0, t * TOP_K + k], 1)]
            pltpu.make_async_copy(src, buf.at[nxt, k].at[pl.ds(t, 1)], sem.at[nxt]).start(priority=k % 2)

    if tail != "c_in":
        is_prompt = i < N_PROMPT // TM_C

        @pl.when(is_prompt)
        def _():
            o_p_ref[...] = res

        @pl.when(jnp.logical_not(is_prompt))
        def _():
            o_s_ref[...] = res

    @pl.when(i == n_tiles - 1)
    def _():
        drain((i + 1) % 3)
        drain(nxt)


def _combine_ln(dest3, gates, x, out_rows, g, b, c_in=None):
    n = x.shape[0]
    n_tiles = n // TM_C
    full = lambda a: pl.BlockSpec(a.shape, lambda i: (0,) * a.ndim)
    row = pl.BlockSpec((TM_C, D_MODEL), lambda i: (i, 0))
    dest_spec = lambda f: pl.BlockSpec((1, 1, TM_C * TOP_K), lambda i: (f(i), 0, 0), memory_space=pltpu.SMEM)
    assert n_tiles >= 2
    in_specs = [dest_spec(lambda i: i), dest_spec(lambda i: jnp.minimum(i + 1, n_tiles - 1)),
                dest_spec(lambda i: jnp.minimum(i + 2, n_tiles - 1)),
                pl.BlockSpec((TM_C, LANES), lambda i: (i, 0)), row, pl.BlockSpec(memory_space=pl.ANY), full(g), full(b)]
    args = [dest3, dest3, dest3, gates, x, out_rows, g, b]
    if c_in is not None:
        in_specs += [full(a) for a in c_in]
        args += list(c_in)
        out_specs = [row, row, row]
        out_shape = [jax.ShapeDtypeStruct((n, D_MODEL), F32)] * 3
    else:
        per_seq = SEQ // TM_C
        last = N_PROMPT // TM_C - 1
        out_specs = [pl.BlockSpec((None, TM_C, D_MODEL),
                                  lambda i: (jnp.minimum(i, last) // per_seq, jnp.minimum(i, last) % per_seq, 0)),
                     pl.BlockSpec((TM_C, D_MODEL), lambda i: (jnp.maximum(i - last - 1, 0), 0))]
        out_shape = [jax.ShapeDtypeStruct((BATCH, SEQ, D_MODEL), F32),
                     jax.ShapeDtypeStruct((N_SAMPLE, D_MODEL), F32)]
    return pl.pallas_call(
        functools.partial(_combine_ln_body, "c_in" if c_in is not None else "final", n_tiles),
        grid=(n_tiles,),
        in_specs=in_specs,
        out_specs=out_specs,
        out_shape=out_shape,
        scratch_shapes=[pltpu.VMEM((3, TOP_K, TM_C, D_MODEL), F32), pltpu.SemaphoreType.DMA((3,))],
        compiler_params=_cparams(("arbitrary",)),
        name="moe_combine_ln",
    )(*args)


def _positions(enc, counts):
    nb = (counts + TM_E - 1) // TM_E
    cum = jnp.cumsum(nb)
    first_blk = cum - nb
    experts = jnp.arange(N_EXP, dtype=I32)
    pair_hot = ((enc >> RANK_BITS)[:, :, None] == experts[None, None, :]).astype(I32)
    dest = jnp.sum(pair_hot * (first_blk * TM_E)[None, None, :], axis=2) + (enc & ((1 << RANK_BITS) - 1))
    n_used = cum[-1]
    blk = jnp.arange(N_EBLOCKS, dtype=I32)
    exp = jnp.minimum(jnp.sum((cum[None, :] <= blk[:, None]).astype(I32), axis=1), N_EXP - 1)
    later = lax.cummin(jnp.where(counts > 0, experts, N_EXP)[::-1])[::-1]
    nxt = jnp.concatenate([later[1:], jnp.full((1,), N_EXP, I32)])
    nxt = jnp.where(nxt >= N_EXP, -1, nxt)
    bnext = jnp.sum((exp[:, None] == experts[None, :]).astype(I32) * nxt[None, :], axis=1)
    pads = (first_blk * TM_E + counts, nb * TM_E - counts, n_used.reshape(1))
    return dest, (exp, (blk < n_used).astype(I32), bnext.astype(I32)), pads


def _moe_post_norm(layer, x, gates, enc, counts, w1, b1, w2, b2, g, b, c_in=None):
    dest4, tables, pads = _positions(enc[:, :TOP_K], counts[0, :N_EXP].astype(I32))
    x_rows = _dispatch(pads, dest4.reshape(N_TOK // TM_D, 1, TM_D * TOP_K), x)
    out_rows = _experts(layer, tables, x_rows, w1, b1.reshape(b1.shape[:2] + (1, 2 * D_EXP)),
                        w2, b2.reshape(b2.shape[:2] + (1, D_MODEL)))
    return _combine_ln(dest4.reshape(N_TOK // TM_C, 1, TM_C * TOP_K), gates, x, out_rows,
                       g.reshape(1, D_MODEL), b.reshape(1, D_MODEL), c_in=c_in)


def _router_weights(w_r, b_r):
    wh = w_r.astype(BF16)
    wl = (w_r - wh.astype(F32)).astype(BF16)
    pad = lambda a: jnp.pad(a, ((0, 0), (0, LANES - N_EXP)))
    return jnp.concatenate([pad(wh), pad(wl)], axis=1), jnp.pad(b_r, (0, LANES - N_EXP)).reshape(1, LANES)


def _tail8(state, keep):
    return jnp.pad(state, ((0, 0), (SUBLANES - keep, 0), (0, 0)))


def kernel(x_prompt, x_sample, state_conv_a, state_conv_qkv, state_delta, ab_w_in, ab_conv_a, ab_conv_qkv,
           ab_a_log, ab_dt_bias, ab_norm_g, ab_w_out, c_w_in, c_b_in, c_ln_g, c_ln_b, c_w_s, c_b_s, c_w_out,
           moe_w_router, moe_b_router, moe_w1, moe_b1, moe_w2, moe_b2, ln_g, ln_b):
    xp, xs = x_prompt.reshape(N_PROMPT, D_MODEL), x_sample.reshape(N_SAMPLE, D_MODEL)
    lnrow = lambda layer, j: (ln_g[layer, j].reshape(1, D_MODEL), ln_b[layer, j].reshape(1, D_MODEL))
    ri = jnp.arange(TM)
    tri = (ri[:, None] > ri[None, :]).astype(BF16)

    w_in = ab_w_in[0]
    w_main = w_in[:, :W_MAIN].astype(BF16)
    w_ab = jnp.pad(w_in[:, W_MAIN:], ((0, 0), (0, LANES - 2 * H_B))).astype(BF16)
    alog_row = jnp.pad(ab_a_log[0], (0, LANES - H_B)).reshape(1, LANES)
    dtb_row = jnp.pad(ab_dt_bias[0], (0, LANES - H_B)).reshape(1, LANES)
    bg, u, qkv, z, gb = _proj_ab(xp, xs, w_main, w_ab, alog_row, dtb_row)

    ng = ab_norm_g[0].reshape(1, DV)
    ycat = jnp.zeros((N_TOK, D_MODEL), BF16)
    gbr_p = _group_rows(gb[:N_PROMPT], BATCH, SEQ, DN_CHUNK, 1, GDN_CPS)
    ycat, pq8, pu8, p_delta = _gdn(
        qkv, u, bg, z, gbr_p,
        jnp.zeros((BATCH, SUBLANES, D_QKV), F32), jnp.zeros((BATCH, SUBLANES, D_A), F32),
        jnp.zeros((BATCH, H_B, DK, DV), F32), ab_conv_qkv[0], ab_conv_a[0], ng, ycat,
        chunk=DN_CHUNK, bb=1, cps=GDN_CPS, n_seq=BATCH, t_len=SEQ, row_block0=0, phased=True)
    bb_s = 16
    gbr_s = _group_rows(gb[N_PROMPT:], DEC_BATCH, DEC_SEQ, DEC_SEQ, bb_s, 1)
    ycat, sq8, su8, s_delta = _gdn(
        qkv, u, bg, z, gbr_s,
        _tail8(state_conv_qkv[0], CONV_B - 1), _tail8(state_conv_a[0], CONV_A - 1), state_delta[0],
        ab_conv_qkv[0], ab_conv_a[0], ng, ycat,
        chunk=DEC_SEQ, bb=bb_s, cps=1, n_seq=DEC_BATCH, t_len=DEC_SEQ, row_block0=N_PROMPT // (bb_s * DEC_SEQ))
    wr0, br0 = _router_weights(moe_w_router[0], moe_b_router[0])
    x, gates, dest, counts = _mm_res_ln(ycat, xp, xs, ab_w_out[0].astype(BF16), *lnrow(0, 0), wr0, br0, tri)
    c_in = (c_w_in[0].astype(BF16), c_b_in[0].reshape(1, 2 * D_C), c_ln_g[0].reshape(1, D_C), c_ln_b[0].reshape(1, D_C))
    x, uc, vc = _moe_post_norm(0, x, gates, dest, counts, moe_w1, moe_b1, moe_w2, moe_b2, ln_g[0, 1], ln_b[0, 1],
                               c_in=c_in)
    ws = c_w_s[0]
    reps = C_CHUNK // DEC_SEQ
    wmix2 = jnp.stack([ws, jnp.tile(ws[:, :DEC_SEQ, :DEC_SEQ], (1, reps, reps))])
    rc = jnp.arange(C_CHUNK)
    tril = rc[:, None] >= rc[None, :]
    mask2 = jnp.stack([tril, tril & ((rc[:, None] // DEC_SEQ) == (rc[None, :] // DEC_SEQ))]).astype(F32)
    bias_p = jnp.repeat(c_b_s[0].T, D_C // C_GROUPS, axis=1)
    bias2 = jnp.stack([bias_p, jnp.tile(bias_p[:DEC_SEQ], (reps, 1))])
    wr1, br1 = _router_weights(moe_w_router[1], moe_b_router[1])
    x, gates, dest, counts = _c_out(uc, vc, x, wmix2, mask2, bias2, c_w_out[0].astype(BF16), *lnrow(1, 0),
                                    wr1, br1, tri)
    y_prompt, y_sample = _moe_post_norm(1, x, gates, dest, counts, moe_w1, moe_b1, moe_w2, moe_b2,
                                        ln_g[1, 1], ln_b[1, 1])
    y_sample = y_sample.reshape(DEC_BATCH, DEC_SEQ, D_MODEL)
    ka, kq = CONV_A - 1, CONV_B - 1
    return (y_prompt, y_sample,
            pu8[None, :, SUBLANES - ka:], pq8[None, :, SUBLANES - kq:], p_delta[None],
            su8[None, :, SUBLANES - ka:], sq8[None, :, SUBLANES - kq:], s_delta[None],
            vc[N_PROMPT:].reshape(1, DEC_BATCH, DEC_SEQ, D_C))
```
